```python
import jax, jax.numpy as jnp
from jax import lax
import numpy as np

D_MODEL = 1024
BATCH = 8
SEQ = 2048
DEPTH = 4
DEC_BATCH = 128
DEC_SEQ = 8
PAST_LEN = 16384
PAGE_SIZE = 128

F32 = jnp.float32
EPS = 1e-6
N_EVEN = (DEPTH + 1) // 2
N_ODD = DEPTH // 2
D_FF = 2816
GLA_H = 4
GLA_DK = D_MODEL // 16
GLA_DV = D_MODEL // 8
GLA_LOWRANK = 16
GLA_TAU = 16.0
GLA_CHUNK = 64
GLA_QK_W = GLA_H * GLA_DK
GLA_V_W = GLA_H * GLA_DV
GMLP_G = 4
GMLP_DG = D_MODEL // 8
GMLP_W = GMLP_G * GMLP_DG
GMLP_CHUNK = 128
AB_SPLITS = (GLA_QK_W, 2 * GLA_QK_W, 2 * GLA_QK_W + GLA_V_W, 2 * GLA_QK_W + 2 * GLA_V_W,
             2 * GLA_QK_W + 2 * GLA_V_W + GLA_LOWRANK, 2 * GLA_QK_W + 2 * GLA_V_W + GLA_LOWRANK + GMLP_W)
AB_IN = 2 * GLA_QK_W + 2 * GLA_V_W + GLA_LOWRANK + 2 * GMLP_W
AB_MIX = GLA_V_W + GMLP_W
ML_INNER = 2 * D_MODEL
ML_H = 4
ML_DH = ML_INNER // ML_H
ML_CONV = 4
ML_BLOCK = 4
ML_NB = ML_INNER // ML_BLOCK
ML_CHUNK = 128

kernel_name = 'hybrid_gla_gmlp_mlstm_macaron_step'


def _rms_norm(x, g):
    xf = x.astype(F32)
    y = xf * lax.rsqrt(jnp.mean(xf * xf, axis=-1, keepdims=True) + EPS)
    return (y * g.astype(F32)).astype(x.dtype)


def _head_layer_norm(x, g):
    xf = x.astype(F32)
    mu = jnp.mean(xf, axis=-1, keepdims=True)
    xc = xf - mu
    var = jnp.mean(xc * xc, axis=-1, keepdims=True)
    return (xc * lax.rsqrt(var + EPS) * g.astype(F32)).astype(x.dtype)


def _modulate(x, shift, scale):
    return x * (1.0 + scale[:, None, :]) + shift[:, None, :]


def _swiglu(x, w_gate, w_up, w_down):
    return (jax.nn.silu(x @ w_gate) * (x @ w_up)) @ w_down


def _to_chunks(a, L):
    B, H, T = a.shape[:3]
    return jnp.moveaxis(a.reshape(B, H, T // L, L, *a.shape[3:]), 2, 0)


def _from_chunks(a):
    a = jnp.moveaxis(a, 0, 2)
    B, H, NC, L = a.shape[:4]
    return a.reshape(B, H, NC * L, *a.shape[4:])


def _gla_chunk(S, inp):
    q, k, v, log_a = inp
    L = q.shape[2]
    causal = jnp.tril(jnp.ones((L, L), dtype=bool))
    b = jnp.cumsum(log_a.astype(F32), axis=2)
    rel = jnp.where(causal[:, :, None], b[:, :, :, None, :] - b[:, :, None, :, :], -jnp.inf)
    scores = jnp.einsum('bhtd,bhsd,bhtsd->bhts', q, k, jnp.exp(rel))
    o = jnp.einsum('bhts,bhsv->bhtv', scores, v) + jnp.einsum('bhtd,bhdv->bhtv', q * jnp.exp(b), S)
    b_last = b[:, :, -1:, :]
    S_new = jnp.exp(b_last)[:, :, 0, :, None] * S + jnp.einsum('bhsd,bhsv->bhdv', k * jnp.exp(b_last - b), v)
    return S_new.astype(S.dtype), o


def _gla_scan(S, q, k, v, log_a):
    T = q.shape[2]
    L = GLA_CHUNK if T % GLA_CHUNK == 0 else T
    S, o = lax.scan(_gla_chunk, S, (_to_chunks(q, L), _to_chunks(k, L), _to_chunks(v, L), _to_chunks(log_a, L)))
    return S, _from_chunks(o)


def _spatial_gate(u, vb, ws, bs):
    B, T = vb.shape[:2]
    L = GMLP_CHUNK if T % GMLP_CHUNK == 0 else T
    w = jnp.tril(ws[:, :L, :L])
    vc = vb.reshape(B, T // L, L, GMLP_G, GMLP_DG)
    z = jnp.einsum('gts,bnsgc->bntgc', w, vc) + bs[:, :L].T[None, None, :, :, None]
    return u * z.reshape(B, T, GMLP_G, GMLP_DG)


def _ab_mixer(h, S, w_in, w_a2, b_a, g_norm, v_norm, ws, bs, w_out):
    B, T, _ = h.shape
    q, k, vg, g, a_lr, u, vb = jnp.split(h @ w_in, AB_SPLITS, axis=-1)
    log_a = jax.nn.log_sigmoid((a_lr @ w_a2 + b_a).astype(F32)) / GLA_TAU

    def heads(a, d):
        return a.reshape(B, T, GLA_H, d).transpose(0, 2, 1, 3)

    S_new, o = _gla_scan(S, heads(q, GLA_DK) * GLA_DK ** -0.5, heads(k, GLA_DK), heads(vg, GLA_DV), heads(log_a, GLA_DK))
    o = _rms_norm(o.transpose(0, 2, 1, 3).astype(h.dtype), g_norm).reshape(B, T, GLA_V_W) * jax.nn.silu(g)
    vb = _rms_norm(vb.reshape(B, T, GMLP_G, GMLP_DG), v_norm)
    yb = _spatial_gate(u.reshape(B, T, GMLP_G, GMLP_DG), vb, ws, bs).reshape(B, T, GMLP_W)
    out = jnp.concatenate([o, yb], axis=-1) @ w_out
    return out, S_new, vb.reshape(B, T, GMLP_W)


def _mlstm_chunk(carry, inp):
    C, n, m = carry
    q, k, v, ig, lf = inp
    L = q.shape[2]
    causal = jnp.tril(jnp.ones((L, L), dtype=bool))
    F = jnp.cumsum(lf.astype(F32), axis=-1)
    igf = ig.astype(F32)
    mf = m.astype(F32)
    log_d = jnp.where(causal, F[..., :, None] - F[..., None, :] + igf[..., None, :], -jnp.inf)
    log_inter = F + mf[..., None]
    m_t = jnp.maximum(log_inter, jnp.max(log_d, axis=-1))
    d = jnp.exp(log_d - m_t[..., None])
    w_inter = jnp.exp(log_inter - m_t)
    s = jnp.einsum('bhtd,bhsd->bhts', q, k) * d
    num = jnp.einsum('bhts,bhsv->bhtv', s, v) + w_inter[..., None] * jnp.einsum('bhvd,bhtd->bhtv', C, q)
    den = jnp.sum(s, axis=-1) + w_inter * jnp.einsum('bhd,bhtd->bht', n, q)
    h = num / jnp.maximum(jnp.abs(den), jnp.exp(-m_t))[..., None]
    m_new = m_t[..., -1]
    w_rows = jnp.exp(F[..., -1:] - F + igf - m_new[..., None])
    decay_prev = jnp.exp(F[..., -1] + mf - m_new)
    C_new = decay_prev[..., None, None] * C + jnp.einsum('bhsv,bhsd->bhvd', v * w_rows[..., None], k)
    n_new = decay_prev[..., None] * n + jnp.einsum('bhs,bhsd->bhd', w_rows, k)
    return (C_new.astype(C.dtype), n_new.astype(n.dtype), m_new.astype(m.dtype)), h


def _mlstm_scan(C, n, m, q, k, v, ig, lf):
    T = q.shape[2]
    L = ML_CHUNK if T % ML_CHUNK == 0 else T
    (C, n, m), h = lax.scan(_mlstm_chunk, (C, n, m),
                            (_to_chunks(q, L), _to_chunks(k, L), _to_chunks(v, L), _to_chunks(ig, L), _to_chunks(lf, L)))
    return C, n, m, _from_chunks(h)


def _blockdiag(x, w):
    B, T, _ = x.shape
    return jnp.einsum('btni,nio->btno', x.reshape(B, T, ML_NB, ML_BLOCK), w).reshape(B, T, ML_INNER)


def _mlstm_mixer(h, C, n, m, conv_buf, w_in, conv_w, conv_b, wq, wk, wv, w_gates, b_gates, g_norm, skip, w_out):
    B, T, _ = h.shape
    xm, z = jnp.split(h @ w_in, 2, axis=-1)
    x_ext = jnp.concatenate([conv_buf.astype(xm.dtype), xm], axis=1)
    new_buf = x_ext[:, T:]
    xc = jax.nn.silu(sum(x_ext[:, j:j + T] * conv_w[j] for j in range(ML_CONV)) + conv_b)
    q = _blockdiag(xc, wq)
    k = _blockdiag(xc, wk)
    v = _blockdiag(xm, wv)
    gates = (jnp.concatenate([q, k, v], axis=-1) @ w_gates + b_gates).astype(F32)
    ig = gates[..., :ML_H].transpose(0, 2, 1)
    lf = jax.nn.log_sigmoid(gates[..., ML_H:]).transpose(0, 2, 1)

    def heads(a):
        return a.reshape(B, T, ML_H, ML_DH).transpose(0, 2, 1, 3)

    C, n, m, hh = _mlstm_scan(C, n, m, heads(q), heads(k) * ML_DH ** -0.5, heads(v), ig, lf)
    hh = hh.transpose(0, 2, 1, 3).astype(h.dtype)
    hn = _head_layer_norm(hh, g_norm.reshape(ML_H, ML_DH)).reshape(B, T, ML_INNER)
    out = (hn + skip * xc) * jax.nn.silu(z)
    return out @ w_out, C, n, m, new_buf


def _trunk(x, c, gla_S, ml_C, ml_n, ml_m, ml_conv, W):
    s_out, v_out, c_out, n_out, m_out, conv_out = [], [], [], [], [], []
    cs = jax.nn.silu(c)
    for l in range(DEPTH):
        mod = cs @ W['ada_w'][l] + W['ada_b'][l]
        sh1, sc1, g1, sh2, sc2, g2, sh3, sc3, g3 = jnp.split(mod, 9, axis=-1)
        h = _modulate(_rms_norm(x, W['ffn_norm'][l, 0]), sh1, sc1)
        x = x + 0.5 * g1[:, None, :] * _swiglu(h, W['ffn_w_gate'][l, 0], W['ffn_w_up'][l, 0], W['ffn_w_down'][l, 0])
        h = _modulate(_rms_norm(x, W['mix_norm'][l]), sh2, sc2)
        if l % 2 == 0:
            e = l // 2
            mix, s_new, v_rows = _ab_mixer(h, gla_S[e], W['ab_w_in'][e], W['gla_w_a2'][e], W['gla_b_a'][e],
                                           W['gla_norm'][e], W['gmlp_norm'][e], W['gmlp_ws'][e], W['gmlp_bs'][e],
                                           W['ab_w_out'][e])
            s_out.append(s_new)
            v_out.append(v_rows)
        else:
            o = l // 2
            mix, C_new, n_new, m_new, buf_new = _mlstm_mixer(
                h, ml_C[o], ml_n[o], ml_m[o], ml_conv[o], W['ml_w_in'][o], W['ml_conv_w'][o], W['ml_conv_b'][o],
                W['ml_wq'][o], W['ml_wk'][o], W['ml_wv'][o], W['ml_w_gates'][o], W['ml_b_gates'][o],
                W['ml_norm'][o], W['ml_skip'][o], W['ml_w_out'][o])
            c_out.append(C_new)
            n_out.append(n_new)
            m_out.append(m_new)
            conv_out.append(buf_new)
        x = x + g2[:, None, :] * mix
        h = _modulate(_rms_norm(x, W['ffn_norm'][l, 1]), sh3, sc3)
        x = x + 0.5 * g3[:, None, :] * _swiglu(h, W['ffn_w_gate'][l, 1], W['ffn_w_up'][l, 1], W['ffn_w_down'][l, 1])
    shf, scf = jnp.split(cs @ W['final_ada_w'] + W['final_ada_b'], 2, axis=-1)
    y = _modulate(_rms_norm(x, W['final_norm']), shf, scf)
    return (y, jnp.stack(s_out), jnp.stack(v_out), jnp.stack(c_out), jnp.stack(n_out), jnp.stack(m_out),
            jnp.stack(conv_out))


def setup_inputs(seed: int = 0) -> dict:
    key = jax.random.key(seed)
    ks = iter(jax.random.split(key, 48))
    D = D_MODEL

    def nrm(shape, scale):
        return jax.random.normal(next(ks), shape, F32) * scale

    def gain(shape):
        return 1.0 + nrm(shape, 0.02)

    inp = {}
    inp['x_prompt'] = nrm((BATCH, SEQ, D), 1.0)
    inp['x_sample'] = nrm((DEC_BATCH, DEC_SEQ, D), 1.0)
    inp['c_prompt'] = nrm((BATCH, D), 1.0)
    inp['c_sample'] = nrm((DEC_BATCH, D), 1.0)
    inp['state_gla_S'] = nrm((N_EVEN, DEC_BATCH, GLA_H, GLA_DK, GLA_DV), GLA_DK ** -0.5)
    inp['state_mlstm_C'] = nrm((N_ODD, DEC_BATCH, ML_H, ML_DH, ML_DH), ML_DH ** -0.5)
    inp['state_mlstm_n'] = nrm((N_ODD, DEC_BATCH, ML_H, ML_DH), ML_DH ** -0.5)
    inp['state_mlstm_m'] = nrm((N_ODD, DEC_BATCH, ML_H), 1.0)
    inp['state_mlstm_conv'] = nrm((N_ODD, DEC_BATCH, ML_CONV - 1, ML_INNER), 1.0)
    inp['ada_w'] = nrm((DEPTH, D, 9 * D), 0.5 * D ** -0.5)
    inp['ada_b'] = nrm((DEPTH, 9 * D), 0.02)
    inp['ffn_norm'] = gain((DEPTH, 2, D))
    inp['ffn_w_gate'] = nrm((DEPTH, 2, D, D_FF), D ** -0.5)
    inp['ffn_w_up'] = nrm((DEPTH, 2, D, D_FF), D ** -0.5)
    inp['ffn_w_down'] = nrm((DEPTH, 2, D_FF, D), D_FF ** -0.5)
    inp['mix_norm'] = gain((DEPTH, D))
    inp['ab_w_in'] = nrm((N_EVEN, D, AB_IN), D ** -0.5)
    inp['gla_w_a2'] = nrm((N_EVEN, GLA_LOWRANK, GLA_QK_W), GLA_LOWRANK ** -0.5)
    inp['gla_b_a'] = nrm((N_EVEN, GLA_QK_W), 0.02)
    inp['gla_norm'] = gain((N_EVEN, GLA_DV))
    inp['gmlp_norm'] = gain((N_EVEN, GMLP_DG))
    inp['gmlp_ws'] = nrm((N_EVEN, GMLP_G, GMLP_CHUNK, GMLP_CHUNK), GMLP_CHUNK ** -0.5)
    inp['gmlp_bs'] = gain((N_EVEN, GMLP_G, GMLP_CHUNK))
    inp['ab_w_out'] = nrm((N_EVEN, AB_MIX, D), AB_MIX ** -0.5)
    inp['ml_w_in'] = nrm((N_ODD, D, 2 * ML_INNER), D ** -0.5)
    inp['ml_conv_w'] = nrm((N_ODD, ML_CONV, ML_INNER), ML_CONV ** -0.5)
    inp['ml_conv_b'] = nrm((N_ODD, ML_INNER), 0.02)
    inp['ml_wq'] = nrm((N_ODD, ML_NB, ML_BLOCK, ML_BLOCK), ML_BLOCK ** -0.5)
    inp['ml_wk'] = nrm((N_ODD, ML_NB, ML_BLOCK, ML_BLOCK), ML_BLOCK ** -0.5)
    inp['ml_wv'] = nrm((N_ODD, ML_NB, ML_BLOCK, ML_BLOCK), ML_BLOCK ** -0.5)
    inp['ml_w_gates'] = nrm((N_ODD, 3 * ML_INNER, 2 * ML_H), (3 * ML_INNER) ** -0.5)
    inp['ml_b_gates'] = jnp.concatenate(
        [nrm((N_ODD, ML_H), 0.1), jnp.linspace(3.0, 6.0, ML_H, dtype=F32)[None, :] + nrm((N_ODD, ML_H), 0.1)], axis=-1)
    inp['ml_norm'] = gain((N_ODD, ML_INNER))
    inp['ml_skip'] = gain((N_ODD, ML_INNER))
    inp['ml_w_out'] = nrm((N_ODD, ML_INNER, D), ML_INNER ** -0.5)
    inp['final_norm'] = gain((D,))
    inp['final_ada_w'] = nrm((D, 2 * D), 0.5 * D ** -0.5)
    inp['final_ada_b'] = nrm((2 * D,), 0.02)
    return inp


def reference(x_prompt, x_sample, c_prompt, c_sample, state_gla_S, state_mlstm_C, state_mlstm_n, state_mlstm_m,
              state_mlstm_conv, ada_w, ada_b, ffn_norm, ffn_w_gate, ffn_w_up, ffn_w_down, mix_norm, ab_w_in,
              gla_w_a2, gla_b_a, gla_norm, gmlp_norm, gmlp_ws, gmlp_bs, ab_w_out, ml_w_in, ml_conv_w, ml_conv_b,
              ml_wq, ml_wk, ml_wv, ml_w_gates, ml_b_gates, ml_norm, ml_skip, ml_w_out, final_norm, final_ada_w,
              final_ada_b):
    W = {'ada_w': ada_w, 'ada_b': ada_b, 'ffn_norm': ffn_norm, 'ffn_w_gate': ffn_w_gate, 'ffn_w_up': ffn_w_up,
         'ffn_w_down': ffn_w_down, 'mix_norm': mix_norm, 'ab_w_in': ab_w_in, 'gla_w_a2': gla_w_a2,
         'gla_b_a': gla_b_a, 'gla_norm': gla_norm, 'gmlp_norm': gmlp_norm, 'gmlp_ws': gmlp_ws,
         'gmlp_bs': gmlp_bs, 'ab_w_out': ab_w_out, 'ml_w_in': ml_w_in, 'ml_conv_w': ml_conv_w,
         'ml_conv_b': ml_conv_b, 'ml_wq': ml_wq, 'ml_wk': ml_wk, 'ml_wv': ml_wv, 'ml_w_gates': ml_w_gates,
         'ml_b_gates': ml_b_gates, 'ml_norm': ml_norm, 'ml_skip': ml_skip, 'ml_w_out': ml_w_out,
         'final_norm': final_norm, 'final_ada_w': final_ada_w, 'final_ada_b': final_ada_b}
    dt = x_prompt.dtype
    B = x_prompt.shape[0]
    z_gla = jnp.zeros((N_EVEN, B, GLA_H, GLA_DK, GLA_DV), dt)
    z_C = jnp.zeros((N_ODD, B, ML_H, ML_DH, ML_DH), dt)
    z_n = jnp.zeros((N_ODD, B, ML_H, ML_DH), dt)
    z_m = jnp.zeros((N_ODD, B, ML_H), dt)
    z_conv = jnp.zeros((N_ODD, B, ML_CONV - 1, ML_INNER), dt)
    y_p, s_p, _, c_p, n_p, m_p, cv_p = _trunk(x_prompt, c_prompt, z_gla, z_C, z_n, z_m, z_conv, W)
    y_s, s_s, v_s, c_s, n_s, m_s, cv_s = _trunk(x_sample, c_sample, state_gla_S, state_mlstm_C, state_mlstm_n,
                                                state_mlstm_m, state_mlstm_conv, W)
    return (y_p, y_s, s_p, s_s, v_s, c_p, c_s, n_p, n_s, m_p, m_s, cv_p, cv_s)
```

```python
import functools

import jax
import jax.numpy as jnp
from jax import lax
from jax.experimental import pallas as pl
from jax.experimental.pallas import tpu as pltpu

F32 = jnp.float32
BF16 = jnp.bfloat16
EPS = 1e-6

D_MODEL = 1024
DEPTH = 4
D_FF = 2816
GLA_H = 4
GLA_DK = 64
GLA_DV = 128
GLA_LOWRANK = 16
GLA_TAU = 16.0
GLA_CHUNK = 64
GMLP_G = 4
GMLP_DG = 128
GMLP_W = GMLP_G * GMLP_DG
GMLP_CHUNK = 128
ML_INNER = 2 * D_MODEL
ML_H = 4
ML_DH = ML_INNER // ML_H
ML_CONV = 4
ML_BLOCK = 4
ML_CHUNK = 128

LANES = 128
SUBLANES = 8
MXU_DIM = 256
VMEM_LIMIT_BYTES = 56 * 1024 * 1024

HP = GLA_H * LANES
AB_Q, AB_K, AB_V, AB_G = 0, HP, 2 * HP, 3 * HP
AB_A = 4 * HP
AB_U = AB_A + LANES
AB_VB = AB_U + GMLP_W
AB_COLS = AB_VB + GMLP_W


def _dot(a, b):
    return jnp.dot(a.astype(BF16), b.astype(BF16), preferred_element_type=F32)


def _dot_nt(a, b):
    return lax.dot_general(a.astype(BF16), b.astype(BF16), (((1,), (1,)), ((), ())), preferred_element_type=F32)


def _dot_tn(a, b):
    return lax.dot_general(a.astype(BF16), b.astype(BF16), (((0,), (0,)), ((), ())), preferred_element_type=F32)


def _dot01(sel, x):
    hi = x.astype(BF16)
    r1 = x - hi.astype(F32)
    mid = r1.astype(BF16)
    lo = (r1 - mid.astype(F32)).astype(BF16)
    return (jnp.dot(sel, hi, preferred_element_type=F32) + jnp.dot(sel, mid, preferred_element_type=F32)
            + jnp.dot(sel, lo, preferred_element_type=F32))


def _rms(x, g):
    return x * lax.rsqrt(jnp.mean(x * x, axis=-1, keepdims=True) + EPS) * g


def _silu(x):
    return x * jax.nn.sigmoid(x)


def _log_sigmoid(x):
    return jnp.minimum(x, 0.0) - jnp.log1p(jnp.exp(-jnp.abs(x)))


def _causal(n):
    row = lax.broadcasted_iota(jnp.int32, (n, n), 0)
    col = lax.broadcasted_iota(jnp.int32, (n, n), 1)
    return row >= col, row == col


def _col_to_row(col, eye):
    return jnp.sum(jnp.where(eye, col, 0.0), axis=0, keepdims=True)


def _tok_spec(tm, w):
    return pl.BlockSpec((None, tm, w), lambda g, i: (g, i, 0))


def _mod_spec(mod, l, j, tm):
    if mod.shape[3] == 1:
        return pl.BlockSpec((None, None, None, 1, D_MODEL), lambda g, i: (l, j, g, 0, 0))
    return pl.BlockSpec((None, None, None, tm, D_MODEL), lambda g, i: (l, j, g, i, 0))


def _layer_spec(a, *lead):
    n = len(lead)
    shape = (None,) * n + tuple(a.shape[n:])
    zeros = (0,) * (a.ndim - n)
    return pl.BlockSpec(shape, lambda g, i: tuple(lead) + zeros, pipeline_mode=pl.Buffered(1))


def _params(*sem):
    return pltpu.CompilerParams(dimension_semantics=sem, vmem_limit_bytes=VMEM_LIMIT_BYTES)


def _ada_kernel(c_ref, w_ref, b_ref, o_ref):
    cs = _silu(c_ref[...])
    o_ref[...] = _dot(cs, w_ref[...]) + b_ref[...]


def _ada(c, w, b):
    n_l, _, width = w.shape
    n_j = width // D_MODEL
    r = c.shape[0]
    return pl.pallas_call(
        _ada_kernel,
        grid=(n_l, n_j),
        in_specs=[pl.BlockSpec((r, D_MODEL), lambda l, j: (0, 0)),
                  pl.BlockSpec((None, D_MODEL, D_MODEL), lambda l, j: (l, 0, j)),
                  pl.BlockSpec((None, None, 1, D_MODEL), lambda l, j: (l, j, 0, 0))],
        out_specs=pl.BlockSpec((None, None, r, D_MODEL), lambda l, j: (l, j, 0, 0)),
        out_shape=jax.ShapeDtypeStruct((n_l, n_j, r, D_MODEL), F32),
        compiler_params=_params("arbitrary", "arbitrary"),
        name="ada",
    )(c, w, b.reshape(n_l, n_j, 1, D_MODEL))


FFN_CHUNK = 2 * MXU_DIM


def _ffn_kernel(x_ref, sh_ref, sc_ref, gt_ref, nw_ref, wg_ref, wu_ref, wd_ref, o_ref):
    x = x_ref[...]
    h = (_rms(x, nw_ref[...]) * (1.0 + sc_ref[...]) + sh_ref[...]).astype(BF16)
    acc = None
    for f0 in range(0, D_FF, FFN_CHUNK):
        f1 = min(f0 + FFN_CHUNK, D_FF)
        g = jnp.dot(h, wg_ref[:, f0:f1], preferred_element_type=F32)
        u = jnp.dot(h, wu_ref[:, f0:f1], preferred_element_type=F32)
        p = _dot(_silu(g) * u, wd_ref[f0:f1, :])
        acc = p if acc is None else acc + p
    o_ref[...] = x + (0.5 * gt_ref[...]) * acc


def _ffn(x, mod, l, s, w, tm):
    g_n, t_n, _ = x.shape
    j0 = 6 * s
    return pl.pallas_call(
        _ffn_kernel,
        grid=(g_n, t_n // tm),
        in_specs=[_tok_spec(tm, D_MODEL), _mod_spec(mod, l, j0, tm), _mod_spec(mod, l, j0 + 1, tm),
                  _mod_spec(mod, l, j0 + 2, tm), _layer_spec(w["ffn_norm"], l, s),
                  _layer_spec(w["ffn_w_gate"], l, s), _layer_spec(w["ffn_w_up"], l, s),
                  _layer_spec(w["ffn_w_down"], l, s)],
        out_specs=_tok_spec(tm, D_MODEL),
        out_shape=jax.ShapeDtypeStruct(x.shape, F32),
        compiler_params=_params("parallel", "parallel"),
        name="ffn",
    )(x, mod, mod, mod, w["ffn_norm"], w["ffn_w_gate"], w["ffn_w_up"], w["ffn_w_down"])


def _ab_in_kernel(x_ref, sh_ref, sc_ref, nw_ref, w_ref, wa2_ref, ba_ref, vn_ref,
                  q_ref, k_ref, v_ref, g_ref, la_ref, u_ref, vb_ref):
    h = (_rms(x_ref[...], nw_ref[...]) * (1.0 + sc_ref[...]) + sh_ref[...]).astype(BF16)

    def seg(c0, width):
        return jnp.dot(h, w_ref[:, c0:c0 + width], preferred_element_type=F32)

    q_ref[...] = seg(AB_Q, HP) * GLA_DK ** -0.5
    k_ref[...] = seg(AB_K, HP)
    v_ref[...] = seg(AB_V, HP)
    g_ref[...] = seg(AB_G, HP)
    xa = _dot(seg(AB_A, LANES), wa2_ref[...]) + ba_ref[...]
    la_ref[...] = _log_sigmoid(xa) * (1.0 / GLA_TAU)
    u_ref[...] = seg(AB_U, GMLP_W)
    vb = seg(AB_VB, GMLP_W)
    for gi in range(GMLP_G):
        cs = slice(gi * GMLP_DG, (gi + 1) * GMLP_DG)
        vb_ref[:, cs] = _rms(vb[:, cs], vn_ref[...])


def _ab_in(x, mod, l, e, w, tm):
    g_n, t_n, _ = x.shape
    out = jax.ShapeDtypeStruct((g_n, t_n, HP), F32)
    return pl.pallas_call(
        _ab_in_kernel,
        grid=(g_n, t_n // tm),
        in_specs=[_tok_spec(tm, D_MODEL), _mod_spec(mod, l, 3, tm), _mod_spec(mod, l, 4, tm),
                  _layer_spec(w["mix_norm"], l), _layer_spec(w["ab_w_in"], e), _layer_spec(w["gla_w_a2"], e),
                  _layer_spec(w["gla_b_a"], e), _layer_spec(w["gmlp_norm"], e)],
        out_specs=[_tok_spec(tm, HP)] * 7,
        out_shape=[out] * 7,
        compiler_params=_params("parallel", "parallel"),
        name="ab_in",
    )(x, mod, mod, w["mix_norm"], w["ab_w_in"], w["gla_w_a2"], w["gla_b_a"], w["gmlp_norm"])


def _gla_head_chunk(q, k, v, la, st, tril, causal):
    n = q.shape[0]
    b = _dot01(tril, la)
    bm = b[n // 2 - 1:n // 2, :]
    a = jnp.where(causal, _dot_nt(q * jnp.exp(b - bm), k * jnp.exp(bm - b)), 0.0)
    o = _dot(a, v) + _dot_nt(q * jnp.exp(b), st)
    bl = b[n - 1:n, :]
    st_new = st * jnp.exp(bl) + _dot_tn(v, k * jnp.exp(bl - b))
    return o, st_new


def _ab_mix_kernel(*refs, chunk, carry):
    if carry:
        (x_ref, gt_ref, q_ref, k_ref, v_ref, g_ref, la_ref, u_ref, vb_ref, gn_ref, ws_ref, bs_ref, wo_ref,
         xo_ref, so_ref, o_scr, st_scr) = refs
    else:
        (x_ref, gt_ref, q_ref, k_ref, v_ref, g_ref, la_ref, u_ref, vb_ref, gn_ref, ws_ref, bs_ref, wo_ref, s0_ref,
         xo_ref, so_ref, o_scr) = refs
    tm = q_ref.shape[0]
    causal, _ = _causal(chunk)
    tril = jnp.where(causal, 1.0, 0.0).astype(BF16)
    key_pad = jnp.zeros((LANES - GLA_DK, GLA_DV), F32)

    if carry:
        @pl.when(pl.program_id(1) == 0)
        def _():
            st_scr[...] = jnp.zeros_like(st_scr)

    def chunk_body(c, _):
        rows = pl.ds(pl.multiple_of(c * chunk, chunk), chunk)
        for h in range(GLA_H):
            cs = slice(h * LANES, (h + 1) * LANES)
            if carry:
                st = st_scr[h]
            else:
                st = jnp.concatenate([s0_ref[c, h], key_pad], axis=0).T
            o, st_new = _gla_head_chunk(q_ref[rows, cs], k_ref[rows, cs], v_ref[rows, cs], la_ref[rows, cs],
                                        st, tril, causal)
            o_scr[rows, cs] = o
            if carry:
                st_scr[h] = st_new
            else:
                so_ref[c, h] = st_new.T[:GLA_DK, :]
        return 0

    lax.fori_loop(0, tm // chunk, chunk_body, 0)

    if carry:
        @pl.when(pl.program_id(1) == pl.num_programs(1) - 1)
        def _():
            for h in range(GLA_H):
                so_ref[h] = st_scr[h].T[:GLA_DK, :]

    mix = None
    for h in range(GLA_H):
        cs = slice(h * GLA_DV, (h + 1) * GLA_DV)
        on = _rms(o_scr[:, cs], gn_ref[...]) * _silu(g_ref[:, cs])
        p = _dot(on, wo_ref[cs, :])
        mix = p if mix is None else mix + p
    for gi in range(GMLP_G):
        cs = slice(gi * GMLP_DG, (gi + 1) * GMLP_DG)
        zs = []
        for r0 in range(0, tm, GMLP_CHUNK):
            zs.append(_dot(ws_ref[gi], vb_ref[r0:r0 + GMLP_CHUNK, cs]) + bs_ref[gi])
        z = zs[0] if len(zs) == 1 else jnp.concatenate(zs, axis=0)
        mix = mix + _dot(u_ref[:, cs] * z, wo_ref[GLA_H * GLA_DV + gi * GMLP_DG:GLA_H * GLA_DV + (gi + 1) * GMLP_DG, :])
    xo_ref[...] = x_ref[...] + gt_ref[...] * mix


def _ab_mix(x, mod, l, e, w, proj, s0, tm, chunk, ws, bs):
    g_n, t_n, _ = x.shape
    carry = s0 is None
    in_specs = ([_tok_spec(tm, D_MODEL), _mod_spec(mod, l, 5, tm)] + [_tok_spec(tm, HP)] * 7
                + [_layer_spec(w["gla_norm"], e), _layer_spec(ws, e), _layer_spec(bs, e), _layer_spec(w["ab_w_out"], e)])
    args = [x, mod, *proj, w["gla_norm"], ws, bs, w["ab_w_out"]]
    scratch = [pltpu.VMEM((tm, HP), F32)]
    if carry:
        s_shape = (g_n, GLA_H, GLA_DK, GLA_DV)
        s_spec = pl.BlockSpec((None, GLA_H, GLA_DK, GLA_DV), lambda g, i: (g, 0, 0, 0))
        scratch.append(pltpu.VMEM((GLA_H, LANES, LANES), F32))
    else:
        s_shape = s0.shape
        s_spec = pl.BlockSpec((None,) + s0.shape[1:], lambda g, i: (g, 0, 0, 0, 0))
        in_specs.append(s_spec)
        args.append(s0)
    return pl.pallas_call(
        functools.partial(_ab_mix_kernel, chunk=chunk, carry=carry),
        grid=(g_n, t_n // tm),
        in_specs=in_specs,
        out_specs=[_tok_spec(tm, D_MODEL), s_spec],
        out_shape=[jax.ShapeDtypeStruct(x.shape, F32), jax.ShapeDtypeStruct(s_shape, F32)],
        scratch_shapes=scratch,
        compiler_params=_params("parallel", "arbitrary"),
        name="ab_mix",
    )(*args)


def _ml_in_kernel(x_ref, sh_ref, sc_ref, nw_ref, w_ref, xm_ref, z_ref):
    h = (_rms(x_ref[...], nw_ref[...]) * (1.0 + sc_ref[...]) + sh_ref[...]).astype(BF16)
    xm_ref[...] = jnp.dot(h, w_ref[:, :ML_INNER], preferred_element_type=F32)
    z_ref[...] = jnp.dot(h, w_ref[:, ML_INNER:], preferred_element_type=F32)


def _ml_in(x, mod, l, o, w, tm):
    g_n, t_n, _ = x.shape
    out = jax.ShapeDtypeStruct((g_n, t_n, ML_INNER), F32)
    return pl.pallas_call(
        _ml_in_kernel,
        grid=(g_n, t_n // tm),
        in_specs=[_tok_spec(tm, D_MODEL), _mod_spec(mod, l, 3, tm), _mod_spec(mod, l, 4, tm),
                  _layer_spec(w["mix_norm"], l), _layer_spec(w["ml_w_in"], o)],
        out_specs=[_tok_spec(tm, ML_INNER)] * 2,
        out_shape=[out] * 2,
        compiler_params=_params("parallel", "parallel"),
        name="ml_in",
    )(x, mod, mod, w["mix_norm"], w["ml_w_in"])


HALO = SUBLANES


def _ml_pre_kernel(*refs, nb, carry):
    if carry:
        (xm_ref, cw_ref, cb_ref, wq_ref, wk_ref, wv_ref, wg_ref, bg_ref,
         xc_ref, q_ref, k_ref, v_ref, gl_ref, co_ref, ext_scr) = refs
    else:
        (xm_ref, c0_ref, cw_ref, cb_ref, wq_ref, wk_ref, wv_ref, wg_ref, bg_ref,
         xc_ref, q_ref, k_ref, v_ref, gl_ref, co_ref, ext_scr) = refs
    tm = xm_ref.shape[0]
    ls = tm // nb
    first = HALO - (ML_CONV - 1)
    if carry:
        @pl.when(pl.program_id(1) == 0)
        def _():
            ext_scr[:, 0:HALO, :] = jnp.zeros((nb, HALO, ML_INNER), F32)
    else:
        ext_scr[:, first:HALO, :] = c0_ref[...]
    xm = xm_ref[...]
    ext_scr[:, HALO:HALO + ls, :] = xm.reshape(nb, ls, ML_INNER)
    pre = cb_ref[...]
    for j in range(ML_CONV):
        pre = pre + ext_scr[:, first + j:first + j + ls, :].reshape(tm, ML_INNER) * cw_ref[j:j + 1, :]
    xc = _silu(pre)
    xc_ref[...] = xc
    co_ref[...] = ext_scr[:, first + ls:HALO + ls, :].reshape(co_ref.shape)
    if carry:
        ext_scr[:, 0:HALO, :] = ext_scr[:, ls:ls + HALO, :]

    gates = bg_ref[...]
    for blk in range(ML_INNER // MXU_DIM):
        cs = slice(blk * MXU_DIM, (blk + 1) * MXU_DIM)
        xcb = xc[:, cs].astype(BF16)
        q = jnp.dot(xcb, wq_ref[blk], preferred_element_type=F32)
        k = jnp.dot(xcb, wk_ref[blk], preferred_element_type=F32)
        v = jnp.dot(xm[:, cs].astype(BF16), wv_ref[blk], preferred_element_type=F32)
        gates = (gates + _dot(q, wg_ref[cs, :]) + _dot(k, wg_ref[ML_INNER + blk * MXU_DIM:ML_INNER + (blk + 1) * MXU_DIM, :])
                 + _dot(v, wg_ref[2 * ML_INNER + blk * MXU_DIM:2 * ML_INNER + (blk + 1) * MXU_DIM, :]))
        q_ref[:, cs] = q.astype(q_ref.dtype)
        k_ref[:, cs] = (k * ML_DH ** -0.5).astype(k_ref.dtype)
        v_ref[:, cs] = v.astype(v_ref.dtype)
    lane = lax.broadcasted_iota(jnp.int32, gates.shape, 1)
    gl_ref[...] = jnp.where(lane < ML_H, gates, _log_sigmoid(gates))


def _ml_pre(xm, c0, o, w, tm, nb, qkv_dtype):
    g_n, t_n, _ = xm.shape
    carry = c0 is None
    wnames = ["ml_conv_w", "ml_conv_b", "ml_wq", "ml_wk", "ml_wv", "ml_w_gates", "ml_b_gates"]
    in_specs = [_tok_spec(tm, ML_INNER)]
    args = [xm]
    if carry:
        c_shape = (g_n, ML_CONV - 1, ML_INNER)
        c_spec = pl.BlockSpec((None, ML_CONV - 1, ML_INNER), lambda g, i: (g, 0, 0))
    else:
        c_shape = c0.shape
        c_spec = pl.BlockSpec((None,) + c0.shape[1:], lambda g, i: (g, 0, 0, 0))
        in_specs.append(c_spec)
        args.append(c0)
    in_specs += [_layer_spec(w[n], o) for n in wnames]
    args += [w[n] for n in wnames]
    big = lambda dt: jax.ShapeDtypeStruct((g_n, t_n, ML_INNER), dt)
    return pl.pallas_call(
        functools.partial(_ml_pre_kernel, nb=nb, carry=carry),
        grid=(g_n, t_n // tm),
        in_specs=in_specs,
        out_specs=[_tok_spec(tm, ML_INNER)] * 4 + [_tok_spec(tm, LANES), c_spec],
        out_shape=[big(F32), big(qkv_dtype), big(qkv_dtype), big(qkv_dtype),
                   jax.ShapeDtypeStruct((g_n, t_n, LANES), F32), jax.ShapeDtypeStruct(c_shape, F32)],
        scratch_shapes=[pltpu.VMEM((nb, HALO + tm // nb, ML_INNER), F32)],
        compiler_params=_params("parallel", "arbitrary"),
        name="ml_pre",
    )(*args)


def _ml_scan_kernel(*refs, carry):
    if carry:
        (q_ref, k_ref, v_ref, gl_ref, hh_ref, co_ref, no_ref, mo_ref, c_scr, n_scr, m_scr) = refs
    else:
        (q_ref, k_ref, v_ref, gl_ref, c0_ref, n0_ref, m0_ref, hh_ref, co_ref, no_ref, mo_ref) = refs
    n_rows = q_ref.shape[0]
    causal, eye = _causal(n_rows)
    tril = jnp.where(causal, 1.0, 0.0).astype(BF16)

    if carry:
        @pl.when(pl.program_id(1) == 0)
        def _():
            c_scr[...] = jnp.zeros_like(c_scr)
            n_scr[...] = jnp.zeros_like(n_scr)
            m_scr[...] = jnp.zeros_like(m_scr)

    gl = gl_ref[...]
    cum = _dot01(tril, gl)
    for h in range(ML_H):
        cs = slice(h * ML_DH, (h + 1) * ML_DH)
        q = q_ref[:, cs]
        k = k_ref[:, cs]
        v = v_ref[:, cs].astype(F32)
        if carry:
            c_prev, n_prev, m_prev = c_scr[h], n_scr[h:h + 1, :], m_scr[h:h + 1, 0:1]
        else:
            c_prev, n_prev, m_prev = c0_ref[h], n0_ref[h:h + 1, :], m0_ref[0:1, h:h + 1]
        ig_col = gl[:, h:h + 1]
        f_col = cum[:, ML_H + h:ML_H + h + 1]
        ig_row = _col_to_row(ig_col, eye)
        f_row = _col_to_row(f_col, eye)
        log_d = jnp.where(causal, f_col - f_row + ig_row, -jnp.inf)
        log_inter = f_col + m_prev
        m_t = jnp.maximum(log_inter, jnp.max(log_d, axis=-1, keepdims=True))
        d = jnp.exp(log_d - m_t)
        w_inter = jnp.exp(log_inter - m_t)
        s = _dot_nt(q, k) * d
        num = _dot(s, v) + w_inter * _dot_nt(q, c_prev)
        den = jnp.sum(s, axis=-1, keepdims=True) + w_inter * jnp.sum(q.astype(F32) * n_prev, axis=-1, keepdims=True)
        hh_ref[:, cs] = num / jnp.maximum(jnp.abs(den), jnp.exp(-m_t))
        m_new = m_t[n_rows - 1:n_rows, :]
        f_last = f_col[n_rows - 1:n_rows, :]
        w_rows = jnp.exp(f_last - f_col + ig_col - m_new)
        decay = jnp.exp(f_last + m_prev - m_new)
        kf = k.astype(F32)
        c_new = decay * c_prev + _dot_tn(v * w_rows, kf)
        n_new = decay * n_prev + jnp.sum(kf * w_rows, axis=0, keepdims=True)
        if carry:
            c_scr[h] = c_new
            n_scr[h:h + 1, :] = n_new
            m_scr[h:h + 1, :] = jnp.broadcast_to(m_new, (1, LANES))
        else:
            co_ref[h] = c_new
            no_ref[h:h + 1, :] = n_new
            mo_ref[0:1, h:h + 1] = m_new

    if carry:
        @pl.when(pl.program_id(1) == pl.num_programs(1) - 1)
        def _():
            co_ref[...] = c_scr[...]
            no_ref[...] = n_scr[0:ML_H, :]
            for h in range(ML_H):
                mo_ref[0:1, h:h + 1] = m_scr[h:h + 1, 0:1]


def _ml_scan(q, k, v, gl, state, tm):
    g_n, t_n, _ = q.shape
    carry = state is None
    c_spec = pl.BlockSpec((None, ML_H, ML_DH, ML_DH), lambda g, i: (g, 0, 0, 0))
    n_spec = pl.BlockSpec((None, ML_H, ML_DH), lambda g, i: (g, 0, 0))
    m_spec = pl.BlockSpec((None, 1, ML_H), lambda g, i: (g, 0, 0))
    in_specs = [_tok_spec(tm, ML_INNER)] * 3 + [_tok_spec(tm, LANES)]
    args = [q, k, v, gl]
    scratch = []
    if carry:
        scratch = [pltpu.VMEM((ML_H, ML_DH, ML_DH), F32), pltpu.VMEM((SUBLANES, ML_DH), F32),
                   pltpu.VMEM((SUBLANES, LANES), F32)]
    else:
        in_specs += [c_spec, n_spec, m_spec]
        args += list(state)
    return pl.pallas_call(
        functools.partial(_ml_scan_kernel, carry=carry),
        grid=(g_n, t_n // tm),
        in_specs=in_specs,
        out_specs=[_tok_spec(tm, ML_INNER), c_spec, n_spec, m_spec],
        out_shape=[jax.ShapeDtypeStruct((g_n, t_n, ML_INNER), F32),
                   jax.ShapeDtypeStruct((g_n, ML_H, ML_DH, ML_DH), F32),
                   jax.ShapeDtypeStruct((g_n, ML_H, ML_DH), F32),
                   jax.ShapeDtypeStruct((g_n, 1, ML_H), F32)],
        scratch_shapes=scratch,
        compiler_params=_params("parallel", "arbitrary"),
        name="ml_scan",
    )(*args)


def _ml_post_kernel(x_ref, gt_ref, hh_ref, xc_ref, z_ref, gn_ref, sk_ref, wo_ref, xo_ref):
    mix = None
    for h in range(ML_H):
        cs = slice(h * ML_DH, (h + 1) * ML_DH)
        hh = hh_ref[:, cs]
        mu = jnp.mean(hh, axis=-1, keepdims=True)
        xc = hh - mu
        var = jnp.mean(xc * xc, axis=-1, keepdims=True)
        hn = xc * lax.rsqrt(var + EPS) * gn_ref[:, cs]
        out = (hn + sk_ref[:, cs] * xc_ref[:, cs]) * _silu(z_ref[:, cs])
        p = _dot(out, wo_ref[cs, :])
        mix = p if mix is None else mix + p
    xo_ref[...] = x_ref[...] + gt_ref[...] * mix


def _ml_post(x, mod, l, o, w, hh, xc, z, tm):
    g_n, t_n, _ = x.shape
    return pl.pallas_call(
        _ml_post_kernel,
        grid=(g_n, t_n // tm),
        in_specs=[_tok_spec(tm, D_MODEL), _mod_spec(mod, l, 5, tm)] + [_tok_spec(tm, ML_INNER)] * 3
        + [_layer_spec(w["ml_norm"], o), _layer_spec(w["ml_skip"], o), _layer_spec(w["ml_w_out"], o)],
        out_specs=_tok_spec(tm, D_MODEL),
        out_shape=jax.ShapeDtypeStruct(x.shape, F32),
        compiler_params=_params("parallel", "parallel"),
        name="ml_post",
    )(x, mod, hh, xc, z, w["ml_norm"], w["ml_skip"], w["ml_w_out"])


def _final_kernel(x_ref, sh_ref, sc_ref, nw_ref, o_ref):
    o_ref[...] = _rms(x_ref[...], nw_ref[...]) * (1.0 + sc_ref[...]) + sh_ref[...]


def _final(x, mod, w, tm):
    g_n, t_n, _ = x.shape
    return pl.pallas_call(
        _final_kernel,
        grid=(g_n, t_n // tm),
        in_specs=[_tok_spec(tm, D_MODEL), _mod_spec(mod, 0, 0, tm), _mod_spec(mod, 0, 1, tm),
                  pl.BlockSpec((1, D_MODEL), lambda g, i: (0, 0))],
        out_specs=_tok_spec(tm, D_MODEL),
        out_shape=jax.ShapeDtypeStruct(x.shape, F32),
        compiler_params=_params("parallel", "parallel"),
        name="final_norm",
    )(x, mod, mod, w["final_norm"])


def _pad_heads(a):
    lead = a.shape[:-1]
    a = a.reshape(lead + (GLA_H, GLA_DK))
    a = jnp.pad(a, [(0, 0)] * len(lead) + [(0, 0), (0, LANES - GLA_DK)])
    return a.reshape(lead + (HP,))


def _block_diag_dense(wb):
    n_e = wb.shape[0]
    per = MXU_DIM // ML_BLOCK
    wb = wb.reshape(n_e, -1, per, ML_BLOCK, ML_BLOCK)
    eye = jnp.eye(per, dtype=wb.dtype)
    dense = jnp.einsum("ebnio,nm->ebnimo", wb, eye)
    return dense.reshape(n_e, -1, MXU_DIM, MXU_DIM).astype(BF16)


def _prepare_weights(p):
    w = {}
    w["ffn_norm"] = p["ffn_norm"].reshape(DEPTH, 2, 1, D_MODEL)
    w["ffn_w_gate"] = p["ffn_w_gate"].astype(BF16)
    w["ffn_w_up"] = p["ffn_w_up"].astype(BF16)
    w["ffn_w_down"] = p["ffn_w_down"].astype(BF16)
    w["mix_norm"] = p["mix_norm"].reshape(DEPTH, 1, D_MODEL)
    w_in = p["ab_w_in"]
    qk = GLA_H * GLA_DK
    vw = GLA_H * GLA_DV
    o_k, o_v, o_g, o_a = qk, 2 * qk, 2 * qk + vw, 2 * qk + 2 * vw
    o_u = o_a + GLA_LOWRANK
    o_vb = o_u + GMLP_W
    w_a = jnp.pad(w_in[:, :, o_a:o_u], ((0, 0), (0, 0), (0, LANES - GLA_LOWRANK)))
    w["ab_w_in"] = jnp.concatenate(
        [_pad_heads(w_in[:, :, :o_k]), _pad_heads(w_in[:, :, o_k:o_v]), w_in[:, :, o_v:o_g], w_in[:, :, o_g:o_a],
         w_a, w_in[:, :, o_u:o_vb], w_in[:, :, o_vb:]], axis=-1).astype(BF16)
    w["gla_w_a2"] = jnp.pad(_pad_heads(p["gla_w_a2"]), ((0, 0), (0, LANES - GLA_LOWRANK), (0, 0))).astype(BF16)
    w["gla_b_a"] = _pad_heads(p["gla_b_a"])[:, None, :]
    w["gla_norm"] = p["gla_norm"][:, None, :]
    w["gmlp_norm"] = p["gmlp_norm"][:, None, :]
    w["ab_w_out"] = p["ab_w_out"].astype(BF16)
    w["ml_w_in"] = p["ml_w_in"].astype(BF16)
    w["ml_conv_w"] = p["ml_conv_w"]
    w["ml_conv_b"] = p["ml_conv_b"][:, None, :]
    w["ml_wq"] = _block_diag_dense(p["ml_wq"])
    w["ml_wk"] = _block_diag_dense(p["ml_wk"])
    w["ml_wv"] = _block_diag_dense(p["ml_wv"])
    w["ml_w_gates"] = jnp.pad(p["ml_w_gates"], ((0, 0), (0, 0), (0, LANES - 2 * ML_H))).astype(BF16)
    w["ml_b_gates"] = jnp.pad(p["ml_b_gates"], ((0, 0), (0, LANES - 2 * ML_H)))[:, None, :]
    w["ml_norm"] = p["ml_norm"][:, None, :]
    w["ml_skip"] = p["ml_skip"][:, None, :]
    w["ml_w_out"] = p["ml_w_out"].astype(BF16)
    w["final_norm"] = p["final_norm"][None, :]
    return w


def _spatial_weights(ws, bs, seq_len):
    if seq_len % GMLP_CHUNK == 0:
        length = GMLP_CHUNK
    else:
        length = seq_len
    reps = GMLP_CHUNK // length
    wt = jnp.tril(ws[:, :, :length, :length])
    eye = jnp.eye(reps, dtype=ws.dtype)
    wt = jnp.einsum("egts,ab->egatbs", wt, eye).reshape(ws.shape[0], GMLP_G, GMLP_CHUNK, GMLP_CHUNK)
    bt = jnp.tile(bs[:, :, :length], (1, 1, reps))
    bt = jnp.broadcast_to(bt[:, :, :, None], (ws.shape[0], GMLP_G, GMLP_CHUNK, GMLP_DG))
    return wt.astype(BF16), bt


def _trunk(x, mod, fmod, w, ws, bs, states, *, tm, mix_tm, gla_chunk, seq_rows):
    fresh = states is None
    g_n, t_n, _ = x.shape
    tm = min(tm, t_n)
    s_out, v_out, c_out, n_out, m_out, conv_out = [], [], [], [], [], []
    for l in range(DEPTH):
        x = _ffn(x, mod, l, 0, w, tm)
        if l % 2 == 0:
            e = l // 2
            proj = _ab_in(x, mod, l, e, w, tm)
            v_out.append(proj[6])
            if fresh:
                x, s_new = _ab_mix(x, mod, l, e, w, proj, None, mix_tm, gla_chunk, ws, bs)
            else:
                nseq = mix_tm // seq_rows
                view = lambda a: a.reshape(-1, mix_tm, a.shape[-1])
                s0 = states["gla_S"][e].reshape(-1, nseq, GLA_H, GLA_DK, GLA_DV)
                mod_v = mod.reshape(mod.shape[0], mod.shape[1], -1, mix_tm, D_MODEL)
                xv, s_new = _ab_mix(view(x), mod_v, l, e, w, [view(a) for a in proj], s0, mix_tm, gla_chunk, ws, bs)
                x = xv.reshape(g_n, t_n, D_MODEL)
                s_new = s_new.reshape(-1, GLA_H, GLA_DK, GLA_DV)
            s_out.append(s_new)
        else:
            o = l // 2
            xm, z = _ml_in(x, mod, l, o, w, tm)
            if fresh:
                pre_tm = min(tm, 2 * ML_CHUNK)
                xc, q, k, v, gl, conv_new = _ml_pre(xm, None, o, w, pre_tm, 1, BF16)
                hh, c_new, n_new, m_new = _ml_scan(q, k, v, gl, None, ML_CHUNK)
                m_new = m_new.reshape(g_n, ML_H)
            else:
                nseq = LANES // seq_rows
                rows = nseq * seq_rows
                view = lambda a, r: a.reshape(-1, r, a.shape[-1])
                c0 = states["ml_conv"][o].reshape(-1, nseq, ML_CONV - 1, ML_INNER)
                xc, q, k, v, gl, conv_new = _ml_pre(view(xm, rows), c0, o, w, rows, nseq, F32)
                conv_new = conv_new.reshape(-1, ML_CONV - 1, ML_INNER)
                st = (states["ml_C"][o], states["ml_n"][o], states["ml_m"][o][:, None, :])
                hh, c_new, n_new, m_new = _ml_scan(view(q, seq_rows), view(k, seq_rows), view(v, seq_rows),
                                                   view(gl, seq_rows), st, seq_rows)
                m_new = m_new.reshape(-1, ML_H)
                xc = xc.reshape(g_n, t_n, ML_INNER)
                hh = hh.reshape(g_n, t_n, ML_INNER)
            post_tm = min(tm, 2 * ML_CHUNK)
            x = _ml_post(x, mod, l, o, w, hh, xc, z, post_tm)
            c_out.append(c_new)
            n_out.append(n_new)
            m_out.append(m_new)
            conv_out.append(conv_new)
        x = _ffn(x, mod, l, 1, w, tm)
    y = _final(x, fmod, w, tm)
    return (y, jnp.stack(s_out), jnp.stack(v_out), jnp.stack(c_out), jnp.stack(n_out), jnp.stack(m_out),
            jnp.stack(conv_out))


def kernel(x_prompt, x_sample, c_prompt, c_sample, state_gla_S, state_mlstm_C, state_mlstm_n, state_mlstm_m, state_mlstm_conv, ada_w, ada_b, ffn_norm, ffn_w_gate, ffn_w_up, ffn_w_down, mix_norm, ab_w_in, gla_w_a2, gla_b_a, gla_norm, gmlp_norm, gmlp_ws, gmlp_bs, ab_w_out, ml_w_in, ml_conv_w, ml_conv_b, ml_wq, ml_wk, ml_wv, ml_w_gates, ml_b_gates, ml_norm, ml_skip, ml_w_out, final_norm, final_ada_w, final_ada_b):
    p = dict(ffn_norm=ffn_norm, ffn_w_gate=ffn_w_gate, ffn_w_up=ffn_w_up, ffn_w_down=ffn_w_down, mix_norm=mix_norm,
             ab_w_in=ab_w_in, gla_w_a2=gla_w_a2, gla_b_a=gla_b_a, gla_norm=gla_norm, gmlp_norm=gmlp_norm,
             ab_w_out=ab_w_out, ml_w_in=ml_w_in, ml_conv_w=ml_conv_w, ml_conv_b=ml_conv_b, ml_wq=ml_wq, ml_wk=ml_wk,
             ml_wv=ml_wv, ml_w_gates=ml_w_gates, ml_b_gates=ml_b_gates, ml_norm=ml_norm, ml_skip=ml_skip,
             ml_w_out=ml_w_out, final_norm=final_norm)
    w = _prepare_weights(p)
    n_p, t_p, _ = x_prompt.shape
    n_s, t_s, _ = x_sample.shape

    c_all = jnp.concatenate([c_prompt, c_sample], axis=0)
    mod = _ada(c_all, ada_w, ada_b)
    fmod = _ada(c_all, final_ada_w[None], final_ada_b[None])

    def split_mod(m):
        m_p = m[:, :, :n_p, None, :]
        m_s = jnp.repeat(m[:, :, n_p:], t_s, axis=2)[:, :, None]
        return m_p, m_s

    mod_p, mod_s = split_mod(mod)
    fmod_p, fmod_s = split_mod(fmod)

    ws_p, bs_p = _spatial_weights(gmlp_ws, gmlp_bs, t_p)
    ws_s, bs_s = _spatial_weights(gmlp_ws, gmlp_bs, t_s)

    gla_chunk_p = GLA_CHUNK if t_p % GLA_CHUNK == 0 else t_p
    y_p, s_p, _, c_p, n_p_, m_p, cv_p = _trunk(
        x_prompt, mod_p, fmod_p, w, ws_p, bs_p, None, tm=512, mix_tm=512, gla_chunk=gla_chunk_p, seq_rows=t_p)

    states = dict(gla_S=state_gla_S, ml_C=state_mlstm_C, ml_n=state_mlstm_n, ml_m=state_mlstm_m,
                  ml_conv=state_mlstm_conv)
    xs = x_sample.reshape(1, n_s * t_s, D_MODEL)
    y_s, s_s, v_s, c_s, n_s_, m_s, cv_s = _trunk(
        xs, mod_s, fmod_s, w, ws_s, bs_s, states, tm=512, mix_tm=GMLP_CHUNK, gla_chunk=t_s, seq_rows=t_s)
    y_s = y_s.reshape(n_s, t_s, D_MODEL)
    v_s = v_s.reshape(-1, n_s, t_s, GMLP_W)
    return (y_p, y_s, s_p, s_s, v_s, c_p, c_s, n_p_, n_s_, m_p, m_s, cv_p, cv_s)
```

```python
import functools

import jax
import jax.numpy as jnp
from jax import lax
from jax.experimental import pallas as pl
from jax.experimental.pallas import tpu as pltpu

F32 = jnp.float32
BF16 = jnp.bfloat16
EPS = 1e-6

D_MODEL = 1024
DEPTH = 4
D_FF = 2816
GLA_H = 4
GLA_DK = 64
GLA_DV = 128
GLA_LOWRANK = 16
GLA_TAU = 16.0
GLA_CHUNK = 64
GMLP_G = 4
GMLP_DG = 128
GMLP_W = GMLP_G * GMLP_DG
GMLP_CHUNK = 128
ML_INNER = 2 * D_MODEL
ML_H = 4
ML_DH = ML_INNER // ML_H
ML_CONV = 4
ML_BLOCK = 4
ML_CHUNK = 128

LANES = 128
SUBLANES = 8
MXU_DIM = 256
VMEM_LIMIT_BYTES = 56 * 1024 * 1024

ROW_TILE = 512
GLA_TILE_CHUNK = 128
ML_TILE_CHUNK = 256
FFN_ROW_TILE = 1024


def _chunk_len(t, preferred, nominal):
    for c in (preferred, nominal):
        if t % c == 0:
            return c
    return t


HP = GLA_H * LANES
AB_Q, AB_K, AB_V, AB_G = 0, HP, 2 * HP, 3 * HP
AB_A = 4 * HP
AB_U = AB_A + LANES
AB_VB = AB_U + GMLP_W
AB_COLS = AB_VB + GMLP_W


def _dot(a, b):
    return jnp.dot(a.astype(BF16), b.astype(BF16), preferred_element_type=F32)


def _dot_nt(a, b):
    return lax.dot_general(a.astype(BF16), b.astype(BF16), (((1,), (1,)), ((), ())), preferred_element_type=F32)


def _dot_tn(a, b):
    return lax.dot_general(a.astype(BF16), b.astype(BF16), (((0,), (0,)), ((), ())), preferred_element_type=F32)


def _dot01(sel, x):
    hi = x.astype(BF16)
    r1 = x - hi.astype(F32)
    mid = r1.astype(BF16)
    lo = (r1 - mid.astype(F32)).astype(BF16)
    return (jnp.dot(sel, hi, preferred_element_type=F32) + jnp.dot(sel, mid, preferred_element_type=F32)
            + jnp.dot(sel, lo, preferred_element_type=F32))


def _rms(x, g):
    return x * lax.rsqrt(jnp.mean(x * x, axis=-1, keepdims=True) + EPS) * g


def _silu(x):
    return x * jax.nn.sigmoid(x)


def _log_sigmoid(x):
    return jnp.minimum(x, 0.0) - jnp.log1p(jnp.exp(-jnp.abs(x)))


def _causal(n):
    row = lax.broadcasted_iota(jnp.int32, (n, n), 0)
    col = lax.broadcasted_iota(jnp.int32, (n, n), 1)
    return row >= col, row == col


def _mod_rows(ref, tm):
    m = ref[...]
    r = m.shape[0]
    if r == 1 or r == tm:
        return m
    rep = tm // r
    row = lax.broadcasted_iota(jnp.int32, (tm, r), 0)
    lo = lax.broadcasted_iota(jnp.int32, (tm, r), 1) * rep
    sel = jnp.where(row >= lo, jnp.where(row < lo + rep, 1.0, 0.0), 0.0).astype(BF16)
    return _dot01(sel, m)


def _col_to_row(col, eye):
    return jnp.sum(jnp.where(eye, col, 0.0), axis=0, keepdims=True)


def _tok_spec(tm, w):
    return pl.BlockSpec((None, tm, w), lambda g, i: (g, i, 0))


def _mod_spec(mod, l, j, tm, t_n):
    t_mod = mod.shape[3]
    if t_mod == 1:
        return pl.BlockSpec((None, None, None, 1, D_MODEL), lambda g, i: (l, j, g, 0, 0))
    return pl.BlockSpec((None, None, None, t_mod * tm // t_n, D_MODEL), lambda g, i: (l, j, g, i, 0))


def _layer_spec(a, *lead):
    n = len(lead)
    shape = (None,) * n + tuple(a.shape[n:])
    zeros = (0,) * (a.ndim - n)
    return pl.BlockSpec(shape, lambda g, i: tuple(lead) + zeros, pipeline_mode=pl.Buffered(1))


def _params(*sem):
    return pltpu.CompilerParams(dimension_semantics=sem, vmem_limit_bytes=VMEM_LIMIT_BYTES)


def _ada_kernel(c_ref, w_ref, b_ref, o_ref):
    cs = _silu(c_ref[...])
    o_ref[...] = _dot(cs, w_ref[...]) + b_ref[...]


def _ada(c, w, b):
    n_l, _, width = w.shape
    n_j = width // D_MODEL
    r = c.shape[0]
    return pl.pallas_call(
        _ada_kernel,
        grid=(n_l, n_j),
        in_specs=[pl.BlockSpec((r, D_MODEL), lambda l, j: (0, 0)),
                  pl.BlockSpec((None, D_MODEL, D_MODEL), lambda l, j: (l, 0, j)),
                  pl.BlockSpec((None, None, 1, D_MODEL), lambda l, j: (l, j, 0, 0))],
        out_specs=pl.BlockSpec((None, None, r, D_MODEL), lambda l, j: (l, j, 0, 0)),
        out_shape=jax.ShapeDtypeStruct((n_l, n_j, r, D_MODEL), F32),
        compiler_params=_params("arbitrary", "arbitrary"),
        name="ada",
    )(c, w, b.reshape(n_l, n_j, 1, D_MODEL))


FFN_CHUNK = 2 * MXU_DIM


def _ffn_kernel(x_ref, sh_ref, sc_ref, gt_ref, nw_ref, wg_ref, wu_ref, wd_ref, o_ref):
    x = x_ref[...]
    tm = x.shape[0]
    h = (_rms(x, nw_ref[...]) * (1.0 + _mod_rows(sc_ref, tm)) + _mod_rows(sh_ref, tm)).astype(BF16)
    acc = None
    for f0 in range(0, D_FF, FFN_CHUNK):
        f1 = min(f0 + FFN_CHUNK, D_FF)
        g = jnp.dot(h, wg_ref[:, f0:f1], preferred_element_type=F32)
        u = jnp.dot(h, wu_ref[:, f0:f1], preferred_element_type=F32)
        p = _dot(_silu(g) * u, wd_ref[f0:f1, :])
        acc = p if acc is None else acc + p
    o_ref[...] = x + (0.5 * _mod_rows(gt_ref, tm)) * acc


def _ffn(x, mod, l, s, w, tm):
    g_n, t_n, _ = x.shape
    j0 = 6 * s
    return pl.pallas_call(
        _ffn_kernel,
        grid=(g_n, t_n // tm),
        in_specs=[_tok_spec(tm, D_MODEL), _mod_spec(mod, l, j0, tm, t_n), _mod_spec(mod, l, j0 + 1, tm, t_n),
                  _mod_spec(mod, l, j0 + 2, tm, t_n), _layer_spec(w["ffn_norm"], l, s),
                  _layer_spec(w["ffn_w_gate"], l, s), _layer_spec(w["ffn_w_up"], l, s),
                  _layer_spec(w["ffn_w_down"], l, s)],
        out_specs=_tok_spec(tm, D_MODEL),
        out_shape=jax.ShapeDtypeStruct(x.shape, F32),
        compiler_params=_params("parallel", "parallel"),
        name="ffn",
    )(x, mod, mod, mod, w["ffn_norm"], w["ffn_w_gate"], w["ffn_w_up"], w["ffn_w_down"])


def _ab_in_kernel(x_ref, sh_ref, sc_ref, nw_ref, w_ref, wa2_ref, ba_ref, vn_ref,
                  q_ref, k_ref, v_ref, g_ref, la_ref, u_ref, vb_ref):
    tm = x_ref.shape[0]
    h = (_rms(x_ref[...], nw_ref[...]) * (1.0 + _mod_rows(sc_ref, tm)) + _mod_rows(sh_ref, tm)).astype(BF16)

    def seg(c0, width):
        return jnp.dot(h, w_ref[:, c0:c0 + width], preferred_element_type=F32)

    q_ref[...] = seg(AB_Q, HP) * GLA_DK ** -0.5
    k_ref[...] = seg(AB_K, HP)
    v_ref[...] = seg(AB_V, HP)
    g_ref[...] = seg(AB_G, HP)
    xa = _dot(seg(AB_A, LANES), wa2_ref[...]) + ba_ref[...]
    la_ref[...] = _log_sigmoid(xa) * (1.0 / GLA_TAU)
    u_ref[...] = seg(AB_U, GMLP_W)
    vb = seg(AB_VB, GMLP_W)
    for gi in range(GMLP_G):
        cs = slice(gi * GMLP_DG, (gi + 1) * GMLP_DG)
        vb_ref[:, cs] = _rms(vb[:, cs], vn_ref[...])


def _ab_in(x, mod, l, e, w, tm):
    g_n, t_n, _ = x.shape
    out = jax.ShapeDtypeStruct((g_n, t_n, HP), F32)
    return pl.pallas_call(
        _ab_in_kernel,
        grid=(g_n, t_n // tm),
        in_specs=[_tok_spec(tm, D_MODEL), _mod_spec(mod, l, 3, tm, t_n), _mod_spec(mod, l, 4, tm, t_n),
                  _layer_spec(w["mix_norm"], l), _layer_spec(w["ab_w_in"], e), _layer_spec(w["gla_w_a2"], e),
                  _layer_spec(w["gla_b_a"], e), _layer_spec(w["gmlp_norm"], e)],
        out_specs=[_tok_spec(tm, HP)] * 7,
        out_shape=[out] * 7,
        compiler_params=_params("parallel", "parallel"),
        name="ab_in",
    )(x, mod, mod, w["mix_norm"], w["ab_w_in"], w["gla_w_a2"], w["gla_b_a"], w["gmlp_norm"])


def _gla_head_chunk(q, k, v, la, st, tril, causal):
    n = q.shape[0]
    b = _dot01(tril, la)
    bm = b[n // 2 - 1:n // 2, :]
    a = jnp.where(causal, _dot_nt(q * jnp.exp(b - bm), k * jnp.exp(bm - b)), 0.0)
    o = _dot(a, v) + _dot_nt(q * jnp.exp(b), st)
    bl = b[n - 1:n, :]
    st_new = st * jnp.exp(bl) + _dot_tn(v, k * jnp.exp(bl - b))
    return o, st_new


def _ab_mix_kernel(*refs, chunk, carry):
    if carry:
        (x_ref, gt_ref, q_ref, k_ref, v_ref, g_ref, la_ref, u_ref, vb_ref, gn_ref, ws_ref, bs_ref, wo_ref,
         xo_ref, so_ref, o_scr, st_scr) = refs
    else:
        (x_ref, gt_ref, q_ref, k_ref, v_ref, g_ref, la_ref, u_ref, vb_ref, gn_ref, ws_ref, bs_ref, wo_ref, s0_ref,
         xo_ref, so_ref, o_scr) = refs
    tm = q_ref.shape[0]
    causal, _ = _causal(chunk)
    tril = jnp.where(causal, 1.0, 0.0).astype(BF16)
    key_pad = jnp.zeros((LANES - GLA_DK, GLA_DV), F32)

    if carry:
        @pl.when(pl.program_id(1) == 0)
        def _():
            st_scr[...] = jnp.zeros_like(st_scr)

    def chunk_body(c, _):
        rows = pl.ds(pl.multiple_of(c * chunk, chunk), chunk)
        for h in range(GLA_H):
            cs = slice(h * LANES, (h + 1) * LANES)
            if carry:
                st = st_scr[h]
            else:
                st = jnp.concatenate([s0_ref[c, h], key_pad], axis=0).T
            o, st_new = _gla_head_chunk(q_ref[rows, cs], k_ref[rows, cs], v_ref[rows, cs], la_ref[rows, cs],
                                        st, tril, causal)
            o_scr[rows, cs] = o
            if carry:
                st_scr[h] = st_new
            else:
                so_ref[c, h] = st_new.T[:GLA_DK, :]
        return 0

    lax.fori_loop(0, tm // chunk, chunk_body, 0)

    if carry:
        @pl.when(pl.program_id(1) == pl.num_programs(1) - 1)
        def _():
            for h in range(GLA_H):
                so_ref[h] = st_scr[h].T[:GLA_DK, :]

    mix = None
    for h in range(GLA_H):
        cs = slice(h * GLA_DV, (h + 1) * GLA_DV)
        on = _rms(o_scr[:, cs], gn_ref[...]) * _silu(g_ref[:, cs])
        p = _dot(on, wo_ref[cs, :])
        mix = p if mix is None else mix + p
    for gi in range(GMLP_G):
        cs = slice(gi * GMLP_DG, (gi + 1) * GMLP_DG)
        zs = []
        for r0 in range(0, tm, GMLP_CHUNK):
            zs.append(_dot(ws_ref[gi], vb_ref[r0:r0 + GMLP_CHUNK, cs]) + bs_ref[gi])
        z = zs[0] if len(zs) == 1 else jnp.concatenate(zs, axis=0)
        mix = mix + _dot(u_ref[:, cs] * z, wo_ref[GLA_H * GLA_DV + gi * GMLP_DG:GLA_H * GLA_DV + (gi + 1) * GMLP_DG, :])
    xo_ref[...] = x_ref[...] + _mod_rows(gt_ref, tm) * mix


def _ab_mix(x, mod, l, e, w, proj, s0, tm, chunk, ws, bs):
    g_n, t_n, _ = x.shape
    carry = s0 is None
    in_specs = ([_tok_spec(tm, D_MODEL), _mod_spec(mod, l, 5, tm, t_n)] + [_tok_spec(tm, HP)] * 7
                + [_layer_spec(w["gla_norm"], e), _layer_spec(ws, e), _layer_spec(bs, e), _layer_spec(w["ab_w_out"], e)])
    args = [x, mod, *proj, w["gla_norm"], ws, bs, w["ab_w_out"]]
    scratch = [pltpu.VMEM((tm, HP), F32)]
    if carry:
        s_shape = (g_n, GLA_H, GLA_DK, GLA_DV)
        s_spec = pl.BlockSpec((None, GLA_H, GLA_DK, GLA_DV), lambda g, i: (g, 0, 0, 0))
        scratch.append(pltpu.VMEM((GLA_H, LANES, LANES), F32))
    else:
        s_shape = s0.shape
        s_spec = pl.BlockSpec((None,) + s0.shape[1:], lambda g, i: (g, 0, 0, 0, 0))
        in_specs.append(s_spec)
        args.append(s0)
    return pl.pallas_call(
        functools.partial(_ab_mix_kernel, chunk=chunk, carry=carry),
        grid=(g_n, t_n // tm),
        in_specs=in_specs,
        out_specs=[_tok_spec(tm, D_MODEL), s_spec],
        out_shape=[jax.ShapeDtypeStruct(x.shape, F32), jax.ShapeDtypeStruct(s_shape, F32)],
        scratch_shapes=scratch,
        compiler_params=_params("parallel", "arbitrary"),
        name="ab_mix",
    )(*args)


def _ml_in_kernel(x_ref, sh_ref, sc_ref, nw_ref, w_ref, xm_ref, z_ref):
    tm = x_ref.shape[0]
    h = (_rms(x_ref[...], nw_ref[...]) * (1.0 + _mod_rows(sc_ref, tm)) + _mod_rows(sh_ref, tm)).astype(BF16)
    xm_ref[...] = jnp.dot(h, w_ref[:, :ML_INNER], preferred_element_type=F32)
    z_ref[...] = jnp.dot(h, w_ref[:, ML_INNER:], preferred_element_type=F32)


def _ml_in(x, mod, l, o, w, tm):
    g_n, t_n, _ = x.shape
    out = jax.ShapeDtypeStruct((g_n, t_n, ML_INNER), F32)
    return pl.pallas_call(
        _ml_in_kernel,
        grid=(g_n, t_n // tm),
        in_specs=[_tok_spec(tm, D_MODEL), _mod_spec(mod, l, 3, tm, t_n), _mod_spec(mod, l, 4, tm, t_n),
                  _layer_spec(w["mix_norm"], l), _layer_spec(w["ml_w_in"], o)],
        out_specs=[_tok_spec(tm, ML_INNER)] * 2,
        out_shape=[out] * 2,
        compiler_params=_params("parallel", "parallel"),
        name="ml_in",
    )(x, mod, mod, w["mix_norm"], w["ml_w_in"])


HALO = SUBLANES


def _ml_pre_kernel(*refs, nb, carry):
    if carry:
        (xm_ref, cw_ref, cb_ref, wq_ref, wk_ref, wv_ref, wg_ref, bg_ref,
         xc_ref, q_ref, k_ref, v_ref, gl_ref, co_ref, ext_scr) = refs
    else:
        (xm_ref, c0_ref, cw_ref, cb_ref, wq_ref, wk_ref, wv_ref, wg_ref, bg_ref,
         xc_ref, q_ref, k_ref, v_ref, gl_ref, co_ref, ext_scr) = refs
    tm = xm_ref.shape[0]
    ls = tm // nb
    first = HALO - (ML_CONV - 1)
    if carry:
        @pl.when(pl.program_id(1) == 0)
        def _():
            ext_scr[:, 0:HALO, :] = jnp.zeros((nb, HALO, ML_INNER), F32)
    else:
        ext_scr[:, first:HALO, :] = c0_ref[...]
    xm = xm_ref[...]
    ext_scr[:, HALO:HALO + ls, :] = xm.reshape(nb, ls, ML_INNER)
    pre = cb_ref[...]
    for j in range(ML_CONV):
        pre = pre + ext_scr[:, first + j:first + j + ls, :].reshape(tm, ML_INNER) * cw_ref[j:j + 1, :]
    xc = _silu(pre)
    xc_ref[...] = xc
    co_ref[...] = ext_scr[:, first + ls:HALO + ls, :].reshape(co_ref.shape)
    if carry:
        ext_scr[:, 0:HALO, :] = ext_scr[:, ls:ls + HALO, :]

    gates = bg_ref[...]
    for blk in range(ML_INNER // MXU_DIM):
        cs = slice(blk * MXU_DIM, (blk + 1) * MXU_DIM)
        xcb = xc[:, cs].astype(BF16)
        q = jnp.dot(xcb, wq_ref[blk], preferred_element_type=F32)
        k = jnp.dot(xcb, wk_ref[blk], preferred_element_type=F32)
        v = jnp.dot(xm[:, cs].astype(BF16), wv_ref[blk], preferred_element_type=F32)
        gates = (gates + _dot(q, wg_ref[cs, :]) + _dot(k, wg_ref[ML_INNER + blk * MXU_DIM:ML_INNER + (blk + 1) * MXU_DIM, :])
                 + _dot(v, wg_ref[2 * ML_INNER + blk * MXU_DIM:2 * ML_INNER + (blk + 1) * MXU_DIM, :]))
        q_ref[:, cs] = q.astype(q_ref.dtype)
        k_ref[:, cs] = (k * ML_DH ** -0.5).astype(k_ref.dtype)
        v_ref[:, cs] = v.astype(v_ref.dtype)
    lane = lax.broadcasted_iota(jnp.int32, gates.shape, 1)
    gl_ref[...] = jnp.where(lane < ML_H, gates, _log_sigmoid(gates))


def _ml_pre(xm, c0, o, w, tm, nb, qkv_dtype):
    g_n, t_n, _ = xm.shape
    carry = c0 is None
    wnames = ["ml_conv_w", "ml_conv_b", "ml_wq", "ml_wk", "ml_wv", "ml_w_gates", "ml_b_gates"]
    in_specs = [_tok_spec(tm, ML_INNER)]
    args = [xm]
    if carry:
        c_shape = (g_n, ML_CONV - 1, ML_INNER)
        c_spec = pl.BlockSpec((None, ML_CONV - 1, ML_INNER), lambda g, i: (g, 0, 0))
    else:
        c_shape = c0.shape
        c_spec = pl.BlockSpec((None,) + c0.shape[1:], lambda g, i: (g, 0, 0, 0))
        in_specs.append(c_spec)
        args.append(c0)
    in_specs += [_layer_spec(w[n], o) for n in wnames]
    args += [w[n] for n in wnames]
    big = lambda dt: jax.ShapeDtypeStruct((g_n, t_n, ML_INNER), dt)
    return pl.pallas_call(
        functools.partial(_ml_pre_kernel, nb=nb, carry=carry),
        grid=(g_n, t_n // tm),
        in_specs=in_specs,
        out_specs=[_tok_spec(tm, ML_INNER)] * 4 + [_tok_spec(tm, LANES), c_spec],
        out_shape=[big(F32), big(qkv_dtype), big(qkv_dtype), big(qkv_dtype),
                   jax.ShapeDtypeStruct((g_n, t_n, LANES), F32), jax.ShapeDtypeStruct(c_shape, F32)],
        scratch_shapes=[pltpu.VMEM((nb, HALO + tm // nb, ML_INNER), F32)],
        compiler_params=_params("parallel", "arbitrary"),
        name="ml_pre",
    )(*args)


def _ml_scan_kernel(*refs, carry, aliased):
    if aliased:
        refs = refs[1:]
    if carry:
        (q_ref, k_ref, v_ref, gl_ref, hh_ref, co_ref, no_ref, mo_ref, c_scr, n_scr, m_scr) = refs
    else:
        (q_ref, k_ref, v_ref, gl_ref, c0_ref, n0_ref, m0_ref, hh_ref, co_ref, no_ref, mo_ref) = refs
    n_rows = q_ref.shape[0]
    causal, eye = _causal(n_rows)
    tril = jnp.where(causal, 1.0, 0.0).astype(BF16)

    if carry:
        @pl.when(pl.program_id(1) == 0)
        def _():
            c_scr[...] = jnp.zeros_like(c_scr)
            n_scr[...] = jnp.zeros_like(n_scr)
            m_scr[...] = jnp.zeros_like(m_scr)

    gl = gl_ref[...]
    cum = _dot01(tril, gl)
    for h in range(ML_H):
        cs = slice(h * ML_DH, (h + 1) * ML_DH)
        q = q_ref[:, cs]
        k = k_ref[:, cs]
        v = v_ref[:, cs].astype(F32)
        if carry:
            c_prev, n_prev, m_prev = c_scr[h], n_scr[h:h + 1, :], m_scr[h:h + 1, 0:1]
        else:
            c_prev, n_prev, m_prev = c0_ref[h], n0_ref[h:h + 1, :], m0_ref[0:1, h:h + 1]
        ig_col = gl[:, h:h + 1]
        f_col = cum[:, ML_H + h:ML_H + h + 1]
        ig_row = _col_to_row(ig_col, eye)
        f_row = _col_to_row(f_col, eye)
        log_d = jnp.where(causal, f_col - f_row + ig_row, -jnp.inf)
        log_inter = f_col + m_prev
        m_t = jnp.maximum(log_inter, jnp.max(log_d, axis=-1, keepdims=True))
        d = jnp.exp(log_d - m_t)
        w_inter = jnp.exp(log_inter - m_t)
        s = _dot_nt(q, k) * d
        num = _dot(s, v) + w_inter * _dot_nt(q, c_prev)
        den = jnp.sum(s, axis=-1, keepdims=True) + w_inter * jnp.sum(q.astype(F32) * n_prev, axis=-1, keepdims=True)
        hh_ref[:, cs] = num / jnp.maximum(jnp.abs(den), jnp.exp(-m_t))
        m_new = m_t[n_rows - 1:n_rows, :]
        f_last = f_col[n_rows - 1:n_rows, :]
        w_rows = jnp.exp(f_last - f_col + ig_col - m_new)
        decay = jnp.exp(f_last + m_prev - m_new)
        kf = k.astype(F32)
        c_new = decay * c_prev + _dot_tn(v * w_rows, kf)
        n_new = decay * n_prev + jnp.sum(kf * w_rows, axis=0, keepdims=True)
        if carry:
            c_scr[h] = c_new
            n_scr[h:h + 1, :] = n_new
            m_scr[h:h + 1, :] = jnp.broadcast_to(m_new, (1, LANES))
        else:
            co_ref[h] = c_new
            no_ref[h:h + 1, :] = n_new
            mo_ref[0:1, h:h + 1] = m_new

    if carry:
        @pl.when(pl.program_id(1) == pl.num_programs(1) - 1)
        def _():
            co_ref[...] = c_scr[...]
            no_ref[...] = n_scr[0:ML_H, :]
            for h in range(ML_H):
                mo_ref[0:1, h:h + 1] = m_scr[h:h + 1, 0:1]


def _ml_scan(q, k, v, gl, state, tm, o, c_stack):
    g_n, t_n, _ = q.shape
    carry = state is None
    c_spec = pl.BlockSpec((None, None, ML_H, ML_DH, ML_DH), lambda g, i: (o, g, 0, 0, 0))
    n_spec = pl.BlockSpec((None, None, ML_H, ML_DH), lambda g, i: (o, g, 0, 0))
    m_spec = pl.BlockSpec((None, None, 1, ML_H), lambda g, i: (o, g, 0, 0))
    n_out_spec = pl.BlockSpec((None, ML_H, ML_DH), lambda g, i: (g, 0, 0))
    m_out_spec = pl.BlockSpec((None, 1, ML_H), lambda g, i: (g, 0, 0))
    in_specs = [_tok_spec(tm, ML_INNER)] * 3 + [_tok_spec(tm, LANES)]
    args = [q, k, v, gl]
    scratch = []
    if carry:
        scratch = [pltpu.VMEM((ML_H, ML_DH, ML_DH), F32), pltpu.VMEM((SUBLANES, ML_DH), F32),
                   pltpu.VMEM((SUBLANES, LANES), F32)]
    else:
        in_specs += [c_spec, n_spec, m_spec]
        args += list(state)
    aliases = {}
    if c_stack is not None:
        aliases = {0: 1}
        in_specs.insert(0, pl.BlockSpec(memory_space=pl.ANY))
        args.insert(0, c_stack)
    return pl.pallas_call(
        functools.partial(_ml_scan_kernel, carry=carry, aliased=c_stack is not None),
        grid=(g_n, t_n // tm),
        in_specs=in_specs,
        out_specs=[_tok_spec(tm, ML_INNER), c_spec, n_out_spec, m_out_spec],
        out_shape=[jax.ShapeDtypeStruct((g_n, t_n, ML_INNER), F32),
                   jax.ShapeDtypeStruct((DEPTH // 2, g_n, ML_H, ML_DH, ML_DH), F32),
                   jax.ShapeDtypeStruct((g_n, ML_H, ML_DH), F32),
                   jax.ShapeDtypeStruct((g_n, 1, ML_H), F32)],
        scratch_shapes=scratch,
        input_output_aliases=aliases,
        compiler_params=_params("parallel", "arbitrary"),
        name="ml_scan",
    )(*args)


def _ml_post_kernel(x_ref, gt_ref, hh_ref, xc_ref, z_ref, gn_ref, sk_ref, wo_ref, xo_ref):
    tm = x_ref.shape[0]
    mix = None
    for h in range(ML_H):
        cs = slice(h * ML_DH, (h + 1) * ML_DH)
        hh = hh_ref[:, cs]
        mu = jnp.mean(hh, axis=-1, keepdims=True)
        xc = hh - mu
        var = jnp.mean(xc * xc, axis=-1, keepdims=True)
        hn = xc * lax.rsqrt(var + EPS) * gn_ref[:, cs]
        out = (hn + sk_ref[:, cs] * xc_ref[:, cs]) * _silu(z_ref[:, cs])
        p = _dot(out, wo_ref[cs, :])
        mix = p if mix is None else mix + p
    xo_ref[...] = x_ref[...] + _mod_rows(gt_ref, tm) * mix


def _ml_post(x, mod, l, o, w, hh, xc, z, tm):
    g_n, t_n, _ = x.shape
    return pl.pallas_call(
        _ml_post_kernel,
        grid=(g_n, t_n // tm),
        in_specs=[_tok_spec(tm, D_MODEL), _mod_spec(mod, l, 5, tm, t_n)] + [_tok_spec(tm, ML_INNER)] * 3
        + [_layer_spec(w["ml_norm"], o), _layer_spec(w["ml_skip"], o), _layer_spec(w["ml_w_out"], o)],
        out_specs=_tok_spec(tm, D_MODEL),
        out_shape=jax.ShapeDtypeStruct(x.shape, F32),
        compiler_params=_params("parallel", "parallel"),
        name="ml_post",
    )(x, mod, hh, xc, z, w["ml_norm"], w["ml_skip"], w["ml_w_out"])


def _final_kernel(x_ref, sh_ref, sc_ref, nw_ref, o_ref):
    tm = x_ref.shape[0]
    o_ref[...] = _rms(x_ref[...], nw_ref[...]) * (1.0 + _mod_rows(sc_ref, tm)) + _mod_rows(sh_ref, tm)


def _final(x, mod, w, tm):
    g_n, t_n, _ = x.shape
    return pl.pallas_call(
        _final_kernel,
        grid=(g_n, t_n // tm),
        in_specs=[_tok_spec(tm, D_MODEL), _mod_spec(mod, 0, 0, tm, t_n), _mod_spec(mod, 0, 1, tm, t_n),
                  pl.BlockSpec((1, D_MODEL), lambda g, i: (0, 0))],
        out_specs=_tok_spec(tm, D_MODEL),
        out_shape=jax.ShapeDtypeStruct(x.shape, F32),
        compiler_params=_params("parallel", "parallel"),
        name="final_norm",
    )(x, mod, mod, w["final_norm"])


def _pad_heads(a):
    lead = a.shape[:-1]
    a = a.reshape(lead + (GLA_H, GLA_DK))
    a = jnp.pad(a, [(0, 0)] * len(lead) + [(0, 0), (0, LANES - GLA_DK)])
    return a.reshape(lead + (HP,))


def _block_diag_dense(wb):
    n_e = wb.shape[0]
    per = MXU_DIM // ML_BLOCK
    wb = wb.reshape(n_e, -1, per, ML_BLOCK, ML_BLOCK)
    eye = jnp.eye(per, dtype=wb.dtype)
    dense = jnp.einsum("ebnio,nm->ebnimo", wb, eye)
    return dense.reshape(n_e, -1, MXU_DIM, MXU_DIM).astype(BF16)


def _prepare_weights(p):
    w = {}
    w["ffn_norm"] = p["ffn_norm"].reshape(DEPTH, 2, 1, D_MODEL)
    w["ffn_w_gate"] = p["ffn_w_gate"].astype(BF16)
    w["ffn_w_up"] = p["ffn_w_up"].astype(BF16)
    w["ffn_w_down"] = p["ffn_w_down"].astype(BF16)
    w["mix_norm"] = p["mix_norm"].reshape(DEPTH, 1, D_MODEL)
    w_in = p["ab_w_in"]
    qk = GLA_H * GLA_DK
    vw = GLA_H * GLA_DV
    o_k, o_v, o_g, o_a = qk, 2 * qk, 2 * qk + vw, 2 * qk + 2 * vw
    o_u = o_a + GLA_LOWRANK
    o_vb = o_u + GMLP_W
    w_a = jnp.pad(w_in[:, :, o_a:o_u], ((0, 0), (0, 0), (0, LANES - GLA_LOWRANK)))
    w["ab_w_in"] = jnp.concatenate(
        [_pad_heads(w_in[:, :, :o_k]), _pad_heads(w_in[:, :, o_k:o_v]), w_in[:, :, o_v:o_g], w_in[:, :, o_g:o_a],
         w_a, w_in[:, :, o_u:o_vb], w_in[:, :, o_vb:]], axis=-1).astype(BF16)
    w["gla_w_a2"] = jnp.pad(_pad_heads(p["gla_w_a2"]), ((0, 0), (0, LANES - GLA_LOWRANK), (0, 0))).astype(BF16)
    w["gla_b_a"] = _pad_heads(p["gla_b_a"])[:, None, :]
    w["gla_norm"] = p["gla_norm"][:, None, :]
    w["gmlp_norm"] = p["gmlp_norm"][:, None, :]
    w["ab_w_out"] = p["ab_w_out"].astype(BF16)
    w["ml_w_in"] = p["ml_w_in"].astype(BF16)
    w["ml_conv_w"] = p["ml_conv_w"]
    w["ml_conv_b"] = p["ml_conv_b"][:, None, :]
    w["ml_wq"] = _block_diag_dense(p["ml_wq"])
    w["ml_wk"] = _block_diag_dense(p["ml_wk"])
    w["ml_wv"] = _block_diag_dense(p["ml_wv"])
    w["ml_w_gates"] = jnp.pad(p["ml_w_gates"], ((0, 0), (0, 0), (0, LANES - 2 * ML_H))).astype(BF16)
    w["ml_b_gates"] = jnp.pad(p["ml_b_gates"], ((0, 0), (0, LANES - 2 * ML_H)))[:, None, :]
    w["ml_norm"] = p["ml_norm"][:, None, :]
    w["ml_skip"] = p["ml_skip"][:, None, :]
    w["ml_w_out"] = p["ml_w_out"].astype(BF16)
    w["final_norm"] = p["final_norm"][None, :]
    return w


def _spatial_weights(ws, bs, seq_len):
    if seq_len % GMLP_CHUNK == 0:
        length = GMLP_CHUNK
    else:
        length = seq_len
    reps = GMLP_CHUNK // length
    wt = jnp.tril(ws[:, :, :length, :length])
    eye = jnp.eye(reps, dtype=ws.dtype)
    wt = jnp.einsum("egts,ab->egatbs", wt, eye).reshape(ws.shape[0], GMLP_G, GMLP_CHUNK, GMLP_CHUNK)
    bt = jnp.tile(bs[:, :, :length], (1, 1, reps))
    bt = jnp.broadcast_to(bt[:, :, :, None], (ws.shape[0], GMLP_G, GMLP_CHUNK, GMLP_DG))
    return wt.astype(BF16), bt


def _trunk(x, mod, fmod, w, ws, bs, states, *, tm, mix_tm, gla_chunk, scan_tm, seq_rows):
    fresh = states is None
    g_n, t_n, _ = x.shape
    tm = min(tm, t_n)
    s_out, v_out, n_out, m_out, conv_out = [], [], [], [], []
    c_stack = None
    for l in range(DEPTH):
        x = _ffn(x, mod, l, 0, w, min(FFN_ROW_TILE, t_n))
        if l % 2 == 0:
            e = l // 2
            proj = _ab_in(x, mod, l, e, w, tm)
            v_out.append(proj[6])
            if fresh:
                x, s_new = _ab_mix(x, mod, l, e, w, proj, None, mix_tm, gla_chunk, ws, bs)
            else:
                nseq = mix_tm // seq_rows
                view = lambda a: a.reshape(-1, mix_tm, a.shape[-1])
                s0 = states["gla_S"][e].reshape(-1, nseq, GLA_H, GLA_DK, GLA_DV)
                mod_v = mod.reshape(mod.shape[0], mod.shape[1], -1, nseq, D_MODEL)
                xv, s_new = _ab_mix(view(x), mod_v, l, e, w, [view(a) for a in proj], s0, mix_tm, gla_chunk, ws, bs)
                x = xv.reshape(g_n, t_n, D_MODEL)
                s_new = s_new.reshape(-1, GLA_H, GLA_DK, GLA_DV)
            s_out.append(s_new)
        else:
            o = l // 2
            xm, z = _ml_in(x, mod, l, o, w, tm)
            if fresh:
                pre_tm = min(tm, 2 * ML_CHUNK)
                xc, q, k, v, gl, conv_new = _ml_pre(xm, None, o, w, pre_tm, 1, BF16)
                hh, c_stack, n_new, m_new = _ml_scan(q, k, v, gl, None, scan_tm, o, c_stack)
                m_new = m_new.reshape(g_n, ML_H)
            else:
                nseq = LANES // seq_rows
                rows = nseq * seq_rows
                view = lambda a, r: a.reshape(-1, r, a.shape[-1])
                c0 = states["ml_conv"][o].reshape(-1, nseq, ML_CONV - 1, ML_INNER)
                xc, q, k, v, gl, conv_new = _ml_pre(view(xm, rows), c0, o, w, rows, nseq, F32)
                conv_new = conv_new.reshape(-1, ML_CONV - 1, ML_INNER)
                st = (states["ml_C"], states["ml_n"], states["ml_m"][:, :, None, :])
                hh, c_stack, n_new, m_new = _ml_scan(view(q, seq_rows), view(k, seq_rows), view(v, seq_rows),
                                                     view(gl, seq_rows), st, seq_rows, o, c_stack)
                m_new = m_new.reshape(-1, ML_H)
                xc = xc.reshape(g_n, t_n, ML_INNER)
                hh = hh.reshape(g_n, t_n, ML_INNER)
            post_tm = min(tm, 2 * ML_CHUNK)
            x = _ml_post(x, mod, l, o, w, hh, xc, z, post_tm)
            n_out.append(n_new)
            m_out.append(m_new)
            conv_out.append(conv_new)
        x = _ffn(x, mod, l, 1, w, min(FFN_ROW_TILE, t_n))
    y = _final(x, fmod, w, tm)
    return (y, jnp.stack(s_out), jnp.stack(v_out), c_stack, jnp.stack(n_out), jnp.stack(m_out),
            jnp.stack(conv_out))


def kernel(x_prompt, x_sample, c_prompt, c_sample, state_gla_S, state_mlstm_C, state_mlstm_n, state_mlstm_m, state_mlstm_conv, ada_w, ada_b, ffn_norm, ffn_w_gate, ffn_w_up, ffn_w_down, mix_norm, ab_w_in, gla_w_a2, gla_b_a, gla_norm, gmlp_norm, gmlp_ws, gmlp_bs, ab_w_out, ml_w_in, ml_conv_w, ml_conv_b, ml_wq, ml_wk, ml_wv, ml_w_gates, ml_b_gates, ml_norm, ml_skip, ml_w_out, final_norm, final_ada_w, final_ada_b):
    p = dict(ffn_norm=ffn_norm, ffn_w_gate=ffn_w_gate, ffn_w_up=ffn_w_up, ffn_w_down=ffn_w_down, mix_norm=mix_norm,
             ab_w_in=ab_w_in, gla_w_a2=gla_w_a2, gla_b_a=gla_b_a, gla_norm=gla_norm, gmlp_norm=gmlp_norm,
             ab_w_out=ab_w_out, ml_w_in=ml_w_in, ml_conv_w=ml_conv_w, ml_conv_b=ml_conv_b, ml_wq=ml_wq, ml_wk=ml_wk,
             ml_wv=ml_wv, ml_w_gates=ml_w_gates, ml_b_gates=ml_b_gates, ml_norm=ml_norm, ml_skip=ml_skip,
             ml_w_out=ml_w_out, final_norm=final_norm)
    w = _prepare_weights(p)
    n_p, t_p, _ = x_prompt.shape
    n_s, t_s, _ = x_sample.shape

    c_all = jnp.concatenate([c_prompt, c_sample], axis=0)
    mod = _ada(c_all, ada_w, ada_b)
    fmod = _ada(c_all, final_ada_w[None], final_ada_b[None])

    def split_mod(m):
        return m[:, :, :n_p, None, :], m[:, :, None, n_p:, :]

    mod_p, mod_s = split_mod(mod)
    fmod_p, fmod_s = split_mod(fmod)

    ws_p, bs_p = _spatial_weights(gmlp_ws, gmlp_bs, t_p)
    ws_s, bs_s = _spatial_weights(gmlp_ws, gmlp_bs, t_s)

    y_p, s_p, _, c_p, n_p_, m_p, cv_p = _trunk(
        x_prompt, mod_p, fmod_p, w, ws_p, bs_p, None, tm=ROW_TILE, mix_tm=ROW_TILE,
        gla_chunk=_chunk_len(t_p, GLA_TILE_CHUNK, GLA_CHUNK), scan_tm=_chunk_len(t_p, ML_TILE_CHUNK, ML_CHUNK),
        seq_rows=t_p)

    states = dict(gla_S=state_gla_S, ml_C=state_mlstm_C, ml_n=state_mlstm_n, ml_m=state_mlstm_m,
                  ml_conv=state_mlstm_conv)
    xs = x_sample.reshape(1, n_s * t_s, D_MODEL)
    y_s, s_s, v_s, c_s, n_s_, m_s, cv_s = _trunk(
        xs, mod_s, fmod_s, w, ws_s, bs_s, states, tm=ROW_TILE, mix_tm=GMLP_CHUNK, gla_chunk=t_s, scan_tm=t_s,
        seq_rows=t_s)
    y_s = y_s.reshape(n_s, t_s, D_MODEL)
    v_s = v_s.reshape(-1, n_s, t_s, GMLP_W)
    return (y_p, y_s, s_p, s_s, v_s, c_p, c_s, n_p_, n_s_, m_p, m_s, cv_p, cv_s)
```

```python
import functools

import jax
import jax.numpy as jnp
from jax import lax
from jax.experimental import pallas as pl
from jax.experimental.pallas import tpu as pltpu

F32 = jnp.float32
BF16 = jnp.bfloat16
EPS = 1e-6

D_MODEL = 1024
DEPTH = 4
D_FF = 2816
GLA_H = 4
GLA_DK = 64
GLA_DV = 128
GLA_LOWRANK = 16
GLA_TAU = 16.0
GLA_CHUNK = 64
GMLP_G = 4
GMLP_DG = 128
GMLP_W = GMLP_G * GMLP_DG
GMLP_CHUNK = 128
ML_INNER = 2 * D_MODEL
ML_H = 4
ML_DH = ML_INNER // ML_H
ML_CONV = 4
ML_BLOCK = 4
ML_CHUNK = 128

LANES = 128
SUBLANES = 8
MXU_DIM = 256
VMEM_LIMIT_BYTES = 56 * 1024 * 1024

ROW_TILE = 512
GLA_TILE_CHUNK = 128
ML_TILE_CHUNK = 256
FFN_ROW_TILE = 1024


def _chunk_len(t, preferred, nominal):
    for c in (preferred, nominal):
        if t % c == 0:
            return c
    return t


HP = GLA_H * LANES
AB_Q, AB_K, AB_V, AB_G = 0, HP, 2 * HP, 3 * HP
AB_A = 4 * HP
AB_U = AB_A + LANES
AB_VB = AB_U + GMLP_W
AB_COLS = AB_VB + GMLP_W


def _dot(a, b):
    return jnp.dot(a.astype(BF16), b.astype(BF16), preferred_element_type=F32)


def _dot_nt(a, b):
    return lax.dot_general(a.astype(BF16), b.astype(BF16), (((1,), (1,)), ((), ())), preferred_element_type=F32)


def _dot_tn(a, b):
    return lax.dot_general(a.astype(BF16), b.astype(BF16), (((0,), (0,)), ((), ())), preferred_element_type=F32)


def _dot01(sel, x):
    hi = x.astype(BF16)
    r1 = x - hi.astype(F32)
    mid = r1.astype(BF16)
    lo = (r1 - mid.astype(F32)).astype(BF16)
    return (jnp.dot(sel, hi, preferred_element_type=F32) + jnp.dot(sel, mid, preferred_element_type=F32)
            + jnp.dot(sel, lo, preferred_element_type=F32))


def _rms(x, g):
    return x * lax.rsqrt(jnp.mean(x * x, axis=-1, keepdims=True) + EPS) * g


def _silu(x):
    return x * jax.nn.sigmoid(x)


def _log_sigmoid(x):
    return jnp.minimum(x, 0.0) - jnp.log1p(jnp.exp(-jnp.abs(x)))


def _causal(n):
    row = lax.broadcasted_iota(jnp.int32, (n, n), 0)
    col = lax.broadcasted_iota(jnp.int32, (n, n), 1)
    return row >= col, row == col


def _mod_rows(ref, tm):
    m = ref[...]
    r = m.shape[0]
    if r == 1 or r == tm:
        return m
    rep = tm // r
    row = lax.broadcasted_iota(jnp.int32, (tm, r), 0)
    lo = lax.broadcasted_iota(jnp.int32, (tm, r), 1) * rep
    sel = jnp.where(row >= lo, jnp.where(row < lo + rep, 1.0, 0.0), 0.0).astype(BF16)
    return _dot01(sel, m)


def _col_to_row(col, eye):
    return jnp.sum(jnp.where(eye, col, 0.0), axis=0, keepdims=True)


def _tok_spec(tm, w):
    return pl.BlockSpec((None, tm, w), lambda g, i: (g, i, 0))


def _mod_spec(mod, l, j, tm, t_n):
    t_mod = mod.shape[3]
    if t_mod == 1:
        return pl.BlockSpec((None, None, None, 1, D_MODEL), lambda g, i: (l, j, g, 0, 0))
    return pl.BlockSpec((None, None, None, t_mod * tm // t_n, D_MODEL), lambda g, i: (l, j, g, i, 0))


def _layer_spec(a, *lead):
    n = len(lead)
    shape = (None,) * n + tuple(a.shape[n:])
    zeros = (0,) * (a.ndim - n)
    return pl.BlockSpec(shape, lambda g, i: tuple(lead) + zeros, pipeline_mode=pl.Buffered(1))


def _params(*sem):
    return pltpu.CompilerParams(dimension_semantics=sem, vmem_limit_bytes=VMEM_LIMIT_BYTES)


def _ada_kernel(c_ref, w_ref, b_ref, o_ref):
    cs = _silu(c_ref[...])
    o_ref[...] = _dot(cs, w_ref[...]) + b_ref[...]


def _ada(c, w, b):
    n_l, _, width = w.shape
    n_j = width // D_MODEL
    r = c.shape[0]
    return pl.pallas_call(
        _ada_kernel,
        grid=(n_l, n_j),
        in_specs=[pl.BlockSpec((r, D_MODEL), lambda l, j: (0, 0)),
                  pl.BlockSpec((None, D_MODEL, D_MODEL), lambda l, j: (l, 0, j)),
                  pl.BlockSpec((None, None, 1, D_MODEL), lambda l, j: (l, j, 0, 0))],
        out_specs=pl.BlockSpec((None, None, r, D_MODEL), lambda l, j: (l, j, 0, 0)),
        out_shape=jax.ShapeDtypeStruct((n_l, n_j, r, D_MODEL), F32),
        compiler_params=_params("arbitrary", "arbitrary"),
        name="ada",
    )(c, w, b.reshape(n_l, n_j, 1, D_MODEL))


FFN_CHUNK = 2 * MXU_DIM


def _ffn_kernel(x_ref, sh_ref, sc_ref, gt_ref, nw_ref, wg_ref, wu_ref, wd_ref, *rest):
    o_ref = rest[-1]
    x = x_ref[...]
    tm = x.shape[0]
    h = (_rms(x, nw_ref[...]) * (1.0 + _mod_rows(sc_ref, tm)) + _mod_rows(sh_ref, tm)).astype(BF16)
    acc = None
    for f0 in range(0, D_FF, FFN_CHUNK):
        f1 = min(f0 + FFN_CHUNK, D_FF)
        g = jnp.dot(h, wg_ref[:, f0:f1], preferred_element_type=F32)
        u = jnp.dot(h, wu_ref[:, f0:f1], preferred_element_type=F32)
        p = _dot(_silu(g) * u, wd_ref[f0:f1, :])
        acc = p if acc is None else acc + p
    y = x + (0.5 * _mod_rows(gt_ref, tm)) * acc
    if len(rest) > 1:
        fsh_ref, fsc_ref, fnw_ref, _ = rest
        y = _rms(y, fnw_ref[...]) * (1.0 + _mod_rows(fsc_ref, tm)) + _mod_rows(fsh_ref, tm)
    o_ref[...] = y


def _ffn(x, mod, l, s, w, tm, fmod=None):
    g_n, t_n, _ = x.shape
    j0 = 6 * s
    in_specs = [_tok_spec(tm, D_MODEL), _mod_spec(mod, l, j0, tm, t_n), _mod_spec(mod, l, j0 + 1, tm, t_n),
                _mod_spec(mod, l, j0 + 2, tm, t_n), _layer_spec(w["ffn_norm"], l, s),
                _layer_spec(w["ffn_w_gate"], l, s), _layer_spec(w["ffn_w_up"], l, s),
                _layer_spec(w["ffn_w_down"], l, s)]
    args = [x, mod, mod, mod, w["ffn_norm"], w["ffn_w_gate"], w["ffn_w_up"], w["ffn_w_down"]]
    if fmod is not None:
        in_specs += [_mod_spec(fmod, 0, 0, tm, t_n), _mod_spec(fmod, 0, 1, tm, t_n),
                     pl.BlockSpec((1, D_MODEL), lambda g, i: (0, 0))]
        args += [fmod, fmod, w["final_norm"]]
    return pl.pallas_call(
        _ffn_kernel,
        grid=(g_n, t_n // tm),
        in_specs=in_specs,
        out_specs=_tok_spec(tm, D_MODEL),
        out_shape=jax.ShapeDtypeStruct(x.shape, F32),
        compiler_params=_params("parallel", "parallel"),
        name="ffn",
    )(*args)


def _ab_in_kernel(x_ref, sh_ref, sc_ref, nw_ref, w_ref, wa2_ref, ba_ref, vn_ref,
                  q_ref, k_ref, v_ref, g_ref, la_ref, u_ref, vb_ref):
    tm = x_ref.shape[0]
    h = (_rms(x_ref[...], nw_ref[...]) * (1.0 + _mod_rows(sc_ref, tm)) + _mod_rows(sh_ref, tm)).astype(BF16)

    def seg(c0, width):
        return jnp.dot(h, w_ref[:, c0:c0 + width], preferred_element_type=F32)

    q_ref[...] = seg(AB_Q, HP) * GLA_DK ** -0.5
    k_ref[...] = seg(AB_K, HP)
    v_ref[...] = seg(AB_V, HP)
    g_ref[...] = seg(AB_G, HP)
    xa = _dot(seg(AB_A, LANES), wa2_ref[...]) + ba_ref[...]
    la_ref[...] = _log_sigmoid(xa) * (1.0 / GLA_TAU)
    u_ref[...] = seg(AB_U, GMLP_W)
    vb = seg(AB_VB, GMLP_W)
    for gi in range(GMLP_G):
        cs = slice(gi * GMLP_DG, (gi + 1) * GMLP_DG)
        vb_ref[:, cs] = _rms(vb[:, cs], vn_ref[...])


def _ab_in(x, mod, l, e, w, tm):
    g_n, t_n, _ = x.shape
    out = jax.ShapeDtypeStruct((g_n, t_n, HP), F32)
    return pl.pallas_call(
        _ab_in_kernel,
        grid=(g_n, t_n // tm),
        in_specs=[_tok_spec(tm, D_MODEL), _mod_spec(mod, l, 3, tm, t_n), _mod_spec(mod, l, 4, tm, t_n),
                  _layer_spec(w["mix_norm"], l), _layer_spec(w["ab_w_in"], e), _layer_spec(w["gla_w_a2"], e),
                  _layer_spec(w["gla_b_a"], e), _layer_spec(w["gmlp_norm"], e)],
        out_specs=[_tok_spec(tm, HP)] * 7,
        out_shape=[out] * 7,
        compiler_params=_params("parallel", "parallel"),
        name="ab_in",
    )(x, mod, mod, w["mix_norm"], w["ab_w_in"], w["gla_w_a2"], w["gla_b_a"], w["gmlp_norm"])


def _gla_head_chunk(q, k, v, la, st, tril, causal):
    n = q.shape[0]
    b = _dot01(tril, la)
    bm = b[n // 2 - 1:n // 2, :]
    a = jnp.where(causal, _dot_nt(q * jnp.exp(b - bm), k * jnp.exp(bm - b)), 0.0)
    o = _dot(a, v) + _dot_nt(q * jnp.exp(b), st)
    bl = b[n - 1:n, :]
    st_new = st * jnp.exp(bl) + _dot_tn(v, k * jnp.exp(bl - b))
    return o, st_new


def _ab_mix_kernel(*refs, chunk, carry, aliased):
    if aliased:
        refs = refs[1:]
    if carry:
        (x_ref, gt_ref, q_ref, k_ref, v_ref, g_ref, la_ref, u_ref, vb_ref, gn_ref, ws_ref, bs_ref, wo_ref,
         xo_ref, so_ref, o_scr, st_scr) = refs
    else:
        (x_ref, gt_ref, q_ref, k_ref, v_ref, g_ref, la_ref, u_ref, vb_ref, gn_ref, ws_ref, bs_ref, wo_ref, s0_ref,
         xo_ref, so_ref, o_scr) = refs
    tm = q_ref.shape[0]
    causal, _ = _causal(chunk)
    tril = jnp.where(causal, 1.0, 0.0).astype(BF16)
    key_pad = jnp.zeros((LANES - GLA_DK, GLA_DV), F32)

    if carry:
        @pl.when(pl.program_id(1) == 0)
        def _():
            st_scr[...] = jnp.zeros_like(st_scr)

    def chunk_body(c, _):
        rows = pl.ds(pl.multiple_of(c * chunk, chunk), chunk)
        for h in range(GLA_H):
            cs = slice(h * LANES, (h + 1) * LANES)
            if carry:
                st = st_scr[h]
            else:
                st = jnp.concatenate([s0_ref[c, h], key_pad], axis=0).T
            o, st_new = _gla_head_chunk(q_ref[rows, cs], k_ref[rows, cs], v_ref[rows, cs], la_ref[rows, cs],
                                        st, tril, causal)
            o_scr[rows, cs] = o
            if carry:
                st_scr[h] = st_new
            else:
                so_ref[c, h] = st_new.T[:GLA_DK, :]
        return 0

    lax.fori_loop(0, tm // chunk, chunk_body, 0)

    if carry:
        @pl.when(pl.program_id(1) == pl.num_programs(1) - 1)
        def _():
            for h in range(GLA_H):
                so_ref[h] = st_scr[h].T[:GLA_DK, :]

    mix = None
    for h in range(GLA_H):
        cs = slice(h * GLA_DV, (h + 1) * GLA_DV)
        on = _rms(o_scr[:, cs], gn_ref[...]) * _silu(g_ref[:, cs])
        p = _dot(on, wo_ref[cs, :])
        mix = p if mix is None else mix + p
    for gi in range(GMLP_G):
        cs = slice(gi * GMLP_DG, (gi + 1) * GMLP_DG)
        zs = []
        for r0 in range(0, tm, GMLP_CHUNK):
            zs.append(_dot(ws_ref[gi], vb_ref[r0:r0 + GMLP_CHUNK, cs]) + bs_ref[gi])
        z = zs[0] if len(zs) == 1 else jnp.concatenate(zs, axis=0)
        mix = mix + _dot(u_ref[:, cs] * z, wo_ref[GLA_H * GLA_DV + gi * GMLP_DG:GLA_H * GLA_DV + (gi + 1) * GMLP_DG, :])
    xo_ref[...] = x_ref[...] + _mod_rows(gt_ref, tm) * mix


def _ab_mix(x, mod, l, e, w, proj, s0, tm, chunk, ws, bs, s_stack):
    g_n, t_n, _ = x.shape
    carry = s0 is None
    in_specs = ([_tok_spec(tm, D_MODEL), _mod_spec(mod, l, 5, tm, t_n)] + [_tok_spec(tm, HP)] * 7
                + [_layer_spec(w["gla_norm"], e), _layer_spec(ws, e), _layer_spec(bs, e), _layer_spec(w["ab_w_out"], e)])
    args = [x, mod, *proj, w["gla_norm"], ws, bs, w["ab_w_out"]]
    scratch = [pltpu.VMEM((tm, HP), F32)]
    if carry:
        s_shape = (DEPTH - DEPTH // 2, g_n, GLA_H, GLA_DK, GLA_DV)
        s_spec = pl.BlockSpec((None, None, GLA_H, GLA_DK, GLA_DV), lambda g, i: (e, g, 0, 0, 0))
        scratch.append(pltpu.VMEM((GLA_H, LANES, LANES), F32))
    else:
        s_shape = s0.shape
        s_spec = pl.BlockSpec((None, None) + s0.shape[2:], lambda g, i: (e, g, 0, 0, 0, 0))
        in_specs.append(s_spec)
        args.append(s0)
    aliases = {}
    if s_stack is not None:
        aliases = {0: 1}
        in_specs.insert(0, pl.BlockSpec(memory_space=pl.ANY))
        args.insert(0, s_stack)
    return pl.pallas_call(
        functools.partial(_ab_mix_kernel, chunk=chunk, carry=carry, aliased=s_stack is not None),
        grid=(g_n, t_n // tm),
        in_specs=in_specs,
        out_specs=[_tok_spec(tm, D_MODEL), s_spec],
        out_shape=[jax.ShapeDtypeStruct(x.shape, F32), jax.ShapeDtypeStruct(s_shape, F32)],
        scratch_shapes=scratch,
        input_output_aliases=aliases,
        compiler_params=_params("parallel", "arbitrary"),
        name="ab_mix",
    )(*args)


def _ml_in_kernel(x_ref, sh_ref, sc_ref, nw_ref, w_ref, xm_ref, z_ref):
    tm = x_ref.shape[0]
    h = (_rms(x_ref[...], nw_ref[...]) * (1.0 + _mod_rows(sc_ref, tm)) + _mod_rows(sh_ref, tm)).astype(BF16)
    xm_ref[...] = jnp.dot(h, w_ref[:, :ML_INNER], preferred_element_type=F32)
    z_ref[...] = jnp.dot(h, w_ref[:, ML_INNER:], preferred_element_type=F32)


def _ml_in(x, mod, l, o, w, tm):
    g_n, t_n, _ = x.shape
    out = jax.ShapeDtypeStruct((g_n, t_n, ML_INNER), F32)
    return pl.pallas_call(
        _ml_in_kernel,
        grid=(g_n, t_n // tm),
        in_specs=[_tok_spec(tm, D_MODEL), _mod_spec(mod, l, 3, tm, t_n), _mod_spec(mod, l, 4, tm, t_n),
                  _layer_spec(w["mix_norm"], l), _layer_spec(w["ml_w_in"], o)],
        out_specs=[_tok_spec(tm, ML_INNER)] * 2,
        out_shape=[out] * 2,
        compiler_params=_params("parallel", "parallel"),
        name="ml_in",
    )(x, mod, mod, w["mix_norm"], w["ml_w_in"])


HALO = SUBLANES
CONV_FIRST = HALO - (ML_CONV - 1)


def _ml_qkv_gates(xm, ext_scr, nb, cw_ref, cb_ref, wq_ref, wk_ref, wv_ref, wg_ref, bg_ref, q_ref, k_ref, v_ref):
    tm = xm.shape[0]
    ls = tm // nb
    pre = cb_ref[...]
    for j in range(ML_CONV):
        pre = pre + ext_scr[:, CONV_FIRST + j:CONV_FIRST + j + ls, :].reshape(tm, ML_INNER) * cw_ref[j:j + 1, :]
    xc = _silu(pre)
    gates = bg_ref[...]
    for blk in range(ML_INNER // MXU_DIM):
        cs = slice(blk * MXU_DIM, (blk + 1) * MXU_DIM)
        xcb = xc[:, cs].astype(BF16)
        q = jnp.dot(xcb, wq_ref[blk], preferred_element_type=F32)
        k = jnp.dot(xcb, wk_ref[blk], preferred_element_type=F32)
        v = jnp.dot(xm[:, cs].astype(BF16), wv_ref[blk], preferred_element_type=F32)
        gates = (gates + _dot(q, wg_ref[cs, :]) + _dot(k, wg_ref[ML_INNER + blk * MXU_DIM:ML_INNER + (blk + 1) * MXU_DIM, :])
                 + _dot(v, wg_ref[2 * ML_INNER + blk * MXU_DIM:2 * ML_INNER + (blk + 1) * MXU_DIM, :]))
        q_ref[:, cs] = q.astype(q_ref.dtype)
        k_ref[:, cs] = (k * ML_DH ** -0.5).astype(k_ref.dtype)
        v_ref[:, cs] = v.astype(v_ref.dtype)
    lane = lax.broadcasted_iota(jnp.int32, gates.shape, 1)
    return xc, jnp.where(lane < ML_H, gates, _log_sigmoid(gates))


def _mlstm_head(q, k, v, ig_col, f_col, c_prev, n_prev, m_prev, causal, eye):
    n_rows = q.shape[0]
    ig_row = _col_to_row(ig_col, eye)
    f_row = _col_to_row(f_col, eye)
    log_d = jnp.where(causal, f_col - f_row + ig_row, -jnp.inf)
    log_inter = f_col + m_prev
    m_t = jnp.maximum(log_inter, jnp.max(log_d, axis=-1, keepdims=True))
    d = jnp.exp(log_d - m_t)
    w_inter = jnp.exp(log_inter - m_t)
    s = _dot_nt(q, k) * d
    num = _dot(s, v) + w_inter * _dot_nt(q, c_prev)
    den = jnp.sum(s, axis=-1, keepdims=True) + w_inter * jnp.sum(q.astype(F32) * n_prev, axis=-1, keepdims=True)
    hh = num / jnp.maximum(jnp.abs(den), jnp.exp(-m_t))
    m_new = m_t[n_rows - 1:n_rows, :]
    f_last = f_col[n_rows - 1:n_rows, :]
    w_rows = jnp.exp(f_last - f_col + ig_col - m_new)
    decay = jnp.exp(f_last + m_prev - m_new)
    kf = k.astype(F32)
    c_new = decay * c_prev + _dot_tn(v.astype(F32) * w_rows, kf)
    n_new = decay * n_prev + jnp.sum(kf * w_rows, axis=0, keepdims=True)
    return hh, c_new, n_new, m_new


def _ml_out_head(hh, xc, z, gn, sk, wo):
    mu = jnp.mean(hh, axis=-1, keepdims=True)
    hc = hh - mu
    var = jnp.mean(hc * hc, axis=-1, keepdims=True)
    hn = hc * lax.rsqrt(var + EPS) * gn
    return _dot((hn + sk * xc) * _silu(z), wo)


_ML_WEIGHTS = ["ml_conv_w", "ml_conv_b", "ml_wq", "ml_wk", "ml_wv", "ml_w_gates", "ml_b_gates"]
_ML_OUT_WEIGHTS = ["ml_norm", "ml_skip", "ml_w_out"]


def _ml_core_kernel(*refs, aliased):
    if aliased:
        refs = refs[1:]
    (x_ref, gt_ref, xm_ref, z_ref, cw_ref, cb_ref, wq_ref, wk_ref, wv_ref, wg_ref, bg_ref, gn_ref, sk_ref, wo_ref,
     xo_ref, co_ref, no_ref, mo_ref, cvo_ref, ext_scr, q_scr, k_scr, v_scr, xc_scr) = refs
    tm = x_ref.shape[0]
    causal, eye = _causal(tm)
    tril = jnp.where(causal, 1.0, 0.0).astype(BF16)

    @pl.when(pl.program_id(1) == 0)
    def _():
        ext_scr[:, 0:HALO, :] = jnp.zeros((1, HALO, ML_INNER), F32)
        co_ref[...] = jnp.zeros_like(co_ref)
        no_ref[...] = jnp.zeros_like(no_ref)
        mo_ref[...] = jnp.zeros_like(mo_ref)

    xm = xm_ref[...]
    ext_scr[:, HALO:HALO + tm, :] = xm.reshape(1, tm, ML_INNER)
    xc, gl = _ml_qkv_gates(xm, ext_scr, 1, cw_ref, cb_ref, wq_ref, wk_ref, wv_ref, wg_ref, bg_ref,
                           q_scr, k_scr, v_scr)
    xc_scr[...] = xc
    cvo_ref[...] = ext_scr[:, CONV_FIRST + tm:HALO + tm, :].reshape(cvo_ref.shape)
    ext_scr[:, 0:HALO, :] = ext_scr[:, tm:tm + HALO, :]

    cum = _dot01(tril, gl)
    mix = None
    for h in range(ML_H):
        cs = slice(h * ML_DH, (h + 1) * ML_DH)
        hh, c_new, n_new, m_new = _mlstm_head(
            q_scr[:, cs], k_scr[:, cs], v_scr[:, cs], gl[:, h:h + 1], cum[:, ML_H + h:ML_H + h + 1],
            co_ref[h], no_ref[h:h + 1, :], mo_ref[0:1, h:h + 1], causal, eye)
        co_ref[h] = c_new
        no_ref[h:h + 1, :] = n_new
        mo_ref[0:1, h:h + 1] = m_new
        p = _ml_out_head(hh, xc_scr[:, cs], z_ref[:, cs], gn_ref[:, cs], sk_ref[:, cs], wo_ref[cs, :])
        mix = p if mix is None else mix + p
    xo_ref[...] = x_ref[...] + _mod_rows(gt_ref, tm) * mix


def _ml_core(x, mod, l, o, w, xm, z, tm, c_stack):
    g_n, t_n, _ = x.shape
    c_spec = pl.BlockSpec((None, None, ML_H, ML_DH, ML_DH), lambda g, i: (o, g, 0, 0, 0))
    names = _ML_WEIGHTS + _ML_OUT_WEIGHTS
    in_specs = ([_tok_spec(tm, D_MODEL), _mod_spec(mod, l, 5, tm, t_n), _tok_spec(tm, ML_INNER), _tok_spec(tm, ML_INNER)]
                + [_layer_spec(w[n], o) for n in names])
    args = [x, mod, xm, z] + [w[n] for n in names]
    aliases = {}
    if c_stack is not None:
        aliases = {0: 1}
        in_specs.insert(0, pl.BlockSpec(memory_space=pl.ANY))
        args.insert(0, c_stack)
    return pl.pallas_call(
        functools.partial(_ml_core_kernel, aliased=c_stack is not None),
        grid=(g_n, t_n // tm),
        in_specs=in_specs,
        out_specs=[_tok_spec(tm, D_MODEL), c_spec,
                   pl.BlockSpec((None, ML_H, ML_DH), lambda g, i: (g, 0, 0)),
                   pl.BlockSpec((None, 1, ML_H), lambda g, i: (g, 0, 0)),
                   pl.BlockSpec((None, ML_CONV - 1, ML_INNER), lambda g, i: (g, 0, 0))],
        out_shape=[jax.ShapeDtypeStruct(x.shape, F32),
                   jax.ShapeDtypeStruct((DEPTH // 2, g_n, ML_H, ML_DH, ML_DH), F32),
                   jax.ShapeDtypeStruct((g_n, ML_H, ML_DH), F32),
                   jax.ShapeDtypeStruct((g_n, 1, ML_H), F32),
                   jax.ShapeDtypeStruct((g_n, ML_CONV - 1, ML_INNER), F32)],
        scratch_shapes=[pltpu.VMEM((1, HALO + tm, ML_INNER), F32)] + [pltpu.VMEM((tm, ML_INNER), BF16)] * 3
        + [pltpu.VMEM((tm, ML_INNER), F32)],
        input_output_aliases=aliases,
        compiler_params=_params("parallel", "arbitrary"),
        name="ml_core",
    )(*args)


def _ml_pre_kernel(xm_ref, c0_ref, cw_ref, cb_ref, wq_ref, wk_ref, wv_ref, wg_ref, bg_ref,
                   xc_ref, q_ref, k_ref, v_ref, gl_ref, co_ref, ext_scr, *, nb):
    tm = xm_ref.shape[0]
    ls = tm // nb
    xm = xm_ref[...]
    ext_scr[:, CONV_FIRST:HALO, :] = c0_ref[...]
    ext_scr[:, HALO:HALO + ls, :] = xm.reshape(nb, ls, ML_INNER)
    xc, gl = _ml_qkv_gates(xm, ext_scr, nb, cw_ref, cb_ref, wq_ref, wk_ref, wv_ref, wg_ref, bg_ref,
                           q_ref, k_ref, v_ref)
    xc_ref[...] = xc
    gl_ref[...] = gl
    co_ref[...] = ext_scr[:, CONV_FIRST + ls:HALO + ls, :]


def _ml_pre(xm, c0, o, w, nb):
    g_n, tm, _ = xm.shape
    c_spec = pl.BlockSpec((None,) + c0.shape[1:], lambda g, i: (g, 0, 0, 0))
    big = jax.ShapeDtypeStruct((g_n, tm, ML_INNER), F32)
    return pl.pallas_call(
        functools.partial(_ml_pre_kernel, nb=nb),
        grid=(g_n, 1),
        in_specs=[_tok_spec(tm, ML_INNER), c_spec] + [_layer_spec(w[n], o) for n in _ML_WEIGHTS],
        out_specs=[_tok_spec(tm, ML_INNER)] * 4 + [_tok_spec(tm, LANES), c_spec],
        out_shape=[big] * 4 + [jax.ShapeDtypeStruct((g_n, tm, LANES), F32), jax.ShapeDtypeStruct(c0.shape, F32)],
        scratch_shapes=[pltpu.VMEM((nb, HALO + tm // nb, ML_INNER), F32)],
        compiler_params=_params("parallel", "arbitrary"),
        name="ml_pre",
    )(xm, c0, *[w[n] for n in _ML_WEIGHTS])


def _ml_scan_kernel(*refs, aliased):
    if aliased:
        refs = refs[1:]
    (q_ref, k_ref, v_ref, gl_ref, c0_ref, n0_ref, m0_ref, hh_ref, co_ref, no_ref, mo_ref) = refs
    n_rows = q_ref.shape[0]
    causal, eye = _causal(n_rows)
    tril = jnp.where(causal, 1.0, 0.0).astype(BF16)
    gl = gl_ref[...]
    cum = _dot01(tril, gl)
    for h in range(ML_H):
        cs = slice(h * ML_DH, (h + 1) * ML_DH)
        hh, c_new, n_new, m_new = _mlstm_head(
            q_ref[:, cs], k_ref[:, cs], v_ref[:, cs], gl[:, h:h + 1], cum[:, ML_H + h:ML_H + h + 1],
            c0_ref[h], n0_ref[h:h + 1, :], m0_ref[0:1, h:h + 1], causal, eye)
        hh_ref[:, cs] = hh
        co_ref[h] = c_new
        no_ref[h:h + 1, :] = n_new
        mo_ref[0:1, h:h + 1] = m_new


def _ml_scan(q, k, v, gl, state, o, c_stack):
    g_n, tm, _ = q.shape
    c_spec = pl.BlockSpec((None, None, ML_H, ML_DH, ML_DH), lambda g, i: (o, g, 0, 0, 0))
    n_spec = pl.BlockSpec((None, None, ML_H, ML_DH), lambda g, i: (o, g, 0, 0))
    m_spec = pl.BlockSpec((None, None, 1, ML_H), lambda g, i: (o, g, 0, 0))
    in_specs = [_tok_spec(tm, ML_INNER)] * 3 + [_tok_spec(tm, LANES), c_spec, n_spec, m_spec]
    args = [q, k, v, gl, *state]
    aliases = {}
    if c_stack is not None:
        aliases = {0: 1}
        in_specs.insert(0, pl.BlockSpec(memory_space=pl.ANY))
        args.insert(0, c_stack)
    return pl.pallas_call(
        functools.partial(_ml_scan_kernel, aliased=c_stack is not None),
        grid=(g_n, 1),
        in_specs=in_specs,
        out_specs=[_tok_spec(tm, ML_INNER), c_spec,
                   pl.BlockSpec((None, ML_H, ML_DH), lambda g, i: (g, 0, 0)),
                   pl.BlockSpec((None, 1, ML_H), lambda g, i: (g, 0, 0))],
        out_shape=[jax.ShapeDtypeStruct((g_n, tm, ML_INNER), F32),
                   jax.ShapeDtypeStruct((DEPTH // 2, g_n, ML_H, ML_DH, ML_DH), F32),
                   jax.ShapeDtypeStruct((g_n, ML_H, ML_DH), F32),
                   jax.ShapeDtypeStruct((g_n, 1, ML_H), F32)],
        input_output_aliases=aliases,
        compiler_params=_params("parallel", "arbitrary"),
        name="ml_scan",
    )(*args)


def _ml_post_kernel(x_ref, gt_ref, hh_ref, xc_ref, z_ref, gn_ref, sk_ref, wo_ref, xo_ref):
    tm = x_ref.shape[0]
    mix = None
    for h in range(ML_H):
        cs = slice(h * ML_DH, (h + 1) * ML_DH)
        p = _ml_out_head(hh_ref[:, cs], xc_ref[:, cs], z_ref[:, cs], gn_ref[:, cs], sk_ref[:, cs], wo_ref[cs, :])
        mix = p if mix is None else mix + p
    xo_ref[...] = x_ref[...] + _mod_rows(gt_ref, tm) * mix


def _ml_post(x, mod, l, o, w, hh, xc, z, tm):
    g_n, t_n, _ = x.shape
    return pl.pallas_call(
        _ml_post_kernel,
        grid=(g_n, t_n // tm),
        in_specs=[_tok_spec(tm, D_MODEL), _mod_spec(mod, l, 5, tm, t_n)] + [_tok_spec(tm, ML_INNER)] * 3
        + [_layer_spec(w[n], o) for n in _ML_OUT_WEIGHTS],
        out_specs=_tok_spec(tm, D_MODEL),
        out_shape=jax.ShapeDtypeStruct(x.shape, F32),
        compiler_params=_params("parallel", "parallel"),
        name="ml_post",
    )(x, mod, hh, xc, z, *[w[n] for n in _ML_OUT_WEIGHTS])


def _pad_heads(a):
    lead = a.shape[:-1]
    a = a.reshape(lead + (GLA_H, GLA_DK))
    a = jnp.pad(a, [(0, 0)] * len(lead) + [(0, 0), (0, LANES - GLA_DK)])
    return a.reshape(lead + (HP,))


def _block_diag_dense(wb):
    rows = wb.reshape(wb.shape[0], -1, MXU_DIM, ML_BLOCK)
    dense = jnp.tile(rows, (1, 1, 1, MXU_DIM // ML_BLOCK))
    r_blk = lax.broadcasted_iota(jnp.int32, (MXU_DIM, MXU_DIM), 0) // ML_BLOCK
    c_blk = lax.broadcasted_iota(jnp.int32, (MXU_DIM, MXU_DIM), 1) // ML_BLOCK
    return jnp.where(r_blk == c_blk, dense, 0.0).astype(BF16)


def _prepare_weights(p):
    w = {}
    w["ffn_norm"] = p["ffn_norm"].reshape(DEPTH, 2, 1, D_MODEL)
    w["ffn_w_gate"] = p["ffn_w_gate"].astype(BF16)
    w["ffn_w_up"] = p["ffn_w_up"].astype(BF16)
    w["ffn_w_down"] = p["ffn_w_down"].astype(BF16)
    w["mix_norm"] = p["mix_norm"].reshape(DEPTH, 1, D_MODEL)
    w_in = p["ab_w_in"]
    qk = GLA_H * GLA_DK
    vw = GLA_H * GLA_DV
    o_k, o_v, o_g, o_a = qk, 2 * qk, 2 * qk + vw, 2 * qk + 2 * vw
    o_u = o_a + GLA_LOWRANK
    o_vb = o_u + GMLP_W
    w_a = jnp.pad(w_in[:, :, o_a:o_u], ((0, 0), (0, 0), (0, LANES - GLA_LOWRANK)))
    w["ab_w_in"] = jnp.concatenate(
        [_pad_heads(w_in[:, :, :o_k]), _pad_heads(w_in[:, :, o_k:o_v]), w_in[:, :, o_v:o_g], w_in[:, :, o_g:o_a],
         w_a, w_in[:, :, o_u:o_vb], w_in[:, :, o_vb:]], axis=-1).astype(BF16)
    w["gla_w_a2"] = jnp.pad(_pad_heads(p["gla_w_a2"]), ((0, 0), (0, LANES - GLA_LOWRANK), (0, 0))).astype(BF16)
    w["gla_b_a"] = _pad_heads(p["gla_b_a"])[:, None, :]
    w["gla_norm"] = p["gla_norm"][:, None, :]
    w["gmlp_norm"] = p["gmlp_norm"][:, None, :]
    w["ab_w_out"] = p["ab_w_out"].astype(BF16)
    w["ml_w_in"] = p["ml_w_in"].astype(BF16)
    w["ml_conv_w"] = p["ml_conv_w"]
    w["ml_conv_b"] = p["ml_conv_b"][:, None, :]
    w["ml_wq"] = _block_diag_dense(p["ml_wq"])
    w["ml_wk"] = _block_diag_dense(p["ml_wk"])
    w["ml_wv"] = _block_diag_dense(p["ml_wv"])
    w["ml_w_gates"] = jnp.pad(p["ml_w_gates"], ((0, 0), (0, 0), (0, LANES - 2 * ML_H))).astype(BF16)
    w["ml_b_gates"] = jnp.pad(p["ml_b_gates"], ((0, 0), (0, LANES - 2 * ML_H)))[:, None, :]
    w["ml_norm"] = p["ml_norm"][:, None, :]
    w["ml_skip"] = p["ml_skip"][:, None, :]
    w["ml_w_out"] = p["ml_w_out"].astype(BF16)
    w["final_norm"] = p["final_norm"][None, :]
    return w


def _spatial_weights(ws, bs, seq_len):
    if seq_len % GMLP_CHUNK == 0:
        length = GMLP_CHUNK
    else:
        length = seq_len
    reps = GMLP_CHUNK // length
    wt = jnp.tril(ws[:, :, :length, :length])
    eye = jnp.eye(reps, dtype=ws.dtype)
    wt = jnp.einsum("egts,ab->egatbs", wt, eye).reshape(ws.shape[0], GMLP_G, GMLP_CHUNK, GMLP_CHUNK)
    bt = jnp.tile(bs[:, :, :length], (1, 1, reps))
    bt = jnp.broadcast_to(bt[:, :, :, None], (ws.shape[0], GMLP_G, GMLP_CHUNK, GMLP_DG))
    return wt.astype(BF16), bt


def _trunk(x, mod, fmod, w, ws, bs, states, *, tm, mix_tm, gla_chunk, scan_tm, seq_rows):
    fresh = states is None
    g_n, t_n, _ = x.shape
    tm = min(tm, t_n)
    v_out, n_out, m_out, conv_out = [], [], [], []
    s_stack = c_stack = None
    for l in range(DEPTH):
        x = _ffn(x, mod, l, 0, w, min(FFN_ROW_TILE, t_n))
        if l % 2 == 0:
            e = l // 2
            proj = _ab_in(x, mod, l, e, w, tm)
            v_out.append(proj[6])
            if fresh:
                x, s_stack = _ab_mix(x, mod, l, e, w, proj, None, mix_tm, gla_chunk, ws, bs, s_stack)
            else:
                nseq = mix_tm // seq_rows
                view = lambda a: a.reshape(-1, mix_tm, a.shape[-1])
                s_all = states["gla_S"]
                s0 = s_all.reshape(s_all.shape[0], -1, nseq, GLA_H, GLA_DK, GLA_DV)
                mod_v = mod.reshape(mod.shape[0], mod.shape[1], -1, nseq, D_MODEL)
                xv, s_stack = _ab_mix(view(x), mod_v, l, e, w, [view(a) for a in proj], s0, mix_tm, gla_chunk, ws, bs,
                                      s_stack)
                x = xv.reshape(g_n, t_n, D_MODEL)
        else:
            o = l // 2
            xm, z = _ml_in(x, mod, l, o, w, tm)
            if fresh:
                x, c_stack, n_new, m_new, conv_new = _ml_core(x, mod, l, o, w, xm, z, scan_tm, c_stack)
                m_new = m_new.reshape(g_n, ML_H)
            else:
                nseq = LANES // seq_rows
                rows = nseq * seq_rows
                view = lambda a, r: a.reshape(-1, r, a.shape[-1])
                c0 = states["ml_conv"][o].reshape(-1, nseq, ML_CONV - 1, ML_INNER)
                xc, q, k, v, gl, conv_new = _ml_pre(view(xm, rows), c0, o, w, nseq)
                conv_new = conv_new.reshape(-1, ML_CONV - 1, ML_INNER)
                st = (states["ml_C"], states["ml_n"], states["ml_m"][:, :, None, :])
                hh, c_stack, n_new, m_new = _ml_scan(view(q, seq_rows), view(k, seq_rows), view(v, seq_rows),
                                                     view(gl, seq_rows), st, o, c_stack)
                m_new = m_new.reshape(-1, ML_H)
                xc = xc.reshape(g_n, t_n, ML_INNER)
                hh = hh.reshape(g_n, t_n, ML_INNER)
                x = _ml_post(x, mod, l, o, w, hh, xc, z, min(tm, 2 * ML_CHUNK))
            n_out.append(n_new)
            m_out.append(m_new)
            conv_out.append(conv_new)
        x = _ffn(x, mod, l, 1, w, min(FFN_ROW_TILE, t_n), fmod if l == DEPTH - 1 else None)
    y = x
    s_stack = s_stack.reshape(s_stack.shape[0], -1, GLA_H, GLA_DK, GLA_DV)
    return (y, s_stack, jnp.stack(v_out), c_stack, jnp.stack(n_out), jnp.stack(m_out),
            jnp.stack(conv_out))


def kernel(x_prompt, x_sample, c_prompt, c_sample, state_gla_S, state_mlstm_C, state_mlstm_n, state_mlstm_m, state_mlstm_conv, ada_w, ada_b, ffn_norm, ffn_w_gate, ffn_w_up, ffn_w_down, mix_norm, ab_w_in, gla_w_a2, gla_b_a, gla_norm, gmlp_norm, gmlp_ws, gmlp_bs, ab_w_out, ml_w_in, ml_conv_w, ml_conv_b, ml_wq, ml_wk, ml_wv, ml_w_gates, ml_b_gates, ml_norm, ml_skip, ml_w_out, final_norm, final_ada_w, final_ada_b):
    p = dict(ffn_norm=ffn_norm, ffn_w_gate=ffn_w_gate, ffn_w_up=ffn_w_up, ffn_w_down=ffn_w_down, mix_norm=mix_norm,
             ab_w_in=ab_w_in, gla_w_a2=gla_w_a2, gla_b_a=gla_b_a, gla_norm=gla_norm, gmlp_norm=gmlp_norm,
             ab_w_out=ab_w_out, ml_w_in=ml_w_in, ml_conv_w=ml_conv_w, ml_conv_b=ml_conv_b, ml_wq=ml_wq, ml_wk=ml_wk,
             ml_wv=ml_wv, ml_w_gates=ml_w_gates, ml_b_gates=ml_b_gates, ml_norm=ml_norm, ml_skip=ml_skip,
             ml_w_out=ml_w_out, final_norm=final_norm)
    w = _prepare_weights(p)
    n_p, t_p, _ = x_prompt.shape
    n_s, t_s, _ = x_sample.shape

    c_all = jnp.concatenate([c_prompt, c_sample], axis=0)
    mod = _ada(c_all, ada_w, ada_b)
    fmod = _ada(c_all, final_ada_w[None], final_ada_b[None])

    def split_mod(m):
        return m[:, :, :n_p, None, :], m[:, :, None, n_p:, :]

    mod_p, mod_s = split_mod(mod)
    fmod_p, fmod_s = split_mod(fmod)

    ws_p, bs_p = _spatial_weights(gmlp_ws, gmlp_bs, t_p)
    ws_s, bs_s = _spatial_weights(gmlp_ws, gmlp_bs, t_s)

    y_p, s_p, _, c_p, n_p_, m_p, cv_p = _trunk(
        x_prompt, mod_p, fmod_p, w, ws_p, bs_p, None, tm=ROW_TILE, mix_tm=ROW_TILE,
        gla_chunk=_chunk_len(t_p, GLA_TILE_CHUNK, GLA_CHUNK), scan_tm=_chunk_len(t_p, ML_TILE_CHUNK, ML_CHUNK),
        seq_rows=t_p)

    states = dict(gla_S=state_gla_S, ml_C=state_mlstm_C, ml_n=state_mlstm_n, ml_m=state_mlstm_m,
                  ml_conv=state_mlstm_conv)
    xs = x_sample.reshape(1, n_s * t_s, D_MODEL)
    y_s, s_s, v_s, c_s, n_s_, m_s, cv_s = _trunk(
        xs, mod_s, fmod_s, w, ws_s, bs_s, states, tm=ROW_TILE, mix_tm=GMLP_CHUNK, gla_chunk=t_s, scan_tm=t_s,
        seq_rows=t_s)
    y_s = y_s.reshape(n_s, t_s, D_MODEL)
    v_s = v_s.reshape(-1, n_s, t_s, GMLP_W)
    return (y_p, y_s, s_p, s_s, v_s, c_p, c_s, n_p_, n_s_, m_p, m_s, cv_p, cv_s)
```

```python
import functools

import jax
import jax.numpy as jnp
from jax import lax
from jax.experimental import pallas as pl
from jax.experimental.pallas import tpu as pltpu

F32 = jnp.float32
BF16 = jnp.bfloat16
EPS = 1e-6

D_MODEL = 1024
DEPTH = 4
D_FF = 2816
GLA_H = 4
GLA_DK = 64
GLA_DV = 128
GLA_LOWRANK = 16
GLA_TAU = 16.0
GLA_CHUNK = 64
GMLP_G = 4
GMLP_DG = 128
GMLP_W = GMLP_G * GMLP_DG
GMLP_CHUNK = 128
ML_INNER = 2 * D_MODEL
ML_H = 4
ML_DH = ML_INNER // ML_H
ML_CONV = 4
ML_BLOCK = 4
ML_CHUNK = 128

LANES = 128
SUBLANES = 8
MXU_DIM = 256
VMEM_LIMIT_BYTES = 56 * 1024 * 1024

ROW_TILE = 512
GLA_TILE_CHUNK = 128
ML_TILE_CHUNK = 256
FFN_ROW_TILE = 1024
SCAN_SEQS = 2


def _chunk_len(t, preferred, nominal):
    for c in (preferred, nominal):
        if t % c == 0:
            return c
    return t


HP = GLA_H * LANES
AB_Q, AB_K, AB_V, AB_G = 0, HP, 2 * HP, 3 * HP
AB_A = 4 * HP
AB_U = AB_A + LANES
AB_VB = AB_U + GMLP_W
AB_COLS = AB_VB + GMLP_W


def _dot(a, b):
    return jnp.dot(a.astype(BF16), b.astype(BF16), preferred_element_type=F32)


def _dot_nt(a, b):
    return lax.dot_general(a.astype(BF16), b.astype(BF16), (((1,), (1,)), ((), ())), preferred_element_type=F32)


def _dot_tn(a, b):
    return lax.dot_general(a.astype(BF16), b.astype(BF16), (((0,), (0,)), ((), ())), preferred_element_type=F32)


def _dot01(sel, x):
    hi = x.astype(BF16)
    r1 = x - hi.astype(F32)
    mid = r1.astype(BF16)
    lo = (r1 - mid.astype(F32)).astype(BF16)
    return (jnp.dot(sel, hi, preferred_element_type=F32) + jnp.dot(sel, mid, preferred_element_type=F32)
            + jnp.dot(sel, lo, preferred_element_type=F32))


def _rms(x, g):
    return x * lax.rsqrt(jnp.mean(x * x, axis=-1, keepdims=True) + EPS) * g


def _silu(x):
    return x * jax.nn.sigmoid(x)


def _log_sigmoid(x):
    return jnp.minimum(x, 0.0) - jnp.log1p(jnp.exp(-jnp.abs(x)))


def _causal(n):
    row = lax.broadcasted_iota(jnp.int32, (n, n), 0)
    col = lax.broadcasted_iota(jnp.int32, (n, n), 1)
    return row >= col, row == col


def _mod_rows(ref, tm):
    m = ref[...]
    r = m.shape[0]
    if r == 1 or r == tm:
        return m
    rep = tm // r
    if rep == SUBLANES:
        return jnp.broadcast_to(m[:, None, :], (r, rep, m.shape[1])).reshape(tm, m.shape[1])
    row = lax.broadcasted_iota(jnp.int32, (tm, r), 0)
    lo = lax.broadcasted_iota(jnp.int32, (tm, r), 1) * rep
    sel = jnp.where(row >= lo, jnp.where(row < lo + rep, 1.0, 0.0), 0.0).astype(BF16)
    return _dot01(sel, m)


def _col_to_row(col, eye):
    return jnp.sum(jnp.where(eye, col, 0.0), axis=0, keepdims=True)


def _tok_spec(tm, w):
    return pl.BlockSpec((None, tm, w), lambda g, i: (g, i, 0))


def _mod_spec(mod, l, j, tm, t_n):
    t_mod = mod.shape[3]
    if t_mod == 1:
        return pl.BlockSpec((None, None, None, 1, D_MODEL), lambda g, i: (l, j, g, 0, 0))
    return pl.BlockSpec((None, None, None, t_mod * tm // t_n, D_MODEL), lambda g, i: (l, j, g, i, 0))


def _layer_spec(a, *lead):
    n = len(lead)
    shape = (None,) * n + tuple(a.shape[n:])
    zeros = (0,) * (a.ndim - n)
    return pl.BlockSpec(shape, lambda g, i: tuple(lead) + zeros, pipeline_mode=pl.Buffered(1))


def _params(*sem):
    return pltpu.CompilerParams(dimension_semantics=sem, vmem_limit_bytes=VMEM_LIMIT_BYTES)


def _ada_kernel(c_ref, w_ref, b_ref, o_ref):
    cs = _silu(c_ref[...])
    o_ref[...] = _dot(cs, w_ref[...]) + b_ref[...]


def _ada(c, w, b):
    n_l, _, width = w.shape
    n_j = width // D_MODEL
    r = c.shape[0]
    return pl.pallas_call(
        _ada_kernel,
        grid=(n_l, n_j),
        in_specs=[pl.BlockSpec((r, D_MODEL), lambda l, j: (0, 0)),
                  pl.BlockSpec((None, D_MODEL, D_MODEL), lambda l, j: (l, 0, j)),
                  pl.BlockSpec((None, None, 1, D_MODEL), lambda l, j: (l, j, 0, 0))],
        out_specs=pl.BlockSpec((None, None, r, D_MODEL), lambda l, j: (l, j, 0, 0)),
        out_shape=jax.ShapeDtypeStruct((n_l, n_j, r, D_MODEL), F32),
        compiler_params=_params("arbitrary", "arbitrary"),
        name="ada",
    )(c, w, b.reshape(n_l, n_j, 1, D_MODEL))


FFN_CHUNK = 2 * MXU_DIM


def _ffn_kernel(x_ref, sh_ref, sc_ref, gt_ref, nw_ref, wg_ref, wu_ref, wd_ref, *rest):
    o_ref = rest[-1]
    x = x_ref[...]
    tm = x.shape[0]
    h = (_rms(x, nw_ref[...]) * (1.0 + _mod_rows(sc_ref, tm)) + _mod_rows(sh_ref, tm)).astype(BF16)
    acc = None
    for f0 in range(0, D_FF, FFN_CHUNK):
        f1 = min(f0 + FFN_CHUNK, D_FF)
        g = jnp.dot(h, wg_ref[:, f0:f1], preferred_element_type=F32)
        u = jnp.dot(h, wu_ref[:, f0:f1], preferred_element_type=F32)
        p = _dot(_silu(g) * u, wd_ref[f0:f1, :])
        acc = p if acc is None else acc + p
    y = x + (0.5 * _mod_rows(gt_ref, tm)) * acc
    if len(rest) > 1:
        fsh_ref, fsc_ref, fnw_ref, _ = rest
        y = _rms(y, fnw_ref[...]) * (1.0 + _mod_rows(fsc_ref, tm)) + _mod_rows(fsh_ref, tm)
    o_ref[...] = y


def _ffn(x, mod, l, s, w, tm, fmod=None):
    g_n, t_n, _ = x.shape
    j0 = 6 * s
    in_specs = [_tok_spec(tm, D_MODEL), _mod_spec(mod, l, j0, tm, t_n), _mod_spec(mod, l, j0 + 1, tm, t_n),
                _mod_spec(mod, l, j0 + 2, tm, t_n), _layer_spec(w["ffn_norm"], l, s),
                _layer_spec(w["ffn_w_gate"], l, s), _layer_spec(w["ffn_w_up"], l, s),
                _layer_spec(w["ffn_w_down"], l, s)]
    args = [x, mod, mod, mod, w["ffn_norm"], w["ffn_w_gate"], w["ffn_w_up"], w["ffn_w_down"]]
    if fmod is not None:
        in_specs += [_mod_spec(fmod, 0, 0, tm, t_n), _mod_spec(fmod, 0, 1, tm, t_n),
                     pl.BlockSpec((1, D_MODEL), lambda g, i: (0, 0))]
        args += [fmod, fmod, w["final_norm"]]
    return pl.pallas_call(
        _ffn_kernel,
        grid=(g_n, t_n // tm),
        in_specs=in_specs,
        out_specs=_tok_spec(tm, D_MODEL),
        out_shape=jax.ShapeDtypeStruct(x.shape, F32),
        compiler_params=_params("parallel", "parallel"),
        name="ffn",
    )(*args)


def _ab_in_kernel(x_ref, sh_ref, sc_ref, nw_ref, w_ref, wa2_ref, ba_ref, vn_ref,
                  q_ref, k_ref, v_ref, g_ref, la_ref, u_ref, vb_ref):
    tm = x_ref.shape[0]
    h = (_rms(x_ref[...], nw_ref[...]) * (1.0 + _mod_rows(sc_ref, tm)) + _mod_rows(sh_ref, tm)).astype(BF16)

    def seg(c0, width):
        return jnp.dot(h, w_ref[:, c0:c0 + width], preferred_element_type=F32)

    q_ref[...] = seg(AB_Q, HP) * GLA_DK ** -0.5
    k_ref[...] = seg(AB_K, HP)
    v_ref[...] = seg(AB_V, HP)
    g_ref[...] = seg(AB_G, HP)
    xa = _dot(seg(AB_A, LANES), wa2_ref[...]) + ba_ref[...]
    la_ref[...] = _log_sigmoid(xa) * (1.0 / GLA_TAU)
    u_ref[...] = seg(AB_U, GMLP_W)
    vb = seg(AB_VB, GMLP_W)
    for gi in range(GMLP_G):
        cs = slice(gi * GMLP_DG, (gi + 1) * GMLP_DG)
        vb_ref[:, cs] = _rms(vb[:, cs], vn_ref[...])


def _ab_in(x, mod, l, e, w, tm):
    g_n, t_n, _ = x.shape
    out = jax.ShapeDtypeStruct((g_n, t_n, HP), F32)
    return pl.pallas_call(
        _ab_in_kernel,
        grid=(g_n, t_n // tm),
        in_specs=[_tok_spec(tm, D_MODEL), _mod_spec(mod, l, 3, tm, t_n), _mod_spec(mod, l, 4, tm, t_n),
                  _layer_spec(w["mix_norm"], l), _layer_spec(w["ab_w_in"], e), _layer_spec(w["gla_w_a2"], e),
                  _layer_spec(w["gla_b_a"], e), _layer_spec(w["gmlp_norm"], e)],
        out_specs=[_tok_spec(tm, HP)] * 7,
        out_shape=[out] * 7,
        compiler_params=_params("parallel", "parallel"),
        name="ab_in",
    )(x, mod, mod, w["mix_norm"], w["ab_w_in"], w["gla_w_a2"], w["gla_b_a"], w["gmlp_norm"])


def _gla_head_chunk(q, k, v, la, st, tril, causal):
    n = q.shape[0]
    b = _dot01(tril, la)
    bm = b[n // 2 - 1:n // 2, :]
    a = jnp.where(causal, _dot_nt(q * jnp.exp(b - bm), k * jnp.exp(bm - b)), 0.0)
    o = _dot(a, v) + _dot_nt(q * jnp.exp(b), st)
    bl = b[n - 1:n, :]
    st_new = st * jnp.exp(bl) + _dot_tn(v, k * jnp.exp(bl - b))
    return o, st_new


def _gla_chunk_intra(q, k, v, la, tril, causal):
    n = q.shape[0]
    b = _dot01(tril, la)
    bm = b[n // 2 - 1:n // 2, :]
    bl = b[n - 1:n, :]
    qe = (q * jnp.exp(b - bm)).astype(BF16)
    ke = (k * jnp.exp(bm - b)).astype(BF16)
    vb = v.astype(BF16)
    o_intra = []
    for h in range(GLA_H):
        cs = slice(h * LANES, (h + 1) * LANES)
        a = jnp.where(causal, _dot_nt(qe[:, cs], ke[:, cs]), 0.0)
        o_intra.append(_dot(a, vb[:, cs]))
    return o_intra, (q * jnp.exp(b)).astype(BF16), (k * jnp.exp(bl - b)).astype(BF16), vb, jnp.exp(bl)


def _ab_mix_kernel(*refs, chunk, carry, aliased):
    if aliased:
        refs = refs[1:]
    if carry:
        (x_ref, gt_ref, q_ref, k_ref, v_ref, g_ref, la_ref, u_ref, vb_ref, gn_ref, ws_ref, bs_ref, wo_ref,
         xo_ref, so_ref, o_scr, st_scr) = refs
    else:
        (x_ref, gt_ref, q_ref, k_ref, v_ref, g_ref, la_ref, u_ref, vb_ref, gn_ref, ws_ref, bs_ref, wo_ref, s0_ref,
         xo_ref, so_ref, o_scr) = refs
    tm = q_ref.shape[0]
    causal, _ = _causal(chunk)
    tril = jnp.where(causal, 1.0, 0.0).astype(BF16)
    key_pad = jnp.zeros((LANES - GLA_DK, GLA_DV), F32)

    if carry:
        @pl.when(pl.program_id(1) == 0)
        def _():
            st_scr[...] = jnp.zeros_like(st_scr)

        n_chunks = tm // chunk
        intra = [_gla_chunk_intra(q_ref[c * chunk:(c + 1) * chunk, :], k_ref[c * chunk:(c + 1) * chunk, :],
                                  v_ref[c * chunk:(c + 1) * chunk, :], la_ref[c * chunk:(c + 1) * chunk, :],
                                  tril, causal) for c in range(n_chunks)]
        for h in range(GLA_H):
            cs = slice(h * LANES, (h + 1) * LANES)
            st = st_scr[h]
            for c in range(n_chunks):
                o_intra, qd, kd, vb, dec = intra[c]
                o_scr[c * chunk:(c + 1) * chunk, cs] = o_intra[h] + _dot_nt(qd[:, cs], st)
                st = st * dec[:, cs] + _dot_tn(vb[:, cs], kd[:, cs])
            st_scr[h] = st

        @pl.when(pl.program_id(1) == pl.num_programs(1) - 1)
        def _():
            for h in range(GLA_H):
                so_ref[h] = st_scr[h].T[:GLA_DK, :]
    else:
        def seq_body(c, _):
            rows = pl.ds(pl.multiple_of(c * chunk, chunk), chunk)
            for h in range(GLA_H):
                cs = slice(h * LANES, (h + 1) * LANES)
                st = jnp.concatenate([s0_ref[c, h], key_pad], axis=0).T
                o, st_new = _gla_head_chunk(q_ref[rows, cs], k_ref[rows, cs], v_ref[rows, cs], la_ref[rows, cs],
                                            st, tril, causal)
                o_scr[rows, cs] = o
                so_ref[c, h] = st_new.T[:GLA_DK, :]
            return 0

        lax.fori_loop(0, tm // chunk, seq_body, 0)

    mix = None
    for h in range(GLA_H):
        cs = slice(h * GLA_DV, (h + 1) * GLA_DV)
        on = _rms(o_scr[:, cs], gn_ref[...]) * _silu(g_ref[:, cs])
        p = _dot(on, wo_ref[cs, :])
        mix = p if mix is None else mix + p
    for gi in range(GMLP_G):
        cs = slice(gi * GMLP_DG, (gi + 1) * GMLP_DG)
        zs = []
        for r0 in range(0, tm, GMLP_CHUNK):
            zs.append(_dot(ws_ref[gi], vb_ref[r0:r0 + GMLP_CHUNK, cs]) + bs_ref[gi])
        z = zs[0] if len(zs) == 1 else jnp.concatenate(zs, axis=0)
        mix = mix + _dot(u_ref[:, cs] * z, wo_ref[GLA_H * GLA_DV + gi * GMLP_DG:GLA_H * GLA_DV + (gi + 1) * GMLP_DG, :])
    xo_ref[...] = x_ref[...] + _mod_rows(gt_ref, tm) * mix


def _ab_mix(x, mod, l, e, w, proj, s0, tm, chunk, ws, bs, s_stack):
    g_n, t_n, _ = x.shape
    carry = s0 is None
    in_specs = ([_tok_spec(tm, D_MODEL), _mod_spec(mod, l, 5, tm, t_n)] + [_tok_spec(tm, HP)] * 7
                + [_layer_spec(w["gla_norm"], e), _layer_spec(ws, e), _layer_spec(bs, e), _layer_spec(w["ab_w_out"], e)])
    args = [x, mod, *proj, w["gla_norm"], ws, bs, w["ab_w_out"]]
    scratch = [pltpu.VMEM((tm, HP), F32)]
    if carry:
        s_shape = (DEPTH - DEPTH // 2, g_n, GLA_H, GLA_DK, GLA_DV)
        s_spec = pl.BlockSpec((None, None, GLA_H, GLA_DK, GLA_DV), lambda g, i: (e, g, 0, 0, 0))
        scratch.append(pltpu.VMEM((GLA_H, LANES, LANES), F32))
    else:
        s_shape = s0.shape
        s_spec = pl.BlockSpec((None, None) + s0.shape[2:], lambda g, i: (e, g, 0, 0, 0, 0))
        in_specs.append(s_spec)
        args.append(s0)
    aliases = {}
    if s_stack is not None:
        aliases = {0: 1}
        in_specs.insert(0, pl.BlockSpec(memory_space=pl.ANY))
        args.insert(0, s_stack)
    return pl.pallas_call(
        functools.partial(_ab_mix_kernel, chunk=chunk, carry=carry, aliased=s_stack is not None),
        grid=(g_n, t_n // tm),
        in_specs=in_specs,
        out_specs=[_tok_spec(tm, D_MODEL), s_spec],
        out_shape=[jax.ShapeDtypeStruct(x.shape, F32), jax.ShapeDtypeStruct(s_shape, F32)],
        scratch_shapes=scratch,
        input_output_aliases=aliases,
        compiler_params=_params("parallel", "arbitrary"),
        name="ab_mix",
    )(*args)


def _ml_in_kernel(x_ref, sh_ref, sc_ref, nw_ref, w_ref, xm_ref, z_ref):
    tm = x_ref.shape[0]
    h = (_rms(x_ref[...], nw_ref[...]) * (1.0 + _mod_rows(sc_ref, tm)) + _mod_rows(sh_ref, tm)).astype(BF16)
    xm_ref[...] = jnp.dot(h, w_ref[:, :ML_INNER], preferred_element_type=F32)
    z_ref[...] = jnp.dot(h, w_ref[:, ML_INNER:], preferred_element_type=F32)


def _ml_in(x, mod, l, o, w, tm):
    g_n, t_n, _ = x.shape
    out = jax.ShapeDtypeStruct((g_n, t_n, ML_INNER), F32)
    return pl.pallas_call(
        _ml_in_kernel,
        grid=(g_n, t_n // tm),
        in_specs=[_tok_spec(tm, D_MODEL), _mod_spec(mod, l, 3, tm, t_n), _mod_spec(mod, l, 4, tm, t_n),
                  _layer_spec(w["mix_norm"], l), _layer_spec(w["ml_w_in"], o)],
        out_specs=[_tok_spec(tm, ML_INNER)] * 2,
        out_shape=[out] * 2,
        compiler_params=_params("parallel", "parallel"),
        name="ml_in",
    )(x, mod, mod, w["mix_norm"], w["ml_w_in"])


HALO = SUBLANES
CONV_FIRST = HALO - (ML_CONV - 1)


def _ml_qkv_gates(xm_ref, ext_scr, nb, cw_ref, cb_ref, wq_ref, wk_ref, wv_ref, wg_ref, bg_ref,
                  xc_ref, q_ref, k_ref, v_ref):
    tm = xm_ref.shape[0]
    ls = tm // nb
    gates = bg_ref[...]
    for blk in range(ML_INNER // MXU_DIM):
        cs = slice(blk * MXU_DIM, (blk + 1) * MXU_DIM)
        pre = cb_ref[:, cs]
        for j in range(ML_CONV):
            pre = pre + ext_scr[:, CONV_FIRST + j:CONV_FIRST + j + ls, cs].reshape(tm, MXU_DIM) * cw_ref[j:j + 1, cs]
        xc = _silu(pre)
        xc_ref[:, cs] = xc
        xcb = xc.astype(BF16)
        q = jnp.dot(xcb, wq_ref[blk], preferred_element_type=F32).astype(BF16)
        k = jnp.dot(xcb, wk_ref[blk], preferred_element_type=F32).astype(BF16)
        v = jnp.dot(xm_ref[:, cs].astype(BF16), wv_ref[blk], preferred_element_type=F32).astype(BF16)
        gates = (gates + jnp.dot(q, wg_ref[cs, :], preferred_element_type=F32)
                 + jnp.dot(k, wg_ref[ML_INNER + blk * MXU_DIM:ML_INNER + (blk + 1) * MXU_DIM, :],
                           preferred_element_type=F32)
                 + jnp.dot(v, wg_ref[2 * ML_INNER + blk * MXU_DIM:2 * ML_INNER + (blk + 1) * MXU_DIM, :],
                           preferred_element_type=F32))
        q_ref[:, cs] = q.astype(q_ref.dtype)
        k_ref[:, cs] = k.astype(k_ref.dtype)
        v_ref[:, cs] = v.astype(v_ref.dtype)
    lane = lax.broadcasted_iota(jnp.int32, gates.shape, 1)
    return jnp.where(lane < ML_H, gates, _log_sigmoid(gates))


def _mlstm_head(q, k, v, ig_col, f_col, c_prev, n_prev, m_prev, causal, eye):
    q, k, v = q.astype(BF16), k.astype(BF16), v.astype(BF16)
    n_rows = q.shape[0]
    scale = ML_DH ** -0.5
    ig_row = _col_to_row(ig_col, eye)
    f_row = _col_to_row(f_col, eye)
    log_d = jnp.where(causal, f_col - f_row + ig_row, -jnp.inf)
    log_inter = f_col + m_prev
    m_t = jnp.maximum(log_inter, jnp.max(log_d, axis=-1, keepdims=True))
    d = jnp.exp(log_d - m_t) * scale
    w_inter = jnp.exp(log_inter - m_t)
    s = _dot_nt(q, k) * d
    qn = _dot_nt(q, jnp.broadcast_to(n_prev, (SUBLANES, ML_DH)))[:, 0:1]
    num = _dot(s, v) + w_inter * _dot_nt(q, c_prev)
    den = jnp.sum(s, axis=-1, keepdims=True) + w_inter * qn
    hh = num / jnp.maximum(jnp.abs(den), jnp.exp(-m_t))
    m_new = m_t[n_rows - 1:n_rows, :]
    f_last = f_col[n_rows - 1:n_rows, :]
    w_rows = jnp.exp(f_last - f_col + ig_col - m_new) * scale
    w_wide = jnp.broadcast_to(w_rows, (n_rows, ML_DH)).astype(BF16)
    decay = jnp.exp(f_last + m_prev - m_new)
    c_new = decay * c_prev + _dot_tn(v * w_wide, k)
    n_new = decay * n_prev + _dot_tn(w_wide[:, 0:LANES], k)[0:1, :]
    return hh, c_new, n_new, m_new


def _ml_out_gate(hh, xc, z, gn, sk):
    mu = jnp.mean(hh, axis=-1, keepdims=True)
    hc = hh - mu
    var = jnp.mean(hc * hc, axis=-1, keepdims=True)
    hn = hc * lax.rsqrt(var + EPS) * gn
    return ((hn + sk * xc) * _silu(z)).astype(BF16)


_ML_WEIGHTS = ["ml_conv_w", "ml_conv_b", "ml_wq", "ml_wk", "ml_wv", "ml_w_gates", "ml_b_gates"]
_ML_OUT_WEIGHTS = ["ml_norm", "ml_skip", "ml_w_out"]


def _ml_core_kernel(*refs, aliased):
    if aliased:
        refs = refs[1:]
    (x_ref, gt_ref, xm_ref, z_ref, cw_ref, cb_ref, wq_ref, wk_ref, wv_ref, wg_ref, bg_ref, gn_ref, sk_ref, wo_ref,
     xo_ref, co_ref, no_ref, mo_ref, cvo_ref, ext_scr, q_scr, k_scr, v_scr, xc_scr) = refs
    tm = x_ref.shape[0]
    causal, eye = _causal(tm)
    tril = jnp.where(causal, 1.0, 0.0).astype(BF16)

    @pl.when(pl.program_id(1) == 0)
    def _():
        ext_scr[:, 0:HALO, :] = jnp.zeros((1, HALO, ML_INNER), F32)
        co_ref[...] = jnp.zeros_like(co_ref)
        no_ref[...] = jnp.zeros_like(no_ref)
        mo_ref[...] = jnp.zeros_like(mo_ref)

    ext_scr[:, HALO:HALO + tm, :] = xm_ref[...].reshape(1, tm, ML_INNER)
    gl = _ml_qkv_gates(xm_ref, ext_scr, 1, cw_ref, cb_ref, wq_ref, wk_ref, wv_ref, wg_ref, bg_ref,
                       xc_scr, q_scr, k_scr, v_scr)
    cvo_ref[...] = ext_scr[:, CONV_FIRST + tm:HALO + tm, :].reshape(cvo_ref.shape)
    ext_scr[:, 0:HALO, :] = ext_scr[:, tm:tm + HALO, :]

    cum = _dot01(tril, gl)
    mix = None
    for h in range(ML_H):
        cs = slice(h * ML_DH, (h + 1) * ML_DH)
        hh, c_new, n_new, m_new = _mlstm_head(
            q_scr[:, cs], k_scr[:, cs], v_scr[:, cs], gl[:, h:h + 1], cum[:, ML_H + h:ML_H + h + 1],
            co_ref[h], no_ref[h:h + 1, :], mo_ref[0:1, h:h + 1], causal, eye)
        co_ref[h] = c_new
        no_ref[h:h + 1, :] = n_new
        mo_ref[0:1, h:h + 1] = m_new
        p = jnp.dot(_ml_out_gate(hh, xc_scr[:, cs], z_ref[:, cs], gn_ref[:, cs], sk_ref[:, cs]), wo_ref[cs, :],
                    preferred_element_type=F32)
        mix = p if mix is None else mix + p
    xo_ref[...] = x_ref[...] + _mod_rows(gt_ref, tm) * mix


def _ml_core(x, mod, l, o, w, xm, z, tm, c_stack):
    g_n, t_n, _ = x.shape
    c_spec = pl.BlockSpec((None, None, ML_H, ML_DH, ML_DH), lambda g, i: (o, g, 0, 0, 0))
    names = _ML_WEIGHTS + _ML_OUT_WEIGHTS
    in_specs = ([_tok_spec(tm, D_MODEL), _mod_spec(mod, l, 5, tm, t_n), _tok_spec(tm, ML_INNER), _tok_spec(tm, ML_INNER)]
                + [_layer_spec(w[n], o) for n in names])
    args = [x, mod, xm, z] + [w[n] for n in names]
    aliases = {}
    if c_stack is not None:
        aliases = {0: 1}
        in_specs.insert(0, pl.BlockSpec(memory_space=pl.ANY))
        args.insert(0, c_stack)
    return pl.pallas_call(
        functools.partial(_ml_core_kernel, aliased=c_stack is not None),
        grid=(g_n, t_n // tm),
        in_specs=in_specs,
        out_specs=[_tok_spec(tm, D_MODEL), c_spec,
                   pl.BlockSpec((None, ML_H, ML_DH), lambda g, i: (g, 0, 0)),
                   pl.BlockSpec((None, 1, ML_H), lambda g, i: (g, 0, 0)),
                   pl.BlockSpec((None, ML_CONV - 1, ML_INNER), lambda g, i: (g, 0, 0))],
        out_shape=[jax.ShapeDtypeStruct(x.shape, F32),
                   jax.ShapeDtypeStruct((DEPTH // 2, g_n, ML_H, ML_DH, ML_DH), F32),
                   jax.ShapeDtypeStruct((g_n, ML_H, ML_DH), F32),
                   jax.ShapeDtypeStruct((g_n, 1, ML_H), F32),
                   jax.ShapeDtypeStruct((g_n, ML_CONV - 1, ML_INNER), F32)],
        scratch_shapes=[pltpu.VMEM((1, HALO + tm, ML_INNER), F32)] + [pltpu.VMEM((tm, ML_INNER), BF16)] * 3
        + [pltpu.VMEM((tm, ML_INNER), F32)],
        input_output_aliases=aliases,
        compiler_params=_params("parallel", "arbitrary"),
        name="ml_core",
    )(*args)


def _ml_pre_kernel(xm_ref, c0_ref, cw_ref, cb_ref, wq_ref, wk_ref, wv_ref, wg_ref, bg_ref,
                   xc_ref, q_ref, k_ref, v_ref, gl_ref, co_ref, ext_scr, *, nb):
    tm = xm_ref.shape[0]
    ls = tm // nb
    ext_scr[:, CONV_FIRST:HALO, :] = c0_ref[...]
    ext_scr[:, HALO:HALO + ls, :] = xm_ref[...].reshape(nb, ls, ML_INNER)
    gl_ref[...] = _ml_qkv_gates(xm_ref, ext_scr, nb, cw_ref, cb_ref, wq_ref, wk_ref, wv_ref, wg_ref, bg_ref,
                                xc_ref, q_ref, k_ref, v_ref)
    co_ref[...] = ext_scr[:, CONV_FIRST + ls:HALO + ls, :]


def _ml_pre(xm, c0, o, w, nb):
    g_n, tm, _ = xm.shape
    c_spec = pl.BlockSpec((None,) + c0.shape[1:], lambda g, i: (g, 0, 0, 0))
    big = jax.ShapeDtypeStruct((g_n, tm, ML_INNER), F32)
    return pl.pallas_call(
        functools.partial(_ml_pre_kernel, nb=nb),
        grid=(g_n, 1),
        in_specs=[_tok_spec(tm, ML_INNER), c_spec] + [_layer_spec(w[n], o) for n in _ML_WEIGHTS],
        out_specs=[_tok_spec(tm, ML_INNER)] * 4 + [_tok_spec(tm, LANES), c_spec],
        out_shape=[big] * 4 + [jax.ShapeDtypeStruct((g_n, tm, LANES), F32), jax.ShapeDtypeStruct(c0.shape, F32)],
        scratch_shapes=[pltpu.VMEM((nb, HALO + tm // nb, ML_INNER), F32)],
        compiler_params=_params("parallel", "arbitrary"),
        name="ml_pre",
    )(xm, c0, *[w[n] for n in _ML_WEIGHTS])


def _ml_scan_kernel(*refs, aliased):
    if aliased:
        refs = refs[1:]
    (q_ref, k_ref, v_ref, gl_ref, c0_ref, n0_ref, m0_ref, hh_ref, co_ref, no_ref, mo_ref) = refs
    n_seq = c0_ref.shape[0]
    n_rows = q_ref.shape[0] // n_seq
    causal, eye = _causal(n_rows)
    tril = jnp.where(causal, 1.0, 0.0).astype(BF16)
    for s in range(n_seq):
        rows = slice(s * n_rows, (s + 1) * n_rows)
        gl = gl_ref[rows, :]
        cum = _dot01(tril, gl)
        for h in range(ML_H):
            cs = slice(h * ML_DH, (h + 1) * ML_DH)
            hh, c_new, n_new, m_new = _mlstm_head(
                q_ref[rows, cs], k_ref[rows, cs], v_ref[rows, cs], gl[:, h:h + 1], cum[:, ML_H + h:ML_H + h + 1],
                c0_ref[s, h], n0_ref[s, h:h + 1, :], m0_ref[s, 0:1, h:h + 1], causal, eye)
            hh_ref[rows, cs] = hh
            co_ref[s, h] = c_new
            no_ref[s, h:h + 1, :] = n_new
            mo_ref[s, 0:1, h:h + 1] = m_new


def _ml_scan(q, k, v, gl, state, o, c_stack, n_seq):
    g_n, tm, _ = q.shape
    b_n = g_n * n_seq
    c_spec = pl.BlockSpec((None, n_seq, ML_H, ML_DH, ML_DH), lambda g, i: (o, g, 0, 0, 0))
    n_spec = pl.BlockSpec((None, n_seq, ML_H, ML_DH), lambda g, i: (o, g, 0, 0))
    m_spec = pl.BlockSpec((None, n_seq, 1, ML_H), lambda g, i: (o, g, 0, 0))
    in_specs = [_tok_spec(tm, ML_INNER)] * 3 + [_tok_spec(tm, LANES), c_spec, n_spec, m_spec]
    args = [q, k, v, gl, *state]
    aliases = {}
    if c_stack is not None:
        aliases = {0: 1}
        in_specs.insert(0, pl.BlockSpec(memory_space=pl.ANY))
        args.insert(0, c_stack)
    return pl.pallas_call(
        functools.partial(_ml_scan_kernel, aliased=c_stack is not None),
        grid=(g_n, 1),
        in_specs=in_specs,
        out_specs=[_tok_spec(tm, ML_INNER), c_spec,
                   pl.BlockSpec((n_seq, ML_H, ML_DH), lambda g, i: (g, 0, 0)),
                   pl.BlockSpec((n_seq, 1, ML_H), lambda g, i: (g, 0, 0))],
        out_shape=[jax.ShapeDtypeStruct((g_n, tm, ML_INNER), F32),
                   jax.ShapeDtypeStruct((DEPTH // 2, b_n, ML_H, ML_DH, ML_DH), F32),
                   jax.ShapeDtypeStruct((b_n, ML_H, ML_DH), F32),
                   jax.ShapeDtypeStruct((b_n, 1, ML_H), F32)],
        input_output_aliases=aliases,
        compiler_params=_params("parallel", "arbitrary"),
        name="ml_scan",
    )(*args)


def _ml_post_kernel(x_ref, gt_ref, hh_ref, xc_ref, z_ref, gn_ref, sk_ref, wo_ref, xo_ref):
    tm = x_ref.shape[0]
    outs = []
    for h in range(ML_H):
        cs = slice(h * ML_DH, (h + 1) * ML_DH)
        outs.append(_ml_out_gate(hh_ref[:, cs], xc_ref[:, cs], z_ref[:, cs], gn_ref[:, cs], sk_ref[:, cs]))
    mix = jnp.dot(jnp.concatenate(outs, axis=1), wo_ref[...], preferred_element_type=F32)
    xo_ref[...] = x_ref[...] + _mod_rows(gt_ref, tm) * mix


def _ml_post(x, mod, l, o, w, hh, xc, z, tm):
    g_n, t_n, _ = x.shape
    return pl.pallas_call(
        _ml_post_kernel,
        grid=(g_n, t_n // tm),
        in_specs=[_tok_spec(tm, D_MODEL), _mod_spec(mod, l, 5, tm, t_n)] + [_tok_spec(tm, ML_INNER)] * 3
        + [_layer_spec(w[n], o) for n in _ML_OUT_WEIGHTS],
        out_specs=_tok_spec(tm, D_MODEL),
        out_shape=jax.ShapeDtypeStruct(x.shape, F32),
        compiler_params=_params("parallel", "parallel"),
        name="ml_post",
    )(x, mod, hh, xc, z, *[w[n] for n in _ML_OUT_WEIGHTS])


def _pad_heads(a):
    lead = a.shape[:-1]
    a = a.reshape(lead + (GLA_H, GLA_DK))
    a = jnp.pad(a, [(0, 0)] * len(lead) + [(0, 0), (0, LANES - GLA_DK)])
    return a.reshape(lead + (HP,))


def _block_diag_dense(wb):
    rows = wb.reshape(wb.shape[0], -1, MXU_DIM, ML_BLOCK)
    dense = jnp.tile(rows, (1, 1, 1, MXU_DIM // ML_BLOCK))
    r_blk = lax.broadcasted_iota(jnp.int32, (MXU_DIM, MXU_DIM), 0) // ML_BLOCK
    c_blk = lax.broadcasted_iota(jnp.int32, (MXU_DIM, MXU_DIM), 1) // ML_BLOCK
    return jnp.where(r_blk == c_blk, dense, 0.0).astype(BF16)


def _prepare_weights(p):
    w = {}
    w["ffn_norm"] = p["ffn_norm"].reshape(DEPTH, 2, 1, D_MODEL)
    w["ffn_w_gate"] = p["ffn_w_gate"].astype(BF16)
    w["ffn_w_up"] = p["ffn_w_up"].astype(BF16)
    w["ffn_w_down"] = p["ffn_w_down"].astype(BF16)
    w["mix_norm"] = p["mix_norm"].reshape(DEPTH, 1, D_MODEL)
    w_in = p["ab_w_in"]
    qk = GLA_H * GLA_DK
    vw = GLA_H * GLA_DV
    o_k, o_v, o_g, o_a = qk, 2 * qk, 2 * qk + vw, 2 * qk + 2 * vw
    o_u = o_a + GLA_LOWRANK
    o_vb = o_u + GMLP_W
    w_a = jnp.pad(w_in[:, :, o_a:o_u], ((0, 0), (0, 0), (0, LANES - GLA_LOWRANK)))
    w["ab_w_in"] = jnp.concatenate(
        [_pad_heads(w_in[:, :, :o_k]), _pad_heads(w_in[:, :, o_k:o_v]), w_in[:, :, o_v:o_g], w_in[:, :, o_g:o_a],
         w_a, w_in[:, :, o_u:o_vb], w_in[:, :, o_vb:]], axis=-1).astype(BF16)
    w["gla_w_a2"] = jnp.pad(_pad_heads(p["gla_w_a2"]), ((0, 0), (0, LANES - GLA_LOWRANK), (0, 0))).astype(BF16)
    w["gla_b_a"] = _pad_heads(p["gla_b_a"])[:, None, :]
    w["gla_norm"] = p["gla_norm"][:, None, :]
    w["gmlp_norm"] = p["gmlp_norm"][:, None, :]
    w["ab_w_out"] = p["ab_w_out"].astype(BF16)
    w["ml_w_in"] = p["ml_w_in"].astype(BF16)
    w["ml_conv_w"] = p["ml_conv_w"]
    w["ml_conv_b"] = p["ml_conv_b"][:, None, :]
    w["ml_wq"] = _block_diag_dense(p["ml_wq"])
    w["ml_wk"] = _block_diag_dense(p["ml_wk"])
    w["ml_wv"] = _block_diag_dense(p["ml_wv"])
    w["ml_w_gates"] = jnp.pad(p["ml_w_gates"], ((0, 0), (0, 0), (0, LANES - 2 * ML_H))).astype(BF16)
    w["ml_b_gates"] = jnp.pad(p["ml_b_gates"], ((0, 0), (0, LANES - 2 * ML_H)))[:, None, :]
    w["ml_norm"] = p["ml_norm"][:, None, :]
    w["ml_skip"] = p["ml_skip"][:, None, :]
    w["ml_w_out"] = p["ml_w_out"].astype(BF16)
    w["final_norm"] = p["final_norm"][None, :]
    return w


def _spatial_weights(ws, bs, seq_len):
    if seq_len % GMLP_CHUNK == 0:
        length = GMLP_CHUNK
    else:
        length = seq_len
    reps = GMLP_CHUNK // length
    wt = jnp.tril(ws[:, :, :length, :length])
    eye = jnp.eye(reps, dtype=ws.dtype)
    wt = jnp.einsum("egts,ab->egatbs", wt, eye).reshape(ws.shape[0], GMLP_G, GMLP_CHUNK, GMLP_CHUNK)
    bt = jnp.tile(bs[:, :, :length], (1, 1, reps))
    bt = jnp.broadcast_to(bt[:, :, :, None], (ws.shape[0], GMLP_G, GMLP_CHUNK, GMLP_DG))
    return wt.astype(BF16), bt


def _trunk(x, mod, fmod, w, ws, bs, states, *, tm, mix_tm, gla_chunk, scan_tm, seq_rows):
    fresh = states is None
    g_n, t_n, _ = x.shape
    tm = min(tm, t_n)
    v_out, n_out, m_out, conv_out = [], [], [], []
    s_stack = c_stack = None
    for l in range(DEPTH):
        x = _ffn(x, mod, l, 0, w, min(FFN_ROW_TILE, t_n))
        if l % 2 == 0:
            e = l // 2
            proj = _ab_in(x, mod, l, e, w, tm)
            v_out.append(proj[6])
            if fresh:
                x, s_stack = _ab_mix(x, mod, l, e, w, proj, None, mix_tm, gla_chunk, ws, bs, s_stack)
            else:
                nseq = mix_tm // seq_rows
                view = lambda a: a.reshape(-1, mix_tm, a.shape[-1])
                s_all = states["gla_S"]
                s0 = s_all.reshape(s_all.shape[0], -1, nseq, GLA_H, GLA_DK, GLA_DV)
                mod_v = mod.reshape(mod.shape[0], mod.shape[1], -1, nseq, D_MODEL)
                xv, s_stack = _ab_mix(view(x), mod_v, l, e, w, [view(a) for a in proj], s0, mix_tm, gla_chunk, ws, bs,
                                      s_stack)
                x = xv.reshape(g_n, t_n, D_MODEL)
        else:
            o = l // 2
            xm, z = _ml_in(x, mod, l, o, w, tm)
            if fresh:
                x, c_stack, n_new, m_new, conv_new = _ml_core(x, mod, l, o, w, xm, z, scan_tm, c_stack)
                m_new = m_new.reshape(g_n, ML_H)
            else:
                nseq = LANES // seq_rows
                rows = nseq * seq_rows
                view = lambda a, r: a.reshape(-1, r, a.shape[-1])
                c0 = states["ml_conv"][o].reshape(-1, nseq, ML_CONV - 1, ML_INNER)
                xc, q, k, v, gl, conv_new = _ml_pre(view(xm, rows), c0, o, w, nseq)
                conv_new = conv_new.reshape(-1, ML_CONV - 1, ML_INNER)
                st = (states["ml_C"], states["ml_n"], states["ml_m"][:, :, None, :])
                srows = SCAN_SEQS * seq_rows
                hh, c_stack, n_new, m_new = _ml_scan(view(q, srows), view(k, srows), view(v, srows),
                                                     view(gl, srows), st, o, c_stack, SCAN_SEQS)
                m_new = m_new.reshape(-1, ML_H)
                xc = xc.reshape(g_n, t_n, ML_INNER)
                hh = hh.reshape(g_n, t_n, ML_INNER)
                x = _ml_post(x, mod, l, o, w, hh, xc, z, min(tm, 2 * ML_CHUNK))
            n_out.append(n_new)
            m_out.append(m_new)
            conv_out.append(conv_new)
        x = _ffn(x, mod, l, 1, w, min(FFN_ROW_TILE, t_n), fmod if l == DEPTH - 1 else None)
    y = x
    s_stack = s_stack.reshape(s_stack.shape[0], -1, GLA_H, GLA_DK, GLA_DV)
    return (y, s_stack, jnp.stack(v_out), c_stack, jnp.stack(n_out), jnp.stack(m_out),
            jnp.stack(conv_out))


def kernel(x_prompt, x_sample, c_prompt, c_sample, state_gla_S, state_mlstm_C, state_mlstm_n, state_mlstm_m, state_mlstm_conv, ada_w, ada_b, ffn_norm, ffn_w_gate, ffn_w_up, ffn_w_down, mix_norm, ab_w_in, gla_w_a2, gla_b_a, gla_norm, gmlp_norm, gmlp_ws, gmlp_bs, ab_w_out, ml_w_in, ml_conv_w, ml_conv_b, ml_wq, ml_wk, ml_wv, ml_w_gates, ml_b_gates, ml_norm, ml_skip, ml_w_out, final_norm, final_ada_w, final_ada_b):
    p = dict(ffn_norm=ffn_norm, ffn_w_gate=ffn_w_gate, ffn_w_up=ffn_w_up, ffn_w_down=ffn_w_down, mix_norm=mix_norm,
             ab_w_in=ab_w_in, gla_w_a2=gla_w_a2, gla_b_a=gla_b_a, gla_norm=gla_norm, gmlp_norm=gmlp_norm,
             ab_w_out=ab_w_out, ml_w_in=ml_w_in, ml_conv_w=ml_conv_w, ml_conv_b=ml_conv_b, ml_wq=ml_wq, ml_wk=ml_wk,
             ml_wv=ml_wv, ml_w_gates=ml_w_gates, ml_b_gates=ml_b_gates, ml_norm=ml_norm, ml_skip=ml_skip,
             ml_w_out=ml_w_out, final_norm=final_norm)
    w = _prepare_weights(p)
    n_p, t_p, _ = x_prompt.shape
    n_s, t_s, _ = x_sample.shape

    c_all = jnp.concatenate([c_prompt, c_sample], axis=0)
    mod = _ada(c_all, ada_w, ada_b)
    fmod = _ada(c_all, final_ada_w[None], final_ada_b[None])

    def split_mod(m):
        return m[:, :, :n_p, None, :], m[:, :, None, n_p:, :]

    mod_p, mod_s = split_mod(mod)
    fmod_p, fmod_s = split_mod(fmod)

    ws_p, bs_p = _spatial_weights(gmlp_ws, gmlp_bs, t_p)
    ws_s, bs_s = _spatial_weights(gmlp_ws, gmlp_bs, t_s)

    y_p, s_p, _, c_p, n_p_, m_p, cv_p = _trunk(
        x_prompt, mod_p, fmod_p, w, ws_p, bs_p, None, tm=ROW_TILE, mix_tm=ROW_TILE,
        gla_chunk=_chunk_len(t_p, GLA_TILE_CHUNK, GLA_CHUNK), scan_tm=_chunk_len(t_p, ML_TILE_CHUNK, ML_CHUNK),
        seq_rows=t_p)

    states = dict(gla_S=state_gla_S, ml_C=state_mlstm_C, ml_n=state_mlstm_n, ml_m=state_mlstm_m,
                  ml_conv=state_mlstm_conv)
    xs = x_sample.reshape(1, n_s * t_s, D_MODEL)
    y_s, s_s, v_s, c_s, n_s_, m_s, cv_s = _trunk(
        xs, mod_s, fmod_s, w, ws_s, bs_s, states, tm=ROW_TILE, mix_tm=GMLP_CHUNK, gla_chunk=t_s, scan_tm=t_s,
        seq_rows=t_s)
    y_s = y_s.reshape(n_s, t_s, D_MODEL)
    v_s = v_s.reshape(-1, n_s, t_s, GMLP_W)
    return (y_p, y_s, s_p, s_s, v_s, c_p, c_s, n_p_, n_s_, m_p, m_s, cv_p, cv_s)
```

```python
import functools

import jax
import jax.numpy as jnp
from jax import lax
from jax.experimental import pallas as pl
from jax.experimental.pallas import tpu as pltpu

F32 = jnp.float32
BF16 = jnp.bfloat16
EPS = 1e-6

D_MODEL = 1024
DEPTH = 4
D_FF = 2816
GLA_H = 4
GLA_DK = 64
GLA_DV = 128
GLA_LOWRANK = 16
GLA_TAU = 16.0
GLA_CHUNK = 64
GMLP_G = 4
GMLP_DG = 128
GMLP_W = GMLP_G * GMLP_DG
GMLP_CHUNK = 128
ML_INNER = 2 * D_MODEL
ML_H = 4
ML_DH = ML_INNER // ML_H
ML_CONV = 4
ML_BLOCK = 4
ML_CHUNK = 128

LANES = 128
SUBLANES = 8
MXU_DIM = 256
VMEM_LIMIT_BYTES = 56 * 1024 * 1024

ROW_TILE = 512
GLA_TILE_CHUNK = 128
ML_TILE_CHUNK = 256
FFN_ROW_TILE = 1024
SCAN_SEQS = 2


def _chunk_len(t, preferred, nominal):
    for c in (preferred, nominal):
        if t % c == 0:
            return c
    return t


HP = GLA_H * LANES
AB_Q, AB_K, AB_V, AB_G = 0, HP, 2 * HP, 3 * HP
AB_A = 4 * HP
AB_U = AB_A + LANES
AB_VB = AB_U + GMLP_W
AB_COLS = AB_VB + GMLP_W


def _dot(a, b):
    return jnp.dot(a.astype(BF16), b.astype(BF16), preferred_element_type=F32)


def _dot_nt(a, b):
    return lax.dot_general(a.astype(BF16), b.astype(BF16), (((1,), (1,)), ((), ())), preferred_element_type=F32)


def _dot_tn(a, b):
    return lax.dot_general(a.astype(BF16), b.astype(BF16), (((0,), (0,)), ((), ())), preferred_element_type=F32)


def _dot01(sel, x):
    hi = x.astype(BF16)
    r1 = x - hi.astype(F32)
    mid = r1.astype(BF16)
    lo = (r1 - mid.astype(F32)).astype(BF16)
    return (jnp.dot(sel, hi, preferred_element_type=F32) + jnp.dot(sel, mid, preferred_element_type=F32)
            + jnp.dot(sel, lo, preferred_element_type=F32))


def _rms(x, g):
    return x * lax.rsqrt(jnp.mean(x * x, axis=-1, keepdims=True) + EPS) * g


def _silu(x):
    return x * jax.nn.sigmoid(x)


def _log_sigmoid(x):
    return jnp.minimum(x, 0.0) - jnp.log1p(jnp.exp(-jnp.abs(x)))


def _causal(n):
    row = lax.broadcasted_iota(jnp.int32, (n, n), 0)
    col = lax.broadcasted_iota(jnp.int32, (n, n), 1)
    return row >= col, row == col


def _mod_rows(ref, tm):
    m = ref[...]
    r = m.shape[0]
    if r == 1 or r == tm:
        return m
    rep = tm // r
    if rep == SUBLANES:
        return jnp.broadcast_to(m[:, None, :], (r, rep, m.shape[1])).reshape(tm, m.shape[1])
    row = lax.broadcasted_iota(jnp.int32, (tm, r), 0)
    lo = lax.broadcasted_iota(jnp.int32, (tm, r), 1) * rep
    sel = jnp.where(row >= lo, jnp.where(row < lo + rep, 1.0, 0.0), 0.0).astype(BF16)
    return _dot01(sel, m)


def _col_to_row(col, eye):
    return jnp.sum(jnp.where(eye, col, 0.0), axis=0, keepdims=True)


def _tok_spec(tm, w):
    return pl.BlockSpec((None, tm, w), lambda g, i: (g, i, 0))


def _mod_spec(mod, l, j, tm, t_n):
    t_mod = mod.shape[3]
    if t_mod == 1:
        return pl.BlockSpec((None, None, None, 1, D_MODEL), lambda g, i: (l, j, g, 0, 0))
    return pl.BlockSpec((None, None, None, t_mod * tm // t_n, D_MODEL), lambda g, i: (l, j, g, i, 0))


def _layer_spec(a, *lead):
    n = len(lead)
    shape = (None,) * n + tuple(a.shape[n:])
    zeros = (0,) * (a.ndim - n)
    return pl.BlockSpec(shape, lambda g, i: tuple(lead) + zeros, pipeline_mode=pl.Buffered(1))


def _params(*sem):
    return pltpu.CompilerParams(dimension_semantics=sem, vmem_limit_bytes=VMEM_LIMIT_BYTES)


def _ada_kernel(c_ref, w_ref, b_ref, o_ref):
    cs = _silu(c_ref[...])
    o_ref[...] = _dot(cs, w_ref[...]) + b_ref[...]


def _ada(c, w, b):
    n_l, _, width = w.shape
    n_j = width // D_MODEL
    r = c.shape[0]
    return pl.pallas_call(
        _ada_kernel,
        grid=(n_l, n_j),
        in_specs=[pl.BlockSpec((r, D_MODEL), lambda l, j: (0, 0)),
                  pl.BlockSpec((None, D_MODEL, D_MODEL), lambda l, j: (l, 0, j)),
                  pl.BlockSpec((None, None, 1, D_MODEL), lambda l, j: (l, j, 0, 0))],
        out_specs=pl.BlockSpec((None, None, r, D_MODEL), lambda l, j: (l, j, 0, 0)),
        out_shape=jax.ShapeDtypeStruct((n_l, n_j, r, D_MODEL), F32),
        compiler_params=_params("arbitrary", "arbitrary"),
        name="ada",
    )(c, w, b.reshape(n_l, n_j, 1, D_MODEL))


FFN_CHUNK = 2 * MXU_DIM


def _ffn_kernel(*refs, final, cast):
    x_ref, sh_ref, sc_ref, gt_ref, nw_ref, wg_ref, wu_ref, wd_ref = refs[:8]
    n_in = 8 + (3 if final else 0) + (3 if cast else 0)
    o_ref = refs[n_in]
    x = x_ref[...]
    tm = x.shape[0]
    h = (_rms(x, nw_ref[...]) * (1.0 + _mod_rows(sc_ref, tm)) + _mod_rows(sh_ref, tm)).astype(BF16)
    acc = None
    for f0 in range(0, D_FF, FFN_CHUNK):
        f1 = min(f0 + FFN_CHUNK, D_FF)
        g = jnp.dot(h, wg_ref[:, f0:f1], preferred_element_type=F32)
        u = jnp.dot(h, wu_ref[:, f0:f1], preferred_element_type=F32)
        p = _dot(_silu(g) * u, wd_ref[f0:f1, :])
        acc = p if acc is None else acc + p
    y = x + (0.5 * _mod_rows(gt_ref, tm)) * acc
    if final:
        fsh_ref, fsc_ref, fnw_ref = refs[8:11]
        y = _rms(y, fnw_ref[...]) * (1.0 + _mod_rows(fsc_ref, tm)) + _mod_rows(fsh_ref, tm)
    o_ref[...] = y
    if cast:
        for src, dst in zip(refs[n_in - 3:n_in], refs[n_in + 1:n_in + 4]):
            dst[...] = src[...].astype(BF16)


def _ffn(x, mod, l, s, w, ffn_w, tm, fmod=None, cast=None):
    g_n, t_n, _ = x.shape
    n_i = t_n // tm
    j0 = 6 * s
    whole = lambda a: pl.BlockSpec(a.shape, lambda g, i: (0, 0), pipeline_mode=pl.Buffered(1))
    in_specs = [_tok_spec(tm, D_MODEL), _mod_spec(mod, l, j0, tm, t_n), _mod_spec(mod, l, j0 + 1, tm, t_n),
                _mod_spec(mod, l, j0 + 2, tm, t_n), _layer_spec(w["ffn_norm"], l, s)] + [whole(a) for a in ffn_w]
    args = [x, mod, mod, mod, w["ffn_norm"], *ffn_w]
    out_specs = [_tok_spec(tm, D_MODEL)]
    out_shape = [jax.ShapeDtypeStruct(x.shape, F32)]
    if fmod is not None:
        in_specs += [_mod_spec(fmod, 0, 0, tm, t_n), _mod_spec(fmod, 0, 1, tm, t_n),
                     pl.BlockSpec((1, D_MODEL), lambda g, i: (0, 0))]
        args += [fmod, fmod, w["final_norm"]]
    if cast is not None:
        l2, s2 = cast
        steps = g_n * n_i
        for name in ("ffn_w_gate", "ffn_w_up", "ffn_w_down"):
            src = w[name]
            rows, cols = src.shape[2:]
            in_specs.append(pl.BlockSpec((None, None, rows // steps, cols), lambda g, i: (l2, s2, g * n_i + i, 0)))
            args.append(src)
            out_specs.append(pl.BlockSpec((rows // steps, cols), lambda g, i: (g * n_i + i, 0)))
            out_shape.append(jax.ShapeDtypeStruct((rows, cols), BF16))
    out = pl.pallas_call(
        functools.partial(_ffn_kernel, final=fmod is not None, cast=cast is not None),
        grid=(g_n, n_i),
        in_specs=in_specs,
        out_specs=out_specs,
        out_shape=out_shape,
        compiler_params=_params("parallel", "parallel"),
        name="ffn",
    )(*args)
    return out[0], tuple(out[1:])


def _ab_project(h, w_ref, wa2_ref, ba_ref, vn_ref, q_ref, k_ref, v_ref, g_ref, la_ref, u_ref, vb_ref):
    def seg(c0, width):
        return jnp.dot(h, w_ref[:, c0:c0 + width], preferred_element_type=F32)

    q_ref[...] = seg(AB_Q, HP) * GLA_DK ** -0.5
    k_ref[...] = seg(AB_K, HP)
    v_ref[...] = seg(AB_V, HP)
    g_ref[...] = seg(AB_G, HP)
    xa = _dot(seg(AB_A, LANES), wa2_ref[...]) + ba_ref[...]
    la_ref[...] = _log_sigmoid(xa) * (1.0 / GLA_TAU)
    u_ref[...] = seg(AB_U, GMLP_W)
    vb = seg(AB_VB, GMLP_W)
    for gi in range(GMLP_G):
        cs = slice(gi * GMLP_DG, (gi + 1) * GMLP_DG)
        vb_ref[:, cs] = _rms(vb[:, cs], vn_ref[...])


def _ab_in_kernel(x_ref, sh_ref, sc_ref, nw_ref, w_ref, wa2_ref, ba_ref, vn_ref, *out_refs):
    tm = x_ref.shape[0]
    h = (_rms(x_ref[...], nw_ref[...]) * (1.0 + _mod_rows(sc_ref, tm)) + _mod_rows(sh_ref, tm)).astype(BF16)
    _ab_project(h, w_ref, wa2_ref, ba_ref, vn_ref, *out_refs)


_AB_IN_WEIGHTS = ["ab_w_in", "gla_w_a2", "gla_b_a", "gmlp_norm"]


def _ab_in(x, mod, l, e, w, tm):
    g_n, t_n, _ = x.shape
    out = jax.ShapeDtypeStruct((g_n, t_n, HP), F32)
    return pl.pallas_call(
        _ab_in_kernel,
        grid=(g_n, t_n // tm),
        in_specs=[_tok_spec(tm, D_MODEL), _mod_spec(mod, l, 3, tm, t_n), _mod_spec(mod, l, 4, tm, t_n),
                  _layer_spec(w["mix_norm"], l), _layer_spec(w["ab_w_in"], e), _layer_spec(w["gla_w_a2"], e),
                  _layer_spec(w["gla_b_a"], e), _layer_spec(w["gmlp_norm"], e)],
        out_specs=[_tok_spec(tm, HP)] * 7,
        out_shape=[out] * 7,
        compiler_params=_params("parallel", "parallel"),
        name="ab_in",
    )(x, mod, mod, w["mix_norm"], w["ab_w_in"], w["gla_w_a2"], w["gla_b_a"], w["gmlp_norm"])


def _gla_head_chunk(q, k, v, la, st, tril, causal):
    n = q.shape[0]
    b = _dot01(tril, la)
    bm = b[n // 2 - 1:n // 2, :]
    a = jnp.where(causal, _dot_nt(q * jnp.exp(b - bm), k * jnp.exp(bm - b)), 0.0)
    o = _dot(a, v) + _dot_nt(q * jnp.exp(b), st)
    bl = b[n - 1:n, :]
    st_new = st * jnp.exp(bl) + _dot_tn(v, k * jnp.exp(bl - b))
    return o, st_new


def _gla_chunk_intra(q, k, v, la, tril, causal):
    n = q.shape[0]
    b = _dot01(tril, la)
    bm = b[n // 2 - 1:n // 2, :]
    bl = b[n - 1:n, :]
    qe = (q * jnp.exp(b - bm)).astype(BF16)
    ke = (k * jnp.exp(bm - b)).astype(BF16)
    vb = v.astype(BF16)
    o_intra = []
    for h in range(GLA_H):
        cs = slice(h * LANES, (h + 1) * LANES)
        a = jnp.where(causal, _dot_nt(qe[:, cs], ke[:, cs]), 0.0)
        o_intra.append(_dot(a, vb[:, cs]))
    return o_intra, (q * jnp.exp(b)).astype(BF16), (k * jnp.exp(bl - b)).astype(BF16), vb, jnp.exp(bl)


def _ab_mix_kernel(*refs, chunk, carry, aliased):
    if aliased:
        refs = refs[1:]
    if carry:
        (x_ref, gt_ref, q_ref, k_ref, v_ref, g_ref, la_ref, u_ref, vb_ref, gn_ref, ws_ref, bs_ref, wo_ref,
         xo_ref, so_ref, o_scr, st_scr) = refs
    else:
        (x_ref, gt_ref, q_ref, k_ref, v_ref, g_ref, la_ref, u_ref, vb_ref, gn_ref, ws_ref, bs_ref, wo_ref, s0_ref,
         xo_ref, so_ref, o_scr) = refs
    tm = q_ref.shape[0]
    causal, _ = _causal(chunk)
    tril = jnp.where(causal, 1.0, 0.0).astype(BF16)
    key_pad = jnp.zeros((LANES - GLA_DK, GLA_DV), F32)

    if carry:
        @pl.when(pl.program_id(1) == 0)
        def _():
            st_scr[...] = jnp.zeros_like(st_scr)

        n_chunks = tm // chunk
        intra = [_gla_chunk_intra(q_ref[c * chunk:(c + 1) * chunk, :], k_ref[c * chunk:(c + 1) * chunk, :],
                                  v_ref[c * chunk:(c + 1) * chunk, :], la_ref[c * chunk:(c + 1) * chunk, :],
                                  tril, causal) for c in range(n_chunks)]
        for h in range(GLA_H):
            cs = slice(h * LANES, (h + 1) * LANES)
            st = st_scr[h]
            for c in range(n_chunks):
                o_intra, qd, kd, vb, dec = intra[c]
                o_scr[c * chunk:(c + 1) * chunk, cs] = o_intra[h] + _dot_nt(qd[:, cs], st)
                st = st * dec[:, cs] + _dot_tn(vb[:, cs], kd[:, cs])
            st_scr[h] = st

        @pl.when(pl.program_id(1) == pl.num_programs(1) - 1)
        def _():
            for h in range(GLA_H):
                so_ref[h] = st_scr[h].T[:GLA_DK, :]
    else:
        def seq_body(c, _):
            rows = pl.ds(pl.multiple_of(c * chunk, chunk), chunk)
            for h in range(GLA_H):
                cs = slice(h * LANES, (h + 1) * LANES)
                st = jnp.concatenate([s0_ref[c, h], key_pad], axis=0).T
                o, st_new = _gla_head_chunk(q_ref[rows, cs], k_ref[rows, cs], v_ref[rows, cs], la_ref[rows, cs],
                                            st, tril, causal)
                o_scr[rows, cs] = o
                so_ref[c, h] = st_new.T[:GLA_DK, :]
            return 0

        lax.fori_loop(0, tm // chunk, seq_body, 0)

    mix = None
    for h in range(GLA_H):
        cs = slice(h * GLA_DV, (h + 1) * GLA_DV)
        on = _rms(o_scr[:, cs], gn_ref[...]) * _silu(g_ref[:, cs])
        p = _dot(on, wo_ref[cs, :])
        mix = p if mix is None else mix + p
    for gi in range(GMLP_G):
        cs = slice(gi * GMLP_DG, (gi + 1) * GMLP_DG)
        zs = []
        for r0 in range(0, tm, GMLP_CHUNK):
            zs.append(_dot(ws_ref[gi], vb_ref[r0:r0 + GMLP_CHUNK, cs]) + bs_ref[gi])
        z = zs[0] if len(zs) == 1 else jnp.concatenate(zs, axis=0)
        mix = mix + _dot(u_ref[:, cs] * z, wo_ref[GLA_H * GLA_DV + gi * GMLP_DG:GLA_H * GLA_DV + (gi + 1) * GMLP_DG, :])
    xo_ref[...] = x_ref[...] + _mod_rows(gt_ref, tm) * mix


def _ab_mix(x, mod, l, e, w, proj, s0, tm, chunk, ws, bs, s_stack):
    g_n, t_n, _ = x.shape
    carry = s0 is None
    in_specs = ([_tok_spec(tm, D_MODEL), _mod_spec(mod, l, 5, tm, t_n)] + [_tok_spec(tm, HP)] * 7
                + [_layer_spec(w["gla_norm"], e), _layer_spec(ws, e), _layer_spec(bs, e), _layer_spec(w["ab_w_out"], e)])
    args = [x, mod, *proj, w["gla_norm"], ws, bs, w["ab_w_out"]]
    scratch = [pltpu.VMEM((tm, HP), F32)]
    if carry:
        s_shape = (DEPTH - DEPTH // 2, g_n, GLA_H, GLA_DK, GLA_DV)
        s_spec = pl.BlockSpec((None, None, GLA_H, GLA_DK, GLA_DV), lambda g, i: (e, g, 0, 0, 0))
        scratch.append(pltpu.VMEM((GLA_H, LANES, LANES), F32))
    else:
        s_shape = s0.shape
        s_spec = pl.BlockSpec((None, None) + s0.shape[2:], lambda g, i: (e, g, 0, 0, 0, 0))
        in_specs.append(s_spec)
        args.append(s0)
    aliases = {}
    if s_stack is not None:
        aliases = {0: 1}
        in_specs.insert(0, pl.BlockSpec(memory_space=pl.ANY))
        args.insert(0, s_stack)
    return pl.pallas_call(
        functools.partial(_ab_mix_kernel, chunk=chunk, carry=carry, aliased=s_stack is not None),
        grid=(g_n, t_n // tm),
        in_specs=in_specs,
        out_specs=[_tok_spec(tm, D_MODEL), s_spec],
        out_shape=[jax.ShapeDtypeStruct(x.shape, F32), jax.ShapeDtypeStruct(s_shape, F32)],
        scratch_shapes=scratch,
        input_output_aliases=aliases,
        compiler_params=_params("parallel", "arbitrary"),
        name="ab_mix",
    )(*args)


def _ml_in_kernel(x_ref, sh_ref, sc_ref, nw_ref, w_ref, xm_ref, z_ref):
    tm = x_ref.shape[0]
    h = (_rms(x_ref[...], nw_ref[...]) * (1.0 + _mod_rows(sc_ref, tm)) + _mod_rows(sh_ref, tm)).astype(BF16)
    xm_ref[...] = jnp.dot(h, w_ref[:, :ML_INNER], preferred_element_type=F32)
    z_ref[...] = jnp.dot(h, w_ref[:, ML_INNER:], preferred_element_type=F32)


def _ml_in(x, mod, l, o, w, tm):
    g_n, t_n, _ = x.shape
    out = jax.ShapeDtypeStruct((g_n, t_n, ML_INNER), F32)
    return pl.pallas_call(
        _ml_in_kernel,
        grid=(g_n, t_n // tm),
        in_specs=[_tok_spec(tm, D_MODEL), _mod_spec(mod, l, 3, tm, t_n), _mod_spec(mod, l, 4, tm, t_n),
                  _layer_spec(w["mix_norm"], l), _layer_spec(w["ml_w_in"], o)],
        out_specs=[_tok_spec(tm, ML_INNER)] * 2,
        out_shape=[out] * 2,
        compiler_params=_params("parallel", "parallel"),
        name="ml_in",
    )(x, mod, mod, w["mix_norm"], w["ml_w_in"])


HALO = SUBLANES
CONV_FIRST = HALO - (ML_CONV - 1)


def _ml_qkv_gates(xm_ref, ext_scr, nb, cw_ref, cb_ref, wq_ref, wk_ref, wv_ref, wg_ref, bg_ref,
                  xc_ref, q_ref, k_ref, v_ref):
    tm = xm_ref.shape[0]
    ls = tm // nb
    gates = bg_ref[...]
    for blk in range(ML_INNER // MXU_DIM):
        cs = slice(blk * MXU_DIM, (blk + 1) * MXU_DIM)
        pre = cb_ref[:, cs]
        for j in range(ML_CONV):
            pre = pre + ext_scr[:, CONV_FIRST + j:CONV_FIRST + j + ls, cs].reshape(tm, MXU_DIM) * cw_ref[j:j + 1, cs]
        xc = _silu(pre)
        xc_ref[:, cs] = xc
        xcb = xc.astype(BF16)
        q = jnp.dot(xcb, wq_ref[blk], preferred_element_type=F32).astype(BF16)
        k = jnp.dot(xcb, wk_ref[blk], preferred_element_type=F32).astype(BF16)
        v = jnp.dot(xm_ref[:, cs].astype(BF16), wv_ref[blk], preferred_element_type=F32).astype(BF16)
        gates = (gates + jnp.dot(q, wg_ref[cs, :], preferred_element_type=F32)
                 + jnp.dot(k, wg_ref[ML_INNER + blk * MXU_DIM:ML_INNER + (blk + 1) * MXU_DIM, :],
                           preferred_element_type=F32)
                 + jnp.dot(v, wg_ref[2 * ML_INNER + blk * MXU_DIM:2 * ML_INNER + (blk + 1) * MXU_DIM, :],
                           preferred_element_type=F32))
        q_ref[:, cs] = q.astype(q_ref.dtype)
        k_ref[:, cs] = k.astype(k_ref.dtype)
        v_ref[:, cs] = v.astype(v_ref.dtype)
    lane = lax.broadcasted_iota(jnp.int32, gates.shape, 1)
    return jnp.where(lane < ML_H, gates, _log_sigmoid(gates))


def _mlstm_head(q, k, v, ig_col, f_col, c_prev, n_prev, m_prev, causal, eye):
    q, k, v = q.astype(BF16), k.astype(BF16), v.astype(BF16)
    n_rows = q.shape[0]
    scale = ML_DH ** -0.5
    ig_row = _col_to_row(ig_col, eye)
    f_row = _col_to_row(f_col, eye)
    log_d = jnp.where(causal, f_col - f_row + ig_row, -jnp.inf)
    log_inter = f_col + m_prev
    m_t = jnp.maximum(log_inter, jnp.max(log_d, axis=-1, keepdims=True))
    d = jnp.exp(log_d - m_t) * scale
    w_inter = jnp.exp(log_inter - m_t)
    s = _dot_nt(q, k) * d
    qn = _dot_nt(q, jnp.broadcast_to(n_prev, (SUBLANES, ML_DH)))[:, 0:1]
    num = _dot(s, v) + w_inter * _dot_nt(q, c_prev)
    den = jnp.sum(s, axis=-1, keepdims=True) + w_inter * qn
    hh = num / jnp.maximum(jnp.abs(den), jnp.exp(-m_t))
    m_new = m_t[n_rows - 1:n_rows, :]
    f_last = f_col[n_rows - 1:n_rows, :]
    w_rows = jnp.exp(f_last - f_col + ig_col - m_new) * scale
    w_wide = jnp.broadcast_to(w_rows, (n_rows, ML_DH)).astype(BF16)
    decay = jnp.exp(f_last + m_prev - m_new)
    c_new = decay * c_prev + _dot_tn(v * w_wide, k)
    n_new = decay * n_prev + _dot_tn(w_wide[:, 0:LANES], k)[0:1, :]
    return hh, c_new, n_new, m_new


def _ml_out_gate(hh, xc, z, gn, sk):
    mu = jnp.mean(hh, axis=-1, keepdims=True)
    hc = hh - mu
    var = jnp.mean(hc * hc, axis=-1, keepdims=True)
    hn = hc * lax.rsqrt(var + EPS) * gn
    return ((hn + sk * xc) * _silu(z)).astype(BF16)


_ML_WEIGHTS = ["ml_conv_w", "ml_conv_b", "ml_wq", "ml_wk", "ml_wv", "ml_w_gates", "ml_b_gates"]
_ML_OUT_WEIGHTS = ["ml_norm", "ml_skip", "ml_w_out"]


def _ml_core_kernel(*refs, aliased):
    if aliased:
        refs = refs[1:]
    (x_ref, sh_ref, sc_ref, gt_ref, nw_ref, win_ref, cw_ref, cb_ref, wq_ref, wk_ref, wv_ref, wg_ref, bg_ref,
     gn_ref, sk_ref, wo_ref, xo_ref, co_ref, no_ref, mo_ref, cvo_ref,
     ext_scr, q_scr, k_scr, v_scr, xc_scr, xm_ref, z_ref) = refs
    tm = x_ref.shape[0]
    causal, eye = _causal(tm)
    tril = jnp.where(causal, 1.0, 0.0).astype(BF16)

    h = (_rms(x_ref[...], nw_ref[...]) * (1.0 + _mod_rows(sc_ref, tm)) + _mod_rows(sh_ref, tm)).astype(BF16)
    xm_ref[...] = jnp.dot(h, win_ref[:, :ML_INNER], preferred_element_type=F32)
    z_ref[...] = jnp.dot(h, win_ref[:, ML_INNER:], preferred_element_type=F32)

    @pl.when(pl.program_id(1) == 0)
    def _():
        ext_scr[:, 0:HALO, :] = jnp.zeros((1, HALO, ML_INNER), F32)
        co_ref[...] = jnp.zeros_like(co_ref)
        no_ref[...] = jnp.zeros_like(no_ref)
        mo_ref[...] = jnp.zeros_like(mo_ref)

    ext_scr[:, HALO:HALO + tm, :] = xm_ref[...].reshape(1, tm, ML_INNER)
    gl = _ml_qkv_gates(xm_ref, ext_scr, 1, cw_ref, cb_ref, wq_ref, wk_ref, wv_ref, wg_ref, bg_ref,
                       xc_scr, q_scr, k_scr, v_scr)
    cvo_ref[...] = ext_scr[:, CONV_FIRST + tm:HALO + tm, :].reshape(cvo_ref.shape)
    ext_scr[:, 0:HALO, :] = ext_scr[:, tm:tm + HALO, :]

    cum = _dot01(tril, gl)
    mix = None
    for h in range(ML_H):
        cs = slice(h * ML_DH, (h + 1) * ML_DH)
        hh, c_new, n_new, m_new = _mlstm_head(
            q_scr[:, cs], k_scr[:, cs], v_scr[:, cs], gl[:, h:h + 1], cum[:, ML_H + h:ML_H + h + 1],
            co_ref[h], no_ref[h:h + 1, :], mo_ref[0:1, h:h + 1], causal, eye)
        co_ref[h] = c_new
        no_ref[h:h + 1, :] = n_new
        mo_ref[0:1, h:h + 1] = m_new
        p = jnp.dot(_ml_out_gate(hh, xc_scr[:, cs], z_ref[:, cs], gn_ref[:, cs], sk_ref[:, cs]), wo_ref[cs, :],
                    preferred_element_type=F32)
        mix = p if mix is None else mix + p
    xo_ref[...] = x_ref[...] + _mod_rows(gt_ref, tm) * mix


def _ml_core(x, mod, l, o, w, tm, c_stack):
    g_n, t_n, _ = x.shape
    c_spec = pl.BlockSpec((None, None, ML_H, ML_DH, ML_DH), lambda g, i: (o, g, 0, 0, 0))
    names = ["ml_w_in"] + _ML_WEIGHTS + _ML_OUT_WEIGHTS
    in_specs = ([_tok_spec(tm, D_MODEL)] + [_mod_spec(mod, l, j, tm, t_n) for j in (3, 4, 5)]
                + [_layer_spec(w["mix_norm"], l)] + [_layer_spec(w[n], o) for n in names])
    args = [x, mod, mod, mod, w["mix_norm"]] + [w[n] for n in names]
    aliases = {}
    if c_stack is not None:
        aliases = {0: 1}
        in_specs.insert(0, pl.BlockSpec(memory_space=pl.ANY))
        args.insert(0, c_stack)
    return pl.pallas_call(
        functools.partial(_ml_core_kernel, aliased=c_stack is not None),
        grid=(g_n, t_n // tm),
        in_specs=in_specs,
        out_specs=[_tok_spec(tm, D_MODEL), c_spec,
                   pl.BlockSpec((None, ML_H, ML_DH), lambda g, i: (g, 0, 0)),
                   pl.BlockSpec((None, 1, ML_H), lambda g, i: (g, 0, 0)),
                   pl.BlockSpec((None, ML_CONV - 1, ML_INNER), lambda g, i: (g, 0, 0))],
        out_shape=[jax.ShapeDtypeStruct(x.shape, F32),
                   jax.ShapeDtypeStruct((DEPTH // 2, g_n, ML_H, ML_DH, ML_DH), F32),
                   jax.ShapeDtypeStruct((g_n, ML_H, ML_DH), F32),
                   jax.ShapeDtypeStruct((g_n, 1, ML_H), F32),
                   jax.ShapeDtypeStruct((g_n, ML_CONV - 1, ML_INNER), F32)],
        scratch_shapes=[pltpu.VMEM((1, HALO + tm, ML_INNER), F32)] + [pltpu.VMEM((tm, ML_INNER), BF16)] * 3
        + [pltpu.VMEM((tm, ML_INNER), F32)] * 3,
        input_output_aliases=aliases,
        compiler_params=_params("parallel", "arbitrary"),
        name="ml_core",
    )(*args)


def _ml_pre_kernel(xm_ref, c0_ref, cw_ref, cb_ref, wq_ref, wk_ref, wv_ref, wg_ref, bg_ref,
                   xc_ref, q_ref, k_ref, v_ref, gl_ref, co_ref, ext_scr, *, nb):
    tm = xm_ref.shape[0]
    ls = tm // nb
    ext_scr[:, CONV_FIRST:HALO, :] = c0_ref[...]
    ext_scr[:, HALO:HALO + ls, :] = xm_ref[...].reshape(nb, ls, ML_INNER)
    gl_ref[...] = _ml_qkv_gates(xm_ref, ext_scr, nb, cw_ref, cb_ref, wq_ref, wk_ref, wv_ref, wg_ref, bg_ref,
                                xc_ref, q_ref, k_ref, v_ref)
    co_ref[...] = ext_scr[:, CONV_FIRST + ls:HALO + ls, :]


def _ml_pre(xm, c0, o, w, nb):
    g_n, tm, _ = xm.shape
    c_spec = pl.BlockSpec((None,) + c0.shape[1:], lambda g, i: (g, 0, 0, 0))
    big = jax.ShapeDtypeStruct((g_n, tm, ML_INNER), F32)
    return pl.pallas_call(
        functools.partial(_ml_pre_kernel, nb=nb),
        grid=(g_n, 1),
        in_specs=[_tok_spec(tm, ML_INNER), c_spec] + [_layer_spec(w[n], o) for n in _ML_WEIGHTS],
        out_specs=[_tok_spec(tm, ML_INNER)] * 4 + [_tok_spec(tm, LANES), c_spec],
        out_shape=[big] * 4 + [jax.ShapeDtypeStruct((g_n, tm, LANES), F32), jax.ShapeDtypeStruct(c0.shape, F32)],
        scratch_shapes=[pltpu.VMEM((nb, HALO + tm // nb, ML_INNER), F32)],
        compiler_params=_params("parallel", "arbitrary"),
        name="ml_pre",
    )(xm, c0, *[w[n] for n in _ML_WEIGHTS])


def _ml_scan_kernel(*refs, aliased):
    if aliased:
        refs = refs[1:]
    (q_ref, k_ref, v_ref, gl_ref, c0_ref, n0_ref, m0_ref, hh_ref, co_ref, no_ref, mo_ref) = refs
    n_seq = c0_ref.shape[0]
    n_rows = q_ref.shape[0] // n_seq
    causal, eye = _causal(n_rows)
    tril = jnp.where(causal, 1.0, 0.0).astype(BF16)
    for s in range(n_seq):
        rows = slice(s * n_rows, (s + 1) * n_rows)
        gl = gl_ref[rows, :]
        cum = _dot01(tril, gl)
        for h in range(ML_H):
            cs = slice(h * ML_DH, (h + 1) * ML_DH)
            hh, c_new, n_new, m_new = _mlstm_head(
                q_ref[rows, cs], k_ref[rows, cs], v_ref[rows, cs], gl[:, h:h + 1], cum[:, ML_H + h:ML_H + h + 1],
                c0_ref[s, h], n0_ref[s, h:h + 1, :], m0_ref[s, 0:1, h:h + 1], causal, eye)
            hh_ref[rows, cs] = hh
            co_ref[s, h] = c_new
            no_ref[s, h:h + 1, :] = n_new
            mo_ref[s, 0:1, h:h + 1] = m_new


def _ml_scan(q, k, v, gl, state, o, c_stack, n_seq):
    g_n, tm, _ = q.shape
    b_n = g_n * n_seq
    c_spec = pl.BlockSpec((None, n_seq, ML_H, ML_DH, ML_DH), lambda g, i: (o, g, 0, 0, 0))
    n_spec = pl.BlockSpec((None, n_seq, ML_H, ML_DH), lambda g, i: (o, g, 0, 0))
    m_spec = pl.BlockSpec((None, n_seq, 1, ML_H), lambda g, i: (o, g, 0, 0))
    in_specs = [_tok_spec(tm, ML_INNER)] * 3 + [_tok_spec(tm, LANES), c_spec, n_spec, m_spec]
    args = [q, k, v, gl, *state]
    aliases = {}
    if c_stack is not None:
        aliases = {0: 1}
        in_specs.insert(0, pl.BlockSpec(memory_space=pl.ANY))
        args.insert(0, c_stack)
    return pl.pallas_call(
        functools.partial(_ml_scan_kernel, aliased=c_stack is not None),
        grid=(g_n, 1),
        in_specs=in_specs,
        out_specs=[_tok_spec(tm, ML_INNER), c_spec,
                   pl.BlockSpec((n_seq, ML_H, ML_DH), lambda g, i: (g, 0, 0)),
                   pl.BlockSpec((n_seq, 1, ML_H), lambda g, i: (g, 0, 0))],
        out_shape=[jax.ShapeDtypeStruct((g_n, tm, ML_INNER), F32),
                   jax.ShapeDtypeStruct((DEPTH // 2, b_n, ML_H, ML_DH, ML_DH), F32),
                   jax.ShapeDtypeStruct((b_n, ML_H, ML_DH), F32),
                   jax.ShapeDtypeStruct((b_n, 1, ML_H), F32)],
        input_output_aliases=aliases,
        compiler_params=_params("parallel", "arbitrary"),
        name="ml_scan",
    )(*args)


def _ml_post_kernel(x_ref, gt_ref, hh_ref, xc_ref, z_ref, gn_ref, sk_ref, wo_ref, xo_ref):
    tm = x_ref.shape[0]
    outs = []
    for h in range(ML_H):
        cs = slice(h * ML_DH, (h + 1) * ML_DH)
        outs.append(_ml_out_gate(hh_ref[:, cs], xc_ref[:, cs], z_ref[:, cs], gn_ref[:, cs], sk_ref[:, cs]))
    mix = jnp.dot(jnp.concatenate(outs, axis=1), wo_ref[...], preferred_element_type=F32)
    xo_ref[...] = x_ref[...] + _mod_rows(gt_ref, tm) * mix


def _ml_post(x, mod, l, o, w, hh, xc, z, tm):
    g_n, t_n, _ = x.shape
    return pl.pallas_call(
        _ml_post_kernel,
        grid=(g_n, t_n // tm),
        in_specs=[_tok_spec(tm, D_MODEL), _mod_spec(mod, l, 5, tm, t_n)] + [_tok_spec(tm, ML_INNER)] * 3
        + [_layer_spec(w[n], o) for n in _ML_OUT_WEIGHTS],
        out_specs=_tok_spec(tm, D_MODEL),
        out_shape=jax.ShapeDtypeStruct(x.shape, F32),
        compiler_params=_params("parallel", "parallel"),
        name="ml_post",
    )(x, mod, hh, xc, z, *[w[n] for n in _ML_OUT_WEIGHTS])


def _pad_heads(a):
    lead = a.shape[:-1]
    a = a.reshape(lead + (GLA_H, GLA_DK))
    a = jnp.pad(a, [(0, 0)] * len(lead) + [(0, 0), (0, LANES - GLA_DK)])
    return a.reshape(lead + (HP,))


def _block_diag_dense(wb):
    rows = wb.reshape(wb.shape[0], -1, MXU_DIM, ML_BLOCK)
    dense = jnp.tile(rows, (1, 1, 1, MXU_DIM // ML_BLOCK))
    r_blk = lax.broadcasted_iota(jnp.int32, (MXU_DIM, MXU_DIM), 0) // ML_BLOCK
    c_blk = lax.broadcasted_iota(jnp.int32, (MXU_DIM, MXU_DIM), 1) // ML_BLOCK
    return jnp.where(r_blk == c_blk, dense, 0.0).astype(BF16)


def _prepare_weights(p):
    w = {}
    w["ffn_norm"] = p["ffn_norm"].reshape(DEPTH, 2, 1, D_MODEL)
    for name in ("ffn_w_gate", "ffn_w_up", "ffn_w_down"):
        w[name] = p[name]
    w["mix_norm"] = p["mix_norm"].reshape(DEPTH, 1, D_MODEL)
    w_in = p["ab_w_in"]
    qk = GLA_H * GLA_DK
    vw = GLA_H * GLA_DV
    o_k, o_v, o_g, o_a = qk, 2 * qk, 2 * qk + vw, 2 * qk + 2 * vw
    o_u = o_a + GLA_LOWRANK
    o_vb = o_u + GMLP_W
    w_a = jnp.pad(w_in[:, :, o_a:o_u], ((0, 0), (0, 0), (0, LANES - GLA_LOWRANK)))
    w["ab_w_in"] = jnp.concatenate(
        [_pad_heads(w_in[:, :, :o_k]), _pad_heads(w_in[:, :, o_k:o_v]), w_in[:, :, o_v:o_g], w_in[:, :, o_g:o_a],
         w_a, w_in[:, :, o_u:o_vb], w_in[:, :, o_vb:]], axis=-1).astype(BF16)
    w["gla_w_a2"] = jnp.pad(_pad_heads(p["gla_w_a2"]), ((0, 0), (0, LANES - GLA_LOWRANK), (0, 0))).astype(BF16)
    w["gla_b_a"] = _pad_heads(p["gla_b_a"])[:, None, :]
    w["gla_norm"] = p["gla_norm"][:, None, :]
    w["gmlp_norm"] = p["gmlp_norm"][:, None, :]
    w["ab_w_out"] = p["ab_w_out"].astype(BF16)
    w["ml_w_in"] = p["ml_w_in"].astype(BF16)
    w["ml_conv_w"] = p["ml_conv_w"]
    w["ml_conv_b"] = p["ml_conv_b"][:, None, :]
    w["ml_wq"] = _block_diag_dense(p["ml_wq"])
    w["ml_wk"] = _block_diag_dense(p["ml_wk"])
    w["ml_wv"] = _block_diag_dense(p["ml_wv"])
    w["ml_w_gates"] = jnp.pad(p["ml_w_gates"], ((0, 0), (0, 0), (0, LANES - 2 * ML_H))).astype(BF16)
    w["ml_b_gates"] = jnp.pad(p["ml_b_gates"], ((0, 0), (0, LANES - 2 * ML_H)))[:, None, :]
    w["ml_norm"] = p["ml_norm"][:, None, :]
    w["ml_skip"] = p["ml_skip"][:, None, :]
    w["ml_w_out"] = p["ml_w_out"].astype(BF16)
    w["final_norm"] = p["final_norm"][None, :]
    return w


def _spatial_weights(ws, bs, seq_len):
    if seq_len % GMLP_CHUNK == 0:
        length = GMLP_CHUNK
    else:
        length = seq_len
    reps = GMLP_CHUNK // length
    wt = jnp.tril(ws[:, :, :length, :length])
    eye = jnp.eye(reps, dtype=ws.dtype)
    wt = jnp.einsum("egts,ab->egatbs", wt, eye).reshape(ws.shape[0], GMLP_G, GMLP_CHUNK, GMLP_CHUNK)
    bt = jnp.tile(bs[:, :, :length], (1, 1, reps))
    bt = jnp.broadcast_to(bt[:, :, :, None], (ws.shape[0], GMLP_G, GMLP_CHUNK, GMLP_DG))
    return wt.astype(BF16), bt


def _ffn_step(x, mod, l, s, w, ffn_bf16, fmod):
    g_n, t_n, _ = x.shape
    tm = min(FFN_ROW_TILE, t_n)
    names = ("ffn_w_gate", "ffn_w_up", "ffn_w_down")
    if (l, s) not in ffn_bf16:
        ffn_bf16[(l, s)] = tuple(w[n][l, s].astype(BF16) for n in names)
    nxt = (l, 1) if s == 0 else (l + 1, 0)
    steps = g_n * (t_n // tm)
    slab_ok = all(w[n].shape[2] % (steps * 2 * SUBLANES) == 0 for n in names)
    cast = nxt if (nxt[0] < DEPTH and nxt not in ffn_bf16 and slab_ok) else None
    y, converted = _ffn(x, mod, l, s, w, ffn_bf16[(l, s)], tm, fmod, cast)
    if cast is not None:
        ffn_bf16[nxt] = converted
    return y


def _trunk(x, mod, fmod, w, ws, bs, states, ffn_bf16, *, tm, mix_tm, gla_chunk, scan_tm, seq_rows):
    fresh = states is None
    g_n, t_n, _ = x.shape
    tm = min(tm, t_n)
    v_out, n_out, m_out, conv_out = [], [], [], []
    s_stack = c_stack = None
    for l in range(DEPTH):
        x = _ffn_step(x, mod, l, 0, w, ffn_bf16, None)
        if l % 2 == 0:
            e = l // 2
            proj = _ab_in(x, mod, l, e, w, tm)
            if fresh:
                x, s_stack = _ab_mix(x, mod, l, e, w, proj, None, mix_tm, gla_chunk, ws, bs, s_stack)
            else:
                v_out.append(proj[6])
                nseq = mix_tm // seq_rows
                view = lambda a: a.reshape(-1, mix_tm, a.shape[-1])
                s_all = states["gla_S"]
                s0 = s_all.reshape(s_all.shape[0], -1, nseq, GLA_H, GLA_DK, GLA_DV)
                mod_v = mod.reshape(mod.shape[0], mod.shape[1], -1, nseq, D_MODEL)
                xv, s_stack = _ab_mix(view(x), mod_v, l, e, w, [view(a) for a in proj], s0, mix_tm, gla_chunk, ws, bs,
                                      s_stack)
                x = xv.reshape(g_n, t_n, D_MODEL)
        else:
            o = l // 2
            if fresh:
                x, c_stack, n_new, m_new, conv_new = _ml_core(x, mod, l, o, w, scan_tm, c_stack)
                m_new = m_new.reshape(g_n, ML_H)
            else:
                xm, z = _ml_in(x, mod, l, o, w, tm)
                nseq = LANES // seq_rows
                rows = nseq * seq_rows
                view = lambda a, r: a.reshape(-1, r, a.shape[-1])
                c0 = states["ml_conv"][o].reshape(-1, nseq, ML_CONV - 1, ML_INNER)
                xc, q, k, v, gl, conv_new = _ml_pre(view(xm, rows), c0, o, w, nseq)
                conv_new = conv_new.reshape(-1, ML_CONV - 1, ML_INNER)
                st = (states["ml_C"], states["ml_n"], states["ml_m"][:, :, None, :])
                srows = SCAN_SEQS * seq_rows
                hh, c_stack, n_new, m_new = _ml_scan(view(q, srows), view(k, srows), view(v, srows),
                                                     view(gl, srows), st, o, c_stack, SCAN_SEQS)
                m_new = m_new.reshape(-1, ML_H)
                xc = xc.reshape(g_n, t_n, ML_INNER)
                hh = hh.reshape(g_n, t_n, ML_INNER)
                x = _ml_post(x, mod, l, o, w, hh, xc, z, min(tm, 2 * ML_CHUNK))
            n_out.append(n_new)
            m_out.append(m_new)
            conv_out.append(conv_new)
        x = _ffn_step(x, mod, l, 1, w, ffn_bf16, fmod if l == DEPTH - 1 else None)
    y = x
    s_stack = s_stack.reshape(s_stack.shape[0], -1, GLA_H, GLA_DK, GLA_DV)
    v_stack = jnp.stack(v_out) if v_out else None
    return (y, s_stack, v_stack, c_stack, jnp.stack(n_out), jnp.stack(m_out),
            jnp.stack(conv_out))


def kernel(x_prompt, x_sample, c_prompt, c_sample, state_gla_S, state_mlstm_C, state_mlstm_n, state_mlstm_m, state_mlstm_conv, ada_w, ada_b, ffn_norm, ffn_w_gate, ffn_w_up, ffn_w_down, mix_norm, ab_w_in, gla_w_a2, gla_b_a, gla_norm, gmlp_norm, gmlp_ws, gmlp_bs, ab_w_out, ml_w_in, ml_conv_w, ml_conv_b, ml_wq, ml_wk, ml_wv, ml_w_gates, ml_b_gates, ml_norm, ml_skip, ml_w_out, final_norm, final_ada_w, final_ada_b):
    p = dict(ffn_norm=ffn_norm, ffn_w_gate=ffn_w_gate, ffn_w_up=ffn_w_up, ffn_w_down=ffn_w_down, mix_norm=mix_norm,
             ab_w_in=ab_w_in, gla_w_a2=gla_w_a2, gla_b_a=gla_b_a, gla_norm=gla_norm, gmlp_norm=gmlp_norm,
             ab_w_out=ab_w_out, ml_w_in=ml_w_in, ml_conv_w=ml_conv_w, ml_conv_b=ml_conv_b, ml_wq=ml_wq, ml_wk=ml_wk,
             ml_wv=ml_wv, ml_w_gates=ml_w_gates, ml_b_gates=ml_b_gates, ml_norm=ml_norm, ml_skip=ml_skip,
             ml_w_out=ml_w_out, final_norm=final_norm)
    w = _prepare_weights(p)
    n_p, t_p, _ = x_prompt.shape
    n_s, t_s, _ = x_sample.shape

    c_all = jnp.concatenate([c_prompt, c_sample], axis=0)
    mod = _ada(c_all, ada_w, ada_b)
    fmod = _ada(c_all, final_ada_w[None], final_ada_b[None])

    def split_mod(m):
        return m[:, :, :n_p, None, :], m[:, :, None, n_p:, :]

    mod_p, mod_s = split_mod(mod)
    fmod_p, fmod_s = split_mod(fmod)

    ws_p, bs_p = _spatial_weights(gmlp_ws, gmlp_bs, t_p)
    ws_s, bs_s = _spatial_weights(gmlp_ws, gmlp_bs, t_s)

    ffn_bf16 = {}
    y_p, s_p, _, c_p, n_p_, m_p, cv_p = _trunk(
        x_prompt, mod_p, fmod_p, w, ws_p, bs_p, None, ffn_bf16, tm=ROW_TILE, mix_tm=ROW_TILE,
        gla_chunk=_chunk_len(t_p, GLA_TILE_CHUNK, GLA_CHUNK), scan_tm=_chunk_len(t_p, ML_TILE_CHUNK, ML_CHUNK),
        seq_rows=t_p)

    states = dict(gla_S=state_gla_S, ml_C=state_mlstm_C, ml_n=state_mlstm_n, ml_m=state_mlstm_m,
                  ml_conv=state_mlstm_conv)
    xs = x_sample.reshape(1, n_s * t_s, D_MODEL)
    y_s, s_s, v_s, c_s, n_s_, m_s, cv_s = _trunk(
        xs, mod_s, fmod_s, w, ws_s, bs_s, states, ffn_bf16, tm=ROW_TILE, mix_tm=GMLP_CHUNK, gla_chunk=t_s, scan_tm=t_s,
        seq_rows=t_s)
    y_s = y_s.reshape(n_s, t_s, D_MODEL)
    v_s = v_s.reshape(-1, n_s, t_s, GMLP_W)
    return (y_p, y_s, s_p, s_s, v_s, c_p, c_s, n_p_, n_s_, m_p, m_s, cv_p, cv_s)
```

```python
import functools

import jax
import jax.numpy as jnp
from jax import lax
from jax.experimental import pallas as pl
from jax.experimental.pallas import tpu as pltpu

F32 = jnp.float32
BF16 = jnp.bfloat16
EPS = 1e-6

D_MODEL = 1024
DEPTH = 4
D_FF = 2816
GLA_H = 4
GLA_DK = 64
GLA_DV = 128
GLA_LOWRANK = 16
GLA_TAU = 16.0
GLA_CHUNK = 64
GMLP_G = 4
GMLP_DG = 128
GMLP_W = GMLP_G * GMLP_DG
GMLP_CHUNK = 128
ML_INNER = 2 * D_MODEL
ML_H = 4
ML_DH = ML_INNER // ML_H
ML_CONV = 4
ML_BLOCK = 4
ML_CHUNK = 128

LANES = 128
SUBLANES = 8
MXU_DIM = 256
VMEM_LIMIT_BYTES = 56 * 1024 * 1024

ROW_TILE = 512
GLA_TILE_CHUNK = 128
ML_TILE_CHUNK = 256
FFN_ROW_TILE = 1024
SCAN_SEQS = 2


def _chunk_len(t, preferred, nominal):
    for c in (preferred, nominal):
        if t % c == 0:
            return c
    return t


HP = GLA_H * LANES
AB_Q, AB_K, AB_V, AB_G = 0, HP, 2 * HP, 3 * HP
AB_A = 4 * HP
AB_U = AB_A + LANES
AB_VB = AB_U + GMLP_W
AB_COLS = AB_VB + GMLP_W


def _dot(a, b):
    return jnp.dot(a.astype(BF16), b.astype(BF16), preferred_element_type=F32)


def _dot_nt(a, b):
    return lax.dot_general(a.astype(BF16), b.astype(BF16), (((1,), (1,)), ((), ())), preferred_element_type=F32)


def _dot_tn(a, b):
    return lax.dot_general(a.astype(BF16), b.astype(BF16), (((0,), (0,)), ((), ())), preferred_element_type=F32)


def _dot01(sel, x):
    hi = x.astype(BF16)
    r1 = x - hi.astype(F32)
    mid = r1.astype(BF16)
    lo = (r1 - mid.astype(F32)).astype(BF16)
    return (jnp.dot(sel, hi, preferred_element_type=F32) + jnp.dot(sel, mid, preferred_element_type=F32)
            + jnp.dot(sel, lo, preferred_element_type=F32))


def _rms(x, g):
    return x * lax.rsqrt(jnp.mean(x * x, axis=-1, keepdims=True) + EPS) * g


def _silu(x):
    return x * jax.nn.sigmoid(x)


def _log_sigmoid(x):
    return jnp.minimum(x, 0.0) - jnp.log1p(jnp.exp(-jnp.abs(x)))


def _causal(n):
    row = lax.broadcasted_iota(jnp.int32, (n, n), 0)
    col = lax.broadcasted_iota(jnp.int32, (n, n), 1)
    return row >= col, row == col


def _mod_rows(ref, tm):
    m = ref[...]
    r = m.shape[0]
    if r == 1 or r == tm:
        return m
    rep = tm // r
    if rep == SUBLANES:
        return jnp.broadcast_to(m[:, None, :], (r, rep, m.shape[1])).reshape(tm, m.shape[1])
    row = lax.broadcasted_iota(jnp.int32, (tm, r), 0)
    lo = lax.broadcasted_iota(jnp.int32, (tm, r), 1) * rep
    sel = jnp.where(row >= lo, jnp.where(row < lo + rep, 1.0, 0.0), 0.0).astype(BF16)
    return _dot01(sel, m)


def _col_to_row(col, eye):
    return jnp.sum(jnp.where(eye, col, 0.0), axis=0, keepdims=True)


def _tok_spec(tm, w):
    return pl.BlockSpec((None, tm, w), lambda g, i: (g, i, 0))


def _mod_spec(mod, l, j, tm, t_n):
    t_mod = mod.shape[3]
    if t_mod == 1:
        return pl.BlockSpec((None, None, None, 1, D_MODEL), lambda g, i: (l, j, g, 0, 0))
    return pl.BlockSpec((None, None, None, t_mod * tm // t_n, D_MODEL), lambda g, i: (l, j, g, i, 0))


def _layer_spec(a, *lead):
    n = len(lead)
    shape = (None,) * n + tuple(a.shape[n:])
    zeros = (0,) * (a.ndim - n)
    return pl.BlockSpec(shape, lambda g, i: tuple(lead) + zeros, pipeline_mode=pl.Buffered(1))


def _params(*sem):
    return pltpu.CompilerParams(dimension_semantics=sem, vmem_limit_bytes=VMEM_LIMIT_BYTES)


def _ada_kernel(c_ref, w_ref, b_ref, o_ref):
    cs = _silu(c_ref[...])
    o_ref[...] = _dot(cs, w_ref[...]) + b_ref[...]


def _ada(c, w, b):
    n_l, _, width = w.shape
    n_j = width // D_MODEL
    r = c.shape[0]
    return pl.pallas_call(
        _ada_kernel,
        grid=(n_l, n_j),
        in_specs=[pl.BlockSpec((r, D_MODEL), lambda l, j: (0, 0)),
                  pl.BlockSpec((None, D_MODEL, D_MODEL), lambda l, j: (l, 0, j)),
                  pl.BlockSpec((None, None, 1, D_MODEL), lambda l, j: (l, j, 0, 0))],
        out_specs=pl.BlockSpec((None, None, r, D_MODEL), lambda l, j: (l, j, 0, 0)),
        out_shape=jax.ShapeDtypeStruct((n_l, n_j, r, D_MODEL), F32),
        compiler_params=_params("arbitrary", "arbitrary"),
        name="ada",
    )(c, w, b.reshape(n_l, n_j, 1, D_MODEL))


FFN_CHUNK = 1 * MXU_DIM


def _ffn_kernel(*refs, final, cast):
    x_ref, sh_ref, sc_ref, gt_ref, nw_ref, wg_ref, wu_ref, wd_ref = refs[:8]
    n_in = 8 + (3 if final else 0) + (3 if cast else 0)
    o_ref = refs[n_in]
    x = x_ref[...]
    tm = x.shape[0]
    h = (_rms(x, nw_ref[...]) * (1.0 + _mod_rows(sc_ref, tm)) + _mod_rows(sh_ref, tm)).astype(BF16)
    acc = None
    for f0 in range(0, D_FF, FFN_CHUNK):
        f1 = min(f0 + FFN_CHUNK, D_FF)
        g = jnp.dot(h, wg_ref[:, f0:f1], preferred_element_type=F32)
        u = jnp.dot(h, wu_ref[:, f0:f1], preferred_element_type=F32)
        p = _dot(_silu(g) * u, wd_ref[f0:f1, :])
        acc = p if acc is None else acc + p
    y = x + (0.5 * _mod_rows(gt_ref, tm)) * acc
    if final:
        fsh_ref, fsc_ref, fnw_ref = refs[8:11]
        y = _rms(y, fnw_ref[...]) * (1.0 + _mod_rows(fsc_ref, tm)) + _mod_rows(fsh_ref, tm)
    o_ref[...] = y
    if cast:
        for src, dst in zip(refs[n_in - 3:n_in], refs[n_in + 1:n_in + 4]):
            dst[...] = src[...].astype(BF16)


def _ffn(x, mod, l, s, w, ffn_w, tm, fmod=None, cast=None):
    g_n, t_n, _ = x.shape
    n_i = t_n // tm
    j0 = 6 * s
    whole = lambda a: pl.BlockSpec(a.shape, lambda g, i: (0, 0), pipeline_mode=pl.Buffered(1))
    in_specs = [_tok_spec(tm, D_MODEL), _mod_spec(mod, l, j0, tm, t_n), _mod_spec(mod, l, j0 + 1, tm, t_n),
                _mod_spec(mod, l, j0 + 2, tm, t_n), _layer_spec(w["ffn_norm"], l, s)] + [whole(a) for a in ffn_w]
    args = [x, mod, mod, mod, w["ffn_norm"], *ffn_w]
    out_specs = [_tok_spec(tm, D_MODEL)]
    out_shape = [jax.ShapeDtypeStruct(x.shape, F32)]
    if fmod is not None:
        in_specs += [_mod_spec(fmod, 0, 0, tm, t_n), _mod_spec(fmod, 0, 1, tm, t_n),
                     pl.BlockSpec((1, D_MODEL), lambda g, i: (0, 0))]
        args += [fmod, fmod, w["final_norm"]]
    if cast is not None:
        l2, s2 = cast
        steps = g_n * n_i
        for name in ("ffn_w_gate", "ffn_w_up", "ffn_w_down"):
            src = w[name]
            rows, cols = src.shape[2:]
            in_specs.append(pl.BlockSpec((None, None, rows // steps, cols), lambda g, i: (l2, s2, g * n_i + i, 0)))
            args.append(src)
            out_specs.append(pl.BlockSpec((rows // steps, cols), lambda g, i: (g * n_i + i, 0)))
            out_shape.append(jax.ShapeDtypeStruct((rows, cols), BF16))
    out = pl.pallas_call(
        functools.partial(_ffn_kernel, final=fmod is not None, cast=cast is not None),
        grid=(g_n, n_i),
        in_specs=in_specs,
        out_specs=out_specs,
        out_shape=out_shape,
        compiler_params=_params("parallel", "parallel"),
        name="ffn",
    )(*args)
    return out[0], tuple(out[1:])


def _ab_project(h, w_ref, wa2_ref, ba_ref, vn_ref, q_ref, k_ref, v_ref, g_ref, la_ref, u_ref, vb_ref):
    def seg(c0, width):
        return jnp.dot(h, w_ref[:, c0:c0 + width], preferred_element_type=F32)

    q_ref[...] = seg(AB_Q, HP) * GLA_DK ** -0.5
    k_ref[...] = seg(AB_K, HP)
    v_ref[...] = seg(AB_V, HP)
    g_ref[...] = seg(AB_G, HP)
    xa = _dot(seg(AB_A, LANES), wa2_ref[...]) + ba_ref[...]
    la_ref[...] = _log_sigmoid(xa) * (1.0 / GLA_TAU)
    u_ref[...] = seg(AB_U, GMLP_W)
    vb = seg(AB_VB, GMLP_W)
    for gi in range(GMLP_G):
        cs = slice(gi * GMLP_DG, (gi + 1) * GMLP_DG)
        vb_ref[:, cs] = _rms(vb[:, cs], vn_ref[...])


def _ab_in_kernel(x_ref, sh_ref, sc_ref, nw_ref, w_ref, wa2_ref, ba_ref, vn_ref, *out_refs):
    tm = x_ref.shape[0]
    h = (_rms(x_ref[...], nw_ref[...]) * (1.0 + _mod_rows(sc_ref, tm)) + _mod_rows(sh_ref, tm)).astype(BF16)
    _ab_project(h, w_ref, wa2_ref, ba_ref, vn_ref, *out_refs)


_AB_IN_WEIGHTS = ["ab_w_in", "gla_w_a2", "gla_b_a", "gmlp_norm"]


def _ab_in(x, mod, l, e, w, tm):
    g_n, t_n, _ = x.shape
    out = jax.ShapeDtypeStruct((g_n, t_n, HP), F32)
    return pl.pallas_call(
        _ab_in_kernel,
        grid=(g_n, t_n // tm),
        in_specs=[_tok_spec(tm, D_MODEL), _mod_spec(mod, l, 3, tm, t_n), _mod_spec(mod, l, 4, tm, t_n),
                  _layer_spec(w["mix_norm"], l), _layer_spec(w["ab_w_in"], e), _layer_spec(w["gla_w_a2"], e),
                  _layer_spec(w["gla_b_a"], e), _layer_spec(w["gmlp_norm"], e)],
        out_specs=[_tok_spec(tm, HP)] * 7,
        out_shape=[out] * 7,
        compiler_params=_params("parallel", "parallel"),
        name="ab_in",
    )(x, mod, mod, w["mix_norm"], w["ab_w_in"], w["gla_w_a2"], w["gla_b_a"], w["gmlp_norm"])


def _gla_chunk_intra(q, k, v, la, tril, causal, seq):
    n, width = q.shape
    b = _dot01(tril, la)
    if seq == n:
        bm = b[n // 2 - 1:n // 2, :]
        bl = b[n - 1:n, :]
        qd = (q * jnp.exp(b)).astype(BF16)
        qe = (q * jnp.exp(b - bm)).astype(BF16)
        ke = (k * jnp.exp(bm - b)).astype(BF16)
    else:
        assert seq == SUBLANES
        last = b.reshape(n // seq, seq, width)[:, seq - 1:seq, :]
        bl = jnp.broadcast_to(last, (n // seq, seq, width)).reshape(n, width)
        qd = q * jnp.exp(b)
        qe = qd.astype(BF16)
        ke = (k * jnp.exp(-b)).astype(BF16)
    kd = k * jnp.exp(bl - b)
    vb = v.astype(BF16)
    o_intra = []
    for h in range(GLA_H):
        cs = slice(h * LANES, (h + 1) * LANES)
        a = jnp.where(causal, _dot_nt(qe[:, cs], ke[:, cs]), 0.0)
        o_intra.append(_dot(a, vb[:, cs]))
    if seq == n:
        kd = kd.astype(BF16)
    return o_intra, qd, kd, vb, jnp.exp(bl)


def _ab_mix_kernel(*refs, chunk, carry, aliased):
    if aliased:
        refs = refs[1:]
    if carry:
        (x_ref, gt_ref, q_ref, k_ref, v_ref, g_ref, la_ref, u_ref, vb_ref, gn_ref, ws_ref, bs_ref, wo_ref,
         xo_ref, so_ref, o_scr, st_scr) = refs
    else:
        (x_ref, gt_ref, q_ref, k_ref, v_ref, g_ref, la_ref, u_ref, vb_ref, gn_ref, ws_ref, bs_ref, wo_ref, s0_ref,
         xo_ref, so_ref, o_scr, qd_scr, kd_scr, dec_scr) = refs
    tm = q_ref.shape[0]

    if carry:
        causal, _ = _causal(chunk)
        tril = jnp.where(causal, 1.0, 0.0).astype(BF16)

        @pl.when(pl.program_id(1) == 0)
        def _():
            st_scr[...] = jnp.zeros_like(st_scr)

        n_chunks = tm // chunk
        intra = [_gla_chunk_intra(q_ref[c * chunk:(c + 1) * chunk, :], k_ref[c * chunk:(c + 1) * chunk, :],
                                  v_ref[c * chunk:(c + 1) * chunk, :], la_ref[c * chunk:(c + 1) * chunk, :],
                                  tril, causal, chunk) for c in range(n_chunks)]
        for h in range(GLA_H):
            cs = slice(h * LANES, (h + 1) * LANES)
            st = st_scr[h]
            for c in range(n_chunks):
                o_intra, qd, kd, vb, dec = intra[c]
                o_scr[c * chunk:(c + 1) * chunk, cs] = o_intra[h] + _dot_nt(qd[:, cs], st)
                st = st * dec[:, cs] + _dot_tn(vb[:, cs], kd[:, cs])
            st_scr[h] = st

        @pl.when(pl.program_id(1) == pl.num_programs(1) - 1)
        def _():
            for h in range(GLA_H):
                so_ref[h] = st_scr[h].T[:GLA_DK, :]
    else:
        row = lax.broadcasted_iota(jnp.int32, (tm, tm), 0)
        col = lax.broadcasted_iota(jnp.int32, (tm, tm), 1)
        causal = jnp.logical_and(row // chunk == col // chunk, row >= col)
        tril = jnp.where(causal, 1.0, 0.0).astype(BF16)
        o_intra, qd, kd, _, dec = _gla_chunk_intra(q_ref[...], k_ref[...], v_ref[...], la_ref[...], tril, causal, chunk)
        for h in range(GLA_H):
            o_scr[:, h * LANES:(h + 1) * LANES] = o_intra[h]
        qd_scr[...] = qd
        kd_scr[...] = kd
        dec_scr[...] = dec
        key_pad = jnp.zeros((LANES - GLA_DK, GLA_DV), F32)

        def seq_body(c, _):
            r0 = pl.multiple_of(c * chunk, chunk)
            rows = pl.ds(r0, chunk)
            for h in range(GLA_H):
                cs = slice(h * LANES, (h + 1) * LANES)
                st = jnp.concatenate([s0_ref[c, h], key_pad], axis=0).T
                o_scr[rows, cs] = o_scr[rows, cs] + _dot_nt(qd_scr[rows, cs], st)
                st_new = st * dec_scr[pl.ds(r0, 1), cs] + _dot_tn(v_ref[rows, cs], kd_scr[rows, cs])
                so_ref[c, h] = st_new.T[:GLA_DK, :]
            return 0

        lax.fori_loop(0, tm // chunk, seq_body, 0, unroll=4)

    mix = None
    for h in range(GLA_H):
        cs = slice(h * GLA_DV, (h + 1) * GLA_DV)
        on = _rms(o_scr[:, cs], gn_ref[...]) * _silu(g_ref[:, cs])
        p = _dot(on, wo_ref[cs, :])
        mix = p if mix is None else mix + p
    for gi in range(GMLP_G):
        cs = slice(gi * GMLP_DG, (gi + 1) * GMLP_DG)
        zs = []
        for r0 in range(0, tm, GMLP_CHUNK):
            zs.append(_dot(ws_ref[gi], vb_ref[r0:r0 + GMLP_CHUNK, cs]) + bs_ref[gi])
        z = zs[0] if len(zs) == 1 else jnp.concatenate(zs, axis=0)
        mix = mix + _dot(u_ref[:, cs] * z, wo_ref[GLA_H * GLA_DV + gi * GMLP_DG:GLA_H * GLA_DV + (gi + 1) * GMLP_DG, :])
    xo_ref[...] = x_ref[...] + _mod_rows(gt_ref, tm) * mix


def _ab_mix(x, mod, l, e, w, proj, s0, tm, chunk, ws, bs, s_stack):
    g_n, t_n, _ = x.shape
    carry = s0 is None
    in_specs = ([_tok_spec(tm, D_MODEL), _mod_spec(mod, l, 5, tm, t_n)] + [_tok_spec(tm, HP)] * 7
                + [_layer_spec(w["gla_norm"], e), _layer_spec(ws, e), _layer_spec(bs, e), _layer_spec(w["ab_w_out"], e)])
    args = [x, mod, *proj, w["gla_norm"], ws, bs, w["ab_w_out"]]
    scratch = [pltpu.VMEM((tm, HP), F32)]
    if carry:
        s_shape = (DEPTH - DEPTH // 2, g_n, GLA_H, GLA_DK, GLA_DV)
        s_spec = pl.BlockSpec((None, None, GLA_H, GLA_DK, GLA_DV), lambda g, i: (e, g, 0, 0, 0))
        scratch.append(pltpu.VMEM((GLA_H, LANES, LANES), F32))
    else:
        s_shape = s0.shape
        s_spec = pl.BlockSpec((None, None) + s0.shape[2:], lambda g, i: (e, g, 0, 0, 0, 0))
        in_specs.append(s_spec)
        args.append(s0)
        scratch += [pltpu.VMEM((tm, HP), F32)] * 3
    aliases = {}
    if s_stack is not None:
        aliases = {0: 1}
        in_specs.insert(0, pl.BlockSpec(memory_space=pl.ANY))
        args.insert(0, s_stack)
    return pl.pallas_call(
        functools.partial(_ab_mix_kernel, chunk=chunk, carry=carry, aliased=s_stack is not None),
        grid=(g_n, t_n // tm),
        in_specs=in_specs,
        out_specs=[_tok_spec(tm, D_MODEL), s_spec],
        out_shape=[jax.ShapeDtypeStruct(x.shape, F32), jax.ShapeDtypeStruct(s_shape, F32)],
        scratch_shapes=scratch,
        input_output_aliases=aliases,
        compiler_params=_params("parallel", "arbitrary"),
        name="ab_mix",
    )(*args)


def _ml_in_kernel(x_ref, sh_ref, sc_ref, nw_ref, w_ref, xm_ref, z_ref):
    tm = x_ref.shape[0]
    h = (_rms(x_ref[...], nw_ref[...]) * (1.0 + _mod_rows(sc_ref, tm)) + _mod_rows(sh_ref, tm)).astype(BF16)
    xm_ref[...] = jnp.dot(h, w_ref[:, :ML_INNER], preferred_element_type=F32)
    z_ref[...] = jnp.dot(h, w_ref[:, ML_INNER:], preferred_element_type=F32)


def _ml_in(x, mod, l, o, w, tm):
    g_n, t_n, _ = x.shape
    out = jax.ShapeDtypeStruct((g_n, t_n, ML_INNER), F32)
    return pl.pallas_call(
        _ml_in_kernel,
        grid=(g_n, t_n // tm),
        in_specs=[_tok_spec(tm, D_MODEL), _mod_spec(mod, l, 3, tm, t_n), _mod_spec(mod, l, 4, tm, t_n),
                  _layer_spec(w["mix_norm"], l), _layer_spec(w["ml_w_in"], o)],
        out_specs=[_tok_spec(tm, ML_INNER)] * 2,
        out_shape=[out] * 2,
        compiler_params=_params("parallel", "parallel"),
        name="ml_in",
    )(x, mod, mod, w["mix_norm"], w["ml_w_in"])


HALO = SUBLANES
CONV_FIRST = HALO - (ML_CONV - 1)


def _ml_qkv_gates(xm_ref, ext_scr, nb, cw_ref, cb_ref, wq_ref, wk_ref, wv_ref, wg_ref, bg_ref,
                  xc_ref, q_ref, k_ref, v_ref):
    tm = xm_ref.shape[0]
    ls = tm // nb
    gates = bg_ref[...]
    for blk in range(ML_INNER // MXU_DIM):
        cs = slice(blk * MXU_DIM, (blk + 1) * MXU_DIM)
        pre = cb_ref[:, cs]
        for j in range(ML_CONV):
            pre = pre + ext_scr[:, CONV_FIRST + j:CONV_FIRST + j + ls, cs].reshape(tm, MXU_DIM) * cw_ref[j:j + 1, cs]
        xc = _silu(pre)
        xc_ref[:, cs] = xc
        xcb = xc.astype(BF16)
        q = jnp.dot(xcb, wq_ref[blk], preferred_element_type=F32).astype(BF16)
        k = jnp.dot(xcb, wk_ref[blk], preferred_element_type=F32).astype(BF16)
        v = jnp.dot(xm_ref[:, cs].astype(BF16), wv_ref[blk], preferred_element_type=F32).astype(BF16)
        gates = (gates + jnp.dot(q, wg_ref[cs, :], preferred_element_type=F32)
                 + jnp.dot(k, wg_ref[ML_INNER + blk * MXU_DIM:ML_INNER + (blk + 1) * MXU_DIM, :],
                           preferred_element_type=F32)
                 + jnp.dot(v, wg_ref[2 * ML_INNER + blk * MXU_DIM:2 * ML_INNER + (blk + 1) * MXU_DIM, :],
                           preferred_element_type=F32))
        q_ref[:, cs] = q.astype(q_ref.dtype)
        k_ref[:, cs] = k.astype(k_ref.dtype)
        v_ref[:, cs] = v.astype(v_ref.dtype)
    lane = lax.broadcasted_iota(jnp.int32, gates.shape, 1)
    return jnp.where(lane < ML_H, gates, _log_sigmoid(gates))


def _mlstm_head(q, k, v, ig_col, f_col, c_prev, n_prev, m_prev, causal, eye):
    q, k, v = q.astype(BF16), k.astype(BF16), v.astype(BF16)
    n_rows = q.shape[0]
    scale = ML_DH ** -0.5
    ig_row = _col_to_row(ig_col, eye)
    f_row = _col_to_row(f_col, eye)
    log_d = jnp.where(causal, f_col - f_row + ig_row, -jnp.inf)
    log_inter = f_col + m_prev
    m_t = jnp.maximum(log_inter, jnp.max(log_d, axis=-1, keepdims=True))
    d = jnp.exp(log_d - m_t) * scale
    w_inter = jnp.exp(log_inter - m_t)
    s = _dot_nt(q, k) * d
    qn = _dot_nt(q, jnp.broadcast_to(n_prev, (SUBLANES, ML_DH)))[:, 0:1]
    num = _dot(s, v) + w_inter * _dot_nt(q, c_prev)
    den = jnp.sum(s, axis=-1, keepdims=True) + w_inter * qn
    hh = num / jnp.maximum(jnp.abs(den), jnp.exp(-m_t))
    m_new = m_t[n_rows - 1:n_rows, :]
    f_last = f_col[n_rows - 1:n_rows, :]
    w_rows = jnp.exp(f_last - f_col + ig_col - m_new) * scale
    w_wide = jnp.broadcast_to(w_rows, (n_rows, ML_DH)).astype(BF16)
    decay = jnp.exp(f_last + m_prev - m_new)
    c_new = decay * c_prev + _dot_tn(v * w_wide, k)
    n_new = decay * n_prev + _dot_tn(w_wide[:, 0:LANES], k)[0:1, :]
    return hh, c_new, n_new, m_new


def _ml_out_gate(hh, xc, z, gn, sk):
    mu = jnp.mean(hh, axis=-1, keepdims=True)
    hc = hh - mu
    var = jnp.mean(hc * hc, axis=-1, keepdims=True)
    hn = hc * lax.rsqrt(var + EPS) * gn
    return ((hn + sk * xc) * _silu(z)).astype(BF16)


_ML_WEIGHTS = ["ml_conv_w", "ml_conv_b", "ml_wq", "ml_wk", "ml_wv", "ml_w_gates", "ml_b_gates"]
_ML_OUT_WEIGHTS = ["ml_norm", "ml_skip", "ml_w_out"]


def _ml_core_kernel(*refs, aliased):
    if aliased:
        refs = refs[1:]
    (x_ref, sh_ref, sc_ref, gt_ref, nw_ref, win_ref, cw_ref, cb_ref, wq_ref, wk_ref, wv_ref, wg_ref, bg_ref,
     gn_ref, sk_ref, wo_ref, xo_ref, co_ref, no_ref, mo_ref, cvo_ref,
     ext_scr, q_scr, k_scr, v_scr, xc_scr, xm_ref, z_ref) = refs
    tm = x_ref.shape[0]
    causal, eye = _causal(tm)
    tril = jnp.where(causal, 1.0, 0.0).astype(BF16)

    h = (_rms(x_ref[...], nw_ref[...]) * (1.0 + _mod_rows(sc_ref, tm)) + _mod_rows(sh_ref, tm)).astype(BF16)
    xm_ref[...] = jnp.dot(h, win_ref[:, :ML_INNER], preferred_element_type=F32)
    z_ref[...] = jnp.dot(h, win_ref[:, ML_INNER:], preferred_element_type=F32)

    @pl.when(pl.program_id(1) == 0)
    def _():
        ext_scr[:, 0:HALO, :] = jnp.zeros((1, HALO, ML_INNER), F32)
        co_ref[...] = jnp.zeros_like(co_ref)
        no_ref[...] = jnp.zeros_like(no_ref)
        mo_ref[...] = jnp.zeros_like(mo_ref)

    ext_scr[:, HALO:HALO + tm, :] = xm_ref[...].reshape(1, tm, ML_INNER)
    gl = _ml_qkv_gates(xm_ref, ext_scr, 1, cw_ref, cb_ref, wq_ref, wk_ref, wv_ref, wg_ref, bg_ref,
                       xc_scr, q_scr, k_scr, v_scr)
    cvo_ref[...] = ext_scr[:, CONV_FIRST + tm:HALO + tm, :].reshape(cvo_ref.shape)
    ext_scr[:, 0:HALO, :] = ext_scr[:, tm:tm + HALO, :]

    cum = _dot01(tril, gl)
    mix = None
    for h in range(ML_H):
        cs = slice(h * ML_DH, (h + 1) * ML_DH)
        hh, c_new, n_new, m_new = _mlstm_head(
            q_scr[:, cs], k_scr[:, cs], v_scr[:, cs], gl[:, h:h + 1], cum[:, ML_H + h:ML_H + h + 1],
            co_ref[h], no_ref[h:h + 1, :], mo_ref[0:1, h:h + 1], causal, eye)
        co_ref[h] = c_new
        no_ref[h:h + 1, :] = n_new
        mo_ref[0:1, h:h + 1] = m_new
        p = jnp.dot(_ml_out_gate(hh, xc_scr[:, cs], z_ref[:, cs], gn_ref[:, cs], sk_ref[:, cs]), wo_ref[cs, :],
                    preferred_element_type=F32)
        mix = p if mix is None else mix + p
    xo_ref[...] = x_ref[...] + _mod_rows(gt_ref, tm) * mix


def _ml_core(x, mod, l, o, w, tm, c_stack):
    g_n, t_n, _ = x.shape
    c_spec = pl.BlockSpec((None, None, ML_H, ML_DH, ML_DH), lambda g, i: (o, g, 0, 0, 0))
    names = ["ml_w_in"] + _ML_WEIGHTS + _ML_OUT_WEIGHTS
    in_specs = ([_tok_spec(tm, D_MODEL)] + [_mod_spec(mod, l, j, tm, t_n) for j in (3, 4, 5)]
                + [_layer_spec(w["mix_norm"], l)] + [_layer_spec(w[n], o) for n in names])
    args = [x, mod, mod, mod, w["mix_norm"]] + [w[n] for n in names]
    aliases = {}
    if c_stack is not None:
        aliases = {0: 1}
        in_specs.insert(0, pl.BlockSpec(memory_space=pl.ANY))
        args.insert(0, c_stack)
    return pl.pallas_call(
        functools.partial(_ml_core_kernel, aliased=c_stack is not None),
        grid=(g_n, t_n // tm),
        in_specs=in_specs,
        out_specs=[_tok_spec(tm, D_MODEL), c_spec,
                   pl.BlockSpec((None, ML_H, ML_DH), lambda g, i: (g, 0, 0)),
                   pl.BlockSpec((None, 1, ML_H), lambda g, i: (g, 0, 0)),
                   pl.BlockSpec((None, ML_CONV - 1, ML_INNER), lambda g, i: (g, 0, 0))],
        out_shape=[jax.ShapeDtypeStruct(x.shape, F32),
                   jax.ShapeDtypeStruct((DEPTH // 2, g_n, ML_H, ML_DH, ML_DH), F32),
                   jax.ShapeDtypeStruct((g_n, ML_H, ML_DH), F32),
                   jax.ShapeDtypeStruct((g_n, 1, ML_H), F32),
                   jax.ShapeDtypeStruct((g_n, ML_CONV - 1, ML_INNER), F32)],
        scratch_shapes=[pltpu.VMEM((1, HALO + tm, ML_INNER), F32)] + [pltpu.VMEM((tm, ML_INNER), BF16)] * 3
        + [pltpu.VMEM((tm, ML_INNER), F32)] * 3,
        input_output_aliases=aliases,
        compiler_params=_params("parallel", "arbitrary"),
        name="ml_core",
    )(*args)


def _ml_pre_kernel(xm_ref, c0_ref, cw_ref, cb_ref, wq_ref, wk_ref, wv_ref, wg_ref, bg_ref,
                   xc_ref, q_ref, k_ref, v_ref, gl_ref, co_ref, ext_scr, *, nb):
    tm = xm_ref.shape[0]
    ls = tm // nb
    ext_scr[:, CONV_FIRST:HALO, :] = c0_ref[...]
    ext_scr[:, HALO:HALO + ls, :] = xm_ref[...].reshape(nb, ls, ML_INNER)
    gl_ref[...] = _ml_qkv_gates(xm_ref, ext_scr, nb, cw_ref, cb_ref, wq_ref, wk_ref, wv_ref, wg_ref, bg_ref,
                                xc_ref, q_ref, k_ref, v_ref)
    co_ref[...] = ext_scr[:, CONV_FIRST + ls:HALO + ls, :]


def _ml_pre(xm, c0, o, w, nb):
    g_n, tm, _ = xm.shape
    c_spec = pl.BlockSpec((None,) + c0.shape[1:], lambda g, i: (g, 0, 0, 0))
    big = jax.ShapeDtypeStruct((g_n, tm, ML_INNER), F32)
    return pl.pallas_call(
        functools.partial(_ml_pre_kernel, nb=nb),
        grid=(g_n, 1),
        in_specs=[_tok_spec(tm, ML_INNER), c_spec] + [_layer_spec(w[n], o) for n in _ML_WEIGHTS],
        out_specs=[_tok_spec(tm, ML_INNER)] * 4 + [_tok_spec(tm, LANES), c_spec],
        out_shape=[big] * 4 + [jax.ShapeDtypeStruct((g_n, tm, LANES), F32), jax.ShapeDtypeStruct(c0.shape, F32)],
        scratch_shapes=[pltpu.VMEM((nb, HALO + tm // nb, ML_INNER), F32)],
        compiler_params=_params("parallel", "arbitrary"),
        name="ml_pre",
    )(xm, c0, *[w[n] for n in _ML_WEIGHTS])


def _ml_scan_kernel(*refs, aliased):
    if aliased:
        refs = refs[1:]
    (q_ref, k_ref, v_ref, gl_ref, c0_ref, n0_ref, m0_ref, hh_ref, co_ref, no_ref, mo_ref) = refs
    n_seq = c0_ref.shape[0]
    n_rows = q_ref.shape[0] // n_seq
    causal, eye = _causal(n_rows)
    tril = jnp.where(causal, 1.0, 0.0).astype(BF16)
    for s in range(n_seq):
        rows = slice(s * n_rows, (s + 1) * n_rows)
        gl = gl_ref[rows, :]
        cum = _dot01(tril, gl)
        for h in range(ML_H):
            cs = slice(h * ML_DH, (h + 1) * ML_DH)
            hh, c_new, n_new, m_new = _mlstm_head(
                q_ref[rows, cs], k_ref[rows, cs], v_ref[rows, cs], gl[:, h:h + 1], cum[:, ML_H + h:ML_H + h + 1],
                c0_ref[s, h], n0_ref[s, h:h + 1, :], m0_ref[s, 0:1, h:h + 1], causal, eye)
            hh_ref[rows, cs] = hh
            co_ref[s, h] = c_new
            no_ref[s, h:h + 1, :] = n_new
            mo_ref[s, 0:1, h:h + 1] = m_new


def _ml_scan(q, k, v, gl, state, o, c_stack, n_seq):
    g_n, tm, _ = q.shape
    b_n = g_n * n_seq
    c_spec = pl.BlockSpec((None, n_seq, ML_H, ML_DH, ML_DH), lambda g, i: (o, g, 0, 0, 0))
    n_spec = pl.BlockSpec((None, n_seq, ML_H, ML_DH), lambda g, i: (o, g, 0, 0))
    m_spec = pl.BlockSpec((None, n_seq, 1, ML_H), lambda g, i: (o, g, 0, 0))
    in_specs = [_tok_spec(tm, ML_INNER)] * 3 + [_tok_spec(tm, LANES), c_spec, n_spec, m_spec]
    args = [q, k, v, gl, *state]
    aliases = {}
    if c_stack is not None:
        aliases = {0: 1}
        in_specs.insert(0, pl.BlockSpec(memory_space=pl.ANY))
        args.insert(0, c_stack)
    return pl.pallas_call(
        functools.partial(_ml_scan_kernel, aliased=c_stack is not None),
        grid=(g_n, 1),
        in_specs=in_specs,
        out_specs=[_tok_spec(tm, ML_INNER), c_spec,
                   pl.BlockSpec((n_seq, ML_H, ML_DH), lambda g, i: (g, 0, 0)),
                   pl.BlockSpec((n_seq, 1, ML_H), lambda g, i: (g, 0, 0))],
        out_shape=[jax.ShapeDtypeStruct((g_n, tm, ML_INNER), F32),
                   jax.ShapeDtypeStruct((DEPTH // 2, b_n, ML_H, ML_DH, ML_DH), F32),
                   jax.ShapeDtypeStruct((b_n, ML_H, ML_DH), F32),
                   jax.ShapeDtypeStruct((b_n, 1, ML_H), F32)],
        input_output_aliases=aliases,
        compiler_params=_params("parallel", "arbitrary"),
        name="ml_scan",
    )(*args)


def _ml_post_kernel(x_ref, gt_ref, hh_ref, xc_ref, z_ref, gn_ref, sk_ref, wo_ref, xo_ref):
    tm = x_ref.shape[0]
    outs = []
    for h in range(ML_H):
        cs = slice(h * ML_DH, (h + 1) * ML_DH)
        outs.append(_ml_out_gate(hh_ref[:, cs], xc_ref[:, cs], z_ref[:, cs], gn_ref[:, cs], sk_ref[:, cs]))
    mix = jnp.dot(jnp.concatenate(outs, axis=1), wo_ref[...], preferred_element_type=F32)
    xo_ref[...] = x_ref[...] + _mod_rows(gt_ref, tm) * mix


def _ml_post(x, mod, l, o, w, hh, xc, z, tm):
    g_n, t_n, _ = x.shape
    return pl.pallas_call(
        _ml_post_kernel,
        grid=(g_n, t_n // tm),
        in_specs=[_tok_spec(tm, D_MODEL), _mod_spec(mod, l, 5, tm, t_n)] + [_tok_spec(tm, ML_INNER)] * 3
        + [_layer_spec(w[n], o) for n in _ML_OUT_WEIGHTS],
        out_specs=_tok_spec(tm, D_MODEL),
        out_shape=jax.ShapeDtypeStruct(x.shape, F32),
        compiler_params=_params("parallel", "parallel"),
        name="ml_post",
    )(x, mod, hh, xc, z, *[w[n] for n in _ML_OUT_WEIGHTS])


def _pad_heads(a):
    lead = a.shape[:-1]
    a = a.reshape(lead + (GLA_H, GLA_DK))
    a = jnp.pad(a, [(0, 0)] * len(lead) + [(0, 0), (0, LANES - GLA_DK)])
    return a.reshape(lead + (HP,))


def _block_diag_dense(wb):
    rows = wb.reshape(wb.shape[0], -1, MXU_DIM, ML_BLOCK)
    dense = jnp.tile(rows, (1, 1, 1, MXU_DIM // ML_BLOCK))
    r_blk = lax.broadcasted_iota(jnp.int32, (MXU_DIM, MXU_DIM), 0) // ML_BLOCK
    c_blk = lax.broadcasted_iota(jnp.int32, (MXU_DIM, MXU_DIM), 1) // ML_BLOCK
    return jnp.where(r_blk == c_blk, dense, 0.0).astype(BF16)


def _prepare_weights(p):
    w = {}
    w["ffn_norm"] = p["ffn_norm"].reshape(DEPTH, 2, 1, D_MODEL)
    for name in ("ffn_w_gate", "ffn_w_up", "ffn_w_down"):
        w[name] = p[name]
    w["mix_norm"] = p["mix_norm"].reshape(DEPTH, 1, D_MODEL)
    w_in = p["ab_w_in"]
    qk = GLA_H * GLA_DK
    vw = GLA_H * GLA_DV
    o_k, o_v, o_g, o_a = qk, 2 * qk, 2 * qk + vw, 2 * qk + 2 * vw
    o_u = o_a + GLA_LOWRANK
    o_vb = o_u + GMLP_W
    w_a = jnp.pad(w_in[:, :, o_a:o_u], ((0, 0), (0, 0), (0, LANES - GLA_LOWRANK)))
    w["ab_w_in"] = jnp.concatenate(
        [_pad_heads(w_in[:, :, :o_k]), _pad_heads(w_in[:, :, o_k:o_v]), w_in[:, :, o_v:o_g], w_in[:, :, o_g:o_a],
         w_a, w_in[:, :, o_u:o_vb], w_in[:, :, o_vb:]], axis=-1).astype(BF16)
    w["gla_w_a2"] = jnp.pad(_pad_heads(p["gla_w_a2"]), ((0, 0), (0, LANES - GLA_LOWRANK), (0, 0))).astype(BF16)
    w["gla_b_a"] = _pad_heads(p["gla_b_a"])[:, None, :]
    w["gla_norm"] = p["gla_norm"][:, None, :]
    w["gmlp_norm"] = p["gmlp_norm"][:, None, :]
    w["ab_w_out"] = p["ab_w_out"].astype(BF16)
    w["ml_w_in"] = p["ml_w_in"].astype(BF16)
    w["ml_conv_w"] = p["ml_conv_w"]
    w["ml_conv_b"] = p["ml_conv_b"][:, None, :]
    w["ml_wq"] = _block_diag_dense(p["ml_wq"])
    w["ml_wk"] = _block_diag_dense(p["ml_wk"])
    w["ml_wv"] = _block_diag_dense(p["ml_wv"])
    w["ml_w_gates"] = jnp.pad(p["ml_w_gates"], ((0, 0), (0, 0), (0, LANES - 2 * ML_H))).astype(BF16)
    w["ml_b_gates"] = jnp.pad(p["ml_b_gates"], ((0, 0), (0, LANES - 2 * ML_H)))[:, None, :]
    w["ml_norm"] = p["ml_norm"][:, None, :]
    w["ml_skip"] = p["ml_skip"][:, None, :]
    w["ml_w_out"] = p["ml_w_out"].astype(BF16)
    w["final_norm"] = p["final_norm"][None, :]
    return w


def _spatial_weights(ws, bs, seq_len):
    if seq_len % GMLP_CHUNK == 0:
        length = GMLP_CHUNK
    else:
        length = seq_len
    reps = GMLP_CHUNK // length
    wt = jnp.tril(ws[:, :, :length, :length])
    eye = jnp.eye(reps, dtype=ws.dtype)
    wt = jnp.einsum("egts,ab->egatbs", wt, eye).reshape(ws.shape[0], GMLP_G, GMLP_CHUNK, GMLP_CHUNK)
    bt = jnp.tile(bs[:, :, :length], (1, 1, reps))
    bt = jnp.broadcast_to(bt[:, :, :, None], (ws.shape[0], GMLP_G, GMLP_CHUNK, GMLP_DG))
    return wt.astype(BF16), bt


def _ffn_step(x, mod, l, s, w, ffn_bf16, fmod):
    g_n, t_n, _ = x.shape
    tm = min(FFN_ROW_TILE, t_n)
    names = ("ffn_w_gate", "ffn_w_up", "ffn_w_down")
    if (l, s) not in ffn_bf16:
        ffn_bf16[(l, s)] = tuple(w[n][l, s].astype(BF16) for n in names)
    nxt = (l, 1) if s == 0 else (l + 1, 0)
    steps = g_n * (t_n // tm)
    slab_ok = all(w[n].shape[2] % (steps * 2 * SUBLANES) == 0 for n in names)
    cast = nxt if (nxt[0] < DEPTH and nxt not in ffn_bf16 and slab_ok) else None
    y, converted = _ffn(x, mod, l, s, w, ffn_bf16[(l, s)], tm, fmod, cast)
    if cast is not None:
        ffn_bf16[nxt] = converted
    return y


def _trunk(x, mod, fmod, w, ws, bs, states, ffn_bf16, *, tm, mix_tm, gla_chunk, scan_tm, seq_rows):
    fresh = states is None
    g_n, t_n, _ = x.shape
    tm = min(tm, t_n)
    v_out, n_out, m_out, conv_out = [], [], [], []
    s_stack = c_stack = None
    for l in range(DEPTH):
        x = _ffn_step(x, mod, l, 0, w, ffn_bf16, None)
        if l % 2 == 0:
            e = l // 2
            proj = _ab_in(x, mod, l, e, w, tm)
            if fresh:
                x, s_stack = _ab_mix(x, mod, l, e, w, proj, None, mix_tm, gla_chunk, ws, bs, s_stack)
            else:
                v_out.append(proj[6])
                nseq = mix_tm // seq_rows
                view = lambda a: a.reshape(-1, mix_tm, a.shape[-1])
                s_all = states["gla_S"]
                s0 = s_all.reshape(s_all.shape[0], -1, nseq, GLA_H, GLA_DK, GLA_DV)
                mod_v = mod.reshape(mod.shape[0], mod.shape[1], -1, nseq, D_MODEL)
                xv, s_stack = _ab_mix(view(x), mod_v, l, e, w, [view(a) for a in proj], s0, mix_tm, gla_chunk, ws, bs,
                                      s_stack)
                x = xv.reshape(g_n, t_n, D_MODEL)
        else:
            o = l // 2
            if fresh:
                x, c_stack, n_new, m_new, conv_new = _ml_core(x, mod, l, o, w, scan_tm, c_stack)
                m_new = m_new.reshape(g_n, ML_H)
            else:
                xm, z = _ml_in(x, mod, l, o, w, tm)
                nseq = LANES // seq_rows
                rows = nseq * seq_rows
                view = lambda a, r: a.reshape(-1, r, a.shape[-1])
                c0 = states["ml_conv"][o].reshape(-1, nseq, ML_CONV - 1, ML_INNER)
                xc, q, k, v, gl, conv_new = _ml_pre(view(xm, rows), c0, o, w, nseq)
                conv_new = conv_new.reshape(-1, ML_CONV - 1, ML_INNER)
                st = (states["ml_C"], states["ml_n"], states["ml_m"][:, :, None, :])
                srows = SCAN_SEQS * seq_rows
                hh, c_stack, n_new, m_new = _ml_scan(view(q, srows), view(k, srows), view(v, srows),
                                                     view(gl, srows), st, o, c_stack, SCAN_SEQS)
                m_new = m_new.reshape(-1, ML_H)
                xc = xc.reshape(g_n, t_n, ML_INNER)
                hh = hh.reshape(g_n, t_n, ML_INNER)
                x = _ml_post(x, mod, l, o, w, hh, xc, z, min(tm, 2 * ML_CHUNK))
            n_out.append(n_new)
            m_out.append(m_new)
            conv_out.append(conv_new)
        x = _ffn_step(x, mod, l, 1, w, ffn_bf16, fmod if l == DEPTH - 1 else None)
    y = x
    s_stack = s_stack.reshape(s_stack.shape[0], -1, GLA_H, GLA_DK, GLA_DV)
    v_stack = jnp.stack(v_out) if v_out else None
    return (y, s_stack, v_stack, c_stack, jnp.stack(n_out), jnp.stack(m_out),
            jnp.stack(conv_out))


def kernel(x_prompt, x_sample, c_prompt, c_sample, state_gla_S, state_mlstm_C, state_mlstm_n, state_mlstm_m, state_mlstm_conv, ada_w, ada_b, ffn_norm, ffn_w_gate, ffn_w_up, ffn_w_down, mix_norm, ab_w_in, gla_w_a2, gla_b_a, gla_norm, gmlp_norm, gmlp_ws, gmlp_bs, ab_w_out, ml_w_in, ml_conv_w, ml_conv_b, ml_wq, ml_wk, ml_wv, ml_w_gates, ml_b_gates, ml_norm, ml_skip, ml_w_out, final_norm, final_ada_w, final_ada_b):
    p = dict(ffn_norm=ffn_norm, ffn_w_gate=ffn_w_gate, ffn_w_up=ffn_w_up, ffn_w_down=ffn_w_down, mix_norm=mix_norm,
             ab_w_in=ab_w_in, gla_w_a2=gla_w_a2, gla_b_a=gla_b_a, gla_norm=gla_norm, gmlp_norm=gmlp_norm,
             ab_w_out=ab_w_out, ml_w_in=ml_w_in, ml_conv_w=ml_conv_w, ml_conv_b=ml_conv_b, ml_wq=ml_wq, ml_wk=ml_wk,
             ml_wv=ml_wv, ml_w_gates=ml_w_gates, ml_b_gates=ml_b_gates, ml_norm=ml_norm, ml_skip=ml_skip,
             ml_w_out=ml_w_out, final_norm=final_norm)
    w = _prepare_weights(p)
    n_p, t_p, _ = x_prompt.shape
    n_s, t_s, _ = x_sample.shape

    c_all = jnp.concatenate([c_prompt, c_sample], axis=0)
    mod = _ada(c_all, ada_w, ada_b)
    fmod = _ada(c_all, final_ada_w[None], final_ada_b[None])

    def split_mod(m):
        return m[:, :, :n_p, None, :], m[:, :, None, n_p:, :]

    mod_p, mod_s = split_mod(mod)
    fmod_p, fmod_s = split_mod(fmod)

    ws_p, bs_p = _spatial_weights(gmlp_ws, gmlp_bs, t_p)
    ws_s, bs_s = _spatial_weights(gmlp_ws, gmlp_bs, t_s)

    ffn_bf16 = {}
    y_p, s_p, _, c_p, n_p_, m_p, cv_p = _trunk(
        x_prompt, mod_p, fmod_p, w, ws_p, bs_p, None, ffn_bf16, tm=ROW_TILE, mix_tm=ROW_TILE,
        gla_chunk=_chunk_len(t_p, GLA_TILE_CHUNK, GLA_CHUNK), scan_tm=_chunk_len(t_p, ML_TILE_CHUNK, ML_CHUNK),
        seq_rows=t_p)

    states = dict(gla_S=state_gla_S, ml_C=state_mlstm_C, ml_n=state_mlstm_n, ml_m=state_mlstm_m,
                  ml_conv=state_mlstm_conv)
    xs = x_sample.reshape(1, n_s * t_s, D_MODEL)
    y_s, s_s, v_s, c_s, n_s_, m_s, cv_s = _trunk(
        xs, mod_s, fmod_s, w, ws_s, bs_s, states, ffn_bf16, tm=ROW_TILE, mix_tm=GMLP_CHUNK, gla_chunk=t_s, scan_tm=t_s,
        seq_rows=t_s)
    y_s = y_s.reshape(n_s, t_s, D_MODEL)
    v_s = v_s.reshape(-1, n_s, t_s, GMLP_W)
    return (y_p, y_s, s_p, s_s, v_s, c_p, c_s, n_p_, n_s_, m_p, m_s, cv_p, cv_s)
```

```python
import functools

import jax
import jax.numpy as jnp
from jax import lax
from jax.experimental import pallas as pl
from jax.experimental.pallas import tpu as pltpu

F32 = jnp.float32
BF16 = jnp.bfloat16
EPS = 1e-6

D_MODEL = 1024
DEPTH = 4
D_FF = 2816
GLA_H = 4
GLA_DK = 64
GLA_DV = 128
GLA_LOWRANK = 16
GLA_TAU = 16.0
GLA_CHUNK = 64
GMLP_G = 4
GMLP_DG = 128
GMLP_W = GMLP_G * GMLP_DG
GMLP_CHUNK = 128
ML_INNER = 2 * D_MODEL
ML_H = 4
ML_DH = ML_INNER // ML_H
ML_CONV = 4
ML_BLOCK = 4
ML_CHUNK = 128

LANES = 128
SUBLANES = 8
MXU_DIM = 256
VMEM_LIMIT_BYTES = 56 * 1024 * 1024

ROW_TILE = 512
GLA_TILE_CHUNK = 128
ML_TILE_CHUNK = 256
FFN_ROW_TILE = 1024
SCAN_SEQS = 2


def _chunk_len(t, preferred, nominal):
    for c in (preferred, nominal):
        if t % c == 0:
            return c
    return t


HP = GLA_H * LANES
AB_Q, AB_K, AB_V, AB_G = 0, HP, 2 * HP, 3 * HP
AB_A = 4 * HP
AB_U = AB_A + LANES
AB_VB = AB_U + GMLP_W
AB_COLS = AB_VB + GMLP_W


def _dot(a, b):
    return jnp.dot(a.astype(BF16), b.astype(BF16), preferred_element_type=F32)


def _dot_nt(a, b):
    return lax.dot_general(a.astype(BF16), b.astype(BF16), (((1,), (1,)), ((), ())), preferred_element_type=F32)


def _dot_tn(a, b):
    return lax.dot_general(a.astype(BF16), b.astype(BF16), (((0,), (0,)), ((), ())), preferred_element_type=F32)


def _dot01(sel, x):
    hi = x.astype(BF16)
    r1 = x - hi.astype(F32)
    mid = r1.astype(BF16)
    lo = (r1 - mid.astype(F32)).astype(BF16)
    return (jnp.dot(sel, hi, preferred_element_type=F32) + jnp.dot(sel, mid, preferred_element_type=F32)
            + jnp.dot(sel, lo, preferred_element_type=F32))


def _rms(x, g):
    return x * lax.rsqrt(jnp.mean(x * x, axis=-1, keepdims=True) + EPS) * g


def _silu(x):
    return x * jax.nn.sigmoid(x)


def _log_sigmoid(x):
    return jnp.minimum(x, 0.0) - jnp.log1p(jnp.exp(-jnp.abs(x)))


def _causal(n):
    row = lax.broadcasted_iota(jnp.int32, (n, n), 0)
    col = lax.broadcasted_iota(jnp.int32, (n, n), 1)
    return row >= col, row == col


def _mod_rows(ref, tm):
    m = ref[...]
    r = m.shape[0]
    if r == 1 or r == tm:
        return m
    rep = tm // r
    if rep == SUBLANES:
        return jnp.broadcast_to(m[:, None, :], (r, rep, m.shape[1])).reshape(tm, m.shape[1])
    row = lax.broadcasted_iota(jnp.int32, (tm, r), 0)
    lo = lax.broadcasted_iota(jnp.int32, (tm, r), 1) * rep
    sel = jnp.where(row >= lo, jnp.where(row < lo + rep, 1.0, 0.0), 0.0).astype(BF16)
    return _dot01(sel, m)


def _col_to_row(col, eye):
    return jnp.sum(jnp.where(eye, col, 0.0), axis=0, keepdims=True)


def _tok_spec(tm, w):
    return pl.BlockSpec((None, tm, w), lambda g, i: (g, i, 0))


def _mod_spec(mod, l, j, tm, t_n):
    t_mod = mod.shape[3]
    if t_mod == 1:
        return pl.BlockSpec((None, None, None, 1, D_MODEL), lambda g, i: (l, j, g, 0, 0))
    return pl.BlockSpec((None, None, None, t_mod * tm // t_n, D_MODEL), lambda g, i: (l, j, g, i, 0))


def _layer_spec(a, *lead):
    n = len(lead)
    shape = (None,) * n + tuple(a.shape[n:])
    zeros = (0,) * (a.ndim - n)
    return pl.BlockSpec(shape, lambda g, i: tuple(lead) + zeros, pipeline_mode=pl.Buffered(1))


def _params(*sem):
    return pltpu.CompilerParams(dimension_semantics=sem, vmem_limit_bytes=VMEM_LIMIT_BYTES)


def _ada_kernel(c_ref, w_ref, b_ref, o_ref):
    cs = _silu(c_ref[...])
    o_ref[...] = _dot(cs, w_ref[...]) + b_ref[...]


def _ada(c, w, b):
    n_l, _, width = w.shape
    n_j = width // D_MODEL
    r = c.shape[0]
    return pl.pallas_call(
        _ada_kernel,
        grid=(n_l, n_j),
        in_specs=[pl.BlockSpec((r, D_MODEL), lambda l, j: (0, 0)),
                  pl.BlockSpec((None, D_MODEL, D_MODEL), lambda l, j: (l, 0, j)),
                  pl.BlockSpec((None, None, 1, D_MODEL), lambda l, j: (l, j, 0, 0))],
        out_specs=pl.BlockSpec((None, None, r, D_MODEL), lambda l, j: (l, j, 0, 0)),
        out_shape=jax.ShapeDtypeStruct((n_l, n_j, r, D_MODEL), F32),
        compiler_params=_params("arbitrary", "arbitrary"),
        name="ada",
    )(c, w, b.reshape(n_l, n_j, 1, D_MODEL))


FFN_CHUNK = 1 * MXU_DIM


def _ffn_kernel(*refs, final, cast):
    x_ref, sh_ref, sc_ref, gt_ref, nw_ref, wg_ref, wu_ref, wd_ref = refs[:8]
    n_in = 8 + (3 if final else 0) + (3 if cast else 0)
    o_ref = refs[n_in]
    x = x_ref[...]
    tm = x.shape[0]
    h = (_rms(x, nw_ref[...]) * (1.0 + _mod_rows(sc_ref, tm)) + _mod_rows(sh_ref, tm)).astype(BF16)
    acc = None
    for f0 in range(0, D_FF, FFN_CHUNK):
        f1 = min(f0 + FFN_CHUNK, D_FF)
        g = jnp.dot(h, wg_ref[:, f0:f1], preferred_element_type=F32)
        u = jnp.dot(h, wu_ref[:, f0:f1], preferred_element_type=F32)
        p = _dot(_silu(g) * u, wd_ref[f0:f1, :])
        acc = p if acc is None else acc + p
    y = x + (0.5 * _mod_rows(gt_ref, tm)) * acc
    if final:
        fsh_ref, fsc_ref, fnw_ref = refs[8:11]
        y = _rms(y, fnw_ref[...]) * (1.0 + _mod_rows(fsc_ref, tm)) + _mod_rows(fsh_ref, tm)
    o_ref[...] = y
    if cast:
        for src, dst in zip(refs[n_in - 3:n_in], refs[n_in + 1:n_in + 4]):
            dst[...] = src[...].astype(BF16)


def _ffn(x, mod, l, s, w, ffn_w, tm, fmod=None, cast=None):
    g_n, t_n, _ = x.shape
    n_i = t_n // tm
    j0 = 6 * s
    whole = lambda a: pl.BlockSpec(a.shape, lambda g, i: (0, 0), pipeline_mode=pl.Buffered(1))
    in_specs = [_tok_spec(tm, D_MODEL), _mod_spec(mod, l, j0, tm, t_n), _mod_spec(mod, l, j0 + 1, tm, t_n),
                _mod_spec(mod, l, j0 + 2, tm, t_n), _layer_spec(w["ffn_norm"], l, s)] + [whole(a) for a in ffn_w]
    args = [x, mod, mod, mod, w["ffn_norm"], *ffn_w]
    out_specs = [_tok_spec(tm, D_MODEL)]
    out_shape = [jax.ShapeDtypeStruct(x.shape, F32)]
    if fmod is not None:
        in_specs += [_mod_spec(fmod, 0, 0, tm, t_n), _mod_spec(fmod, 0, 1, tm, t_n),
                     pl.BlockSpec((1, D_MODEL), lambda g, i: (0, 0))]
        args += [fmod, fmod, w["final_norm"]]
    if cast is not None:
        l2, s2 = cast
        steps = g_n * n_i
        for name in ("ffn_w_gate", "ffn_w_up", "ffn_w_down"):
            src = w[name]
            rows, cols = src.shape[2:]
            in_specs.append(pl.BlockSpec((None, None, rows // steps, cols), lambda g, i: (l2, s2, g * n_i + i, 0)))
            args.append(src)
            out_specs.append(pl.BlockSpec((rows // steps, cols), lambda g, i: (g * n_i + i, 0)))
            out_shape.append(jax.ShapeDtypeStruct((rows, cols), BF16))
    out = pl.pallas_call(
        functools.partial(_ffn_kernel, final=fmod is not None, cast=cast is not None),
        grid=(g_n, n_i),
        in_specs=in_specs,
        out_specs=out_specs,
        out_shape=out_shape,
        compiler_params=_params("parallel", "parallel"),
        name="ffn",
    )(*args)
    return out[0], tuple(out[1:])


def _ab_project(h, w_ref, wa2_ref, ba_ref, vn_ref, q_ref, k_ref, v_ref, g_ref, la_ref, u_ref, vb_ref):
    def seg(c0, width):
        return jnp.dot(h, w_ref[:, c0:c0 + width], preferred_element_type=F32)

    q_ref[...] = seg(AB_Q, HP) * GLA_DK ** -0.5
    k_ref[...] = seg(AB_K, HP)
    v_ref[...] = seg(AB_V, HP)
    g_ref[...] = seg(AB_G, HP)
    xa = _dot(seg(AB_A, LANES), wa2_ref[...]) + ba_ref[...]
    la_ref[...] = _log_sigmoid(xa) * (1.0 / GLA_TAU)
    u_ref[...] = seg(AB_U, GMLP_W)
    vb = seg(AB_VB, GMLP_W)
    for gi in range(GMLP_G):
        cs = slice(gi * GMLP_DG, (gi + 1) * GMLP_DG)
        vb_ref[:, cs] = _rms(vb[:, cs], vn_ref[...])


def _ab_in_kernel(x_ref, sh_ref, sc_ref, nw_ref, w_ref, wa2_ref, ba_ref, vn_ref, *out_refs):
    tm = x_ref.shape[0]
    h = (_rms(x_ref[...], nw_ref[...]) * (1.0 + _mod_rows(sc_ref, tm)) + _mod_rows(sh_ref, tm)).astype(BF16)
    _ab_project(h, w_ref, wa2_ref, ba_ref, vn_ref, *out_refs)


_AB_IN_WEIGHTS = ["ab_w_in", "gla_w_a2", "gla_b_a", "gmlp_norm"]


def _ab_in(x, mod, l, e, w, tm):
    g_n, t_n, _ = x.shape
    out = jax.ShapeDtypeStruct((g_n, t_n, HP), F32)
    return pl.pallas_call(
        _ab_in_kernel,
        grid=(g_n, t_n // tm),
        in_specs=[_tok_spec(tm, D_MODEL), _mod_spec(mod, l, 3, tm, t_n), _mod_spec(mod, l, 4, tm, t_n),
                  _layer_spec(w["mix_norm"], l), _layer_spec(w["ab_w_in"], e), _layer_spec(w["gla_w_a2"], e),
                  _layer_spec(w["gla_b_a"], e), _layer_spec(w["gmlp_norm"], e)],
        out_specs=[_tok_spec(tm, HP)] * 7,
        out_shape=[out] * 7,
        compiler_params=_params("parallel", "parallel"),
        name="ab_in",
    )(x, mod, mod, w["mix_norm"], w["ab_w_in"], w["gla_w_a2"], w["gla_b_a"], w["gmlp_norm"])


def _gla_chunk_intra(q, k, v, la, tril, causal, seq):
    n, width = q.shape
    b = _dot01(tril, la)
    if seq == n:
        bm = b[n // 2 - 1:n // 2, :]
        bl = b[n - 1:n, :]
        qd = (q * jnp.exp(b)).astype(BF16)
        qe = (q * jnp.exp(b - bm)).astype(BF16)
        ke = (k * jnp.exp(bm - b)).astype(BF16)
    else:
        assert seq == SUBLANES
        last = b.reshape(n // seq, seq, width)[:, seq - 1:seq, :]
        bl = jnp.broadcast_to(last, (n // seq, seq, width)).reshape(n, width)
        qd = q * jnp.exp(b)
        qe = qd.astype(BF16)
        ke = (k * jnp.exp(-b)).astype(BF16)
    kd = k * jnp.exp(bl - b)
    vb = v.astype(BF16)
    o_intra = []
    for h in range(GLA_H):
        cs = slice(h * LANES, (h + 1) * LANES)
        a = jnp.where(causal, _dot_nt(qe[:, cs], ke[:, cs]), 0.0)
        o_intra.append(_dot(a, vb[:, cs]))
    if seq == n:
        kd = kd.astype(BF16)
    return o_intra, qd, kd, vb, jnp.exp(bl)


def _ab_mix_kernel(*refs, chunk, carry, aliased):
    if aliased:
        refs = refs[1:]
    if carry:
        (x_ref, gt_ref, q_ref, k_ref, v_ref, g_ref, la_ref, u_ref, vb_ref, gn_ref, ws_ref, bs_ref, wo_ref,
         xo_ref, so_ref, o_scr, st_scr) = refs
    else:
        (x_ref, gt_ref, q_ref, k_ref, v_ref, g_ref, la_ref, u_ref, vb_ref, gn_ref, ws_ref, bs_ref, wo_ref, s0_ref,
         xo_ref, so_ref, o_scr, qd_scr, kd_scr, dec_scr) = refs
    tm = q_ref.shape[0]

    if carry:
        causal, _ = _causal(chunk)
        tril = jnp.where(causal, 1.0, 0.0).astype(BF16)

        @pl.when(pl.program_id(1) == 0)
        def _():
            st_scr[...] = jnp.zeros_like(st_scr)

        n_chunks = tm // chunk
        intra = [_gla_chunk_intra(q_ref[c * chunk:(c + 1) * chunk, :], k_ref[c * chunk:(c + 1) * chunk, :],
                                  v_ref[c * chunk:(c + 1) * chunk, :], la_ref[c * chunk:(c + 1) * chunk, :],
                                  tril, causal, chunk) for c in range(n_chunks)]
        for h in range(GLA_H):
            cs = slice(h * LANES, (h + 1) * LANES)
            st = st_scr[h]
            for c in range(n_chunks):
                o_intra, qd, kd, vb, dec = intra[c]
                o_scr[c * chunk:(c + 1) * chunk, cs] = o_intra[h] + _dot_nt(qd[:, cs], st)
                st = st * dec[:, cs] + _dot_tn(vb[:, cs], kd[:, cs])
            st_scr[h] = st

        @pl.when(pl.program_id(1) == pl.num_programs(1) - 1)
        def _():
            for h in range(GLA_H):
                so_ref[h] = st_scr[h].T[:GLA_DK, :]
    else:
        row = lax.broadcasted_iota(jnp.int32, (tm, tm), 0)
        col = lax.broadcasted_iota(jnp.int32, (tm, tm), 1)
        causal = jnp.logical_and(row // chunk == col // chunk, row >= col)
        tril = jnp.where(causal, 1.0, 0.0).astype(BF16)
        o_intra, qd, kd, _, dec = _gla_chunk_intra(q_ref[...], k_ref[...], v_ref[...], la_ref[...], tril, causal, chunk)
        for h in range(GLA_H):
            o_scr[:, h * LANES:(h + 1) * LANES] = o_intra[h]
        qd_scr[...] = qd
        kd_scr[...] = kd
        dec_scr[...] = dec
        key_pad = jnp.zeros((LANES - GLA_DK, GLA_DV), F32)

        def seq_body(c, _):
            r0 = pl.multiple_of(c * chunk, chunk)
            rows = pl.ds(r0, chunk)
            for h in range(GLA_H):
                cs = slice(h * LANES, (h + 1) * LANES)
                st = jnp.concatenate([s0_ref[c, h], key_pad], axis=0).T
                o_scr[rows, cs] = o_scr[rows, cs] + _dot_nt(qd_scr[rows, cs], st)
                st_new = st * dec_scr[pl.ds(r0, 1), cs] + _dot_tn(v_ref[rows, cs], kd_scr[rows, cs])
                so_ref[c, h] = st_new.T[:GLA_DK, :]
            return 0

        lax.fori_loop(0, tm // chunk, seq_body, 0, unroll=4)

    pieces = []
    for h in range(GLA_H):
        cs = slice(h * GLA_DV, (h + 1) * GLA_DV)
        pieces.append((_rms(o_scr[:, cs], gn_ref[...]) * _silu(g_ref[:, cs])).astype(BF16))
    for gi in range(GMLP_G):
        cs = slice(gi * GMLP_DG, (gi + 1) * GMLP_DG)
        zs = []
        for r0 in range(0, tm, GMLP_CHUNK):
            zs.append(_dot(ws_ref[gi], vb_ref[r0:r0 + GMLP_CHUNK, cs]) + bs_ref[gi])
        z = zs[0] if len(zs) == 1 else jnp.concatenate(zs, axis=0)
        pieces.append((u_ref[:, cs] * z).astype(BF16))
    mix = jnp.dot(jnp.concatenate(pieces, axis=1), wo_ref[...], preferred_element_type=F32)
    xo_ref[...] = x_ref[...] + _mod_rows(gt_ref, tm) * mix


def _ab_mix(x, mod, l, e, w, proj, s0, tm, chunk, ws, bs, s_stack):
    g_n, t_n, _ = x.shape
    carry = s0 is None
    in_specs = ([_tok_spec(tm, D_MODEL), _mod_spec(mod, l, 5, tm, t_n)] + [_tok_spec(tm, HP)] * 7
                + [_layer_spec(w["gla_norm"], e), _layer_spec(ws, e), _layer_spec(bs, e), _layer_spec(w["ab_w_out"], e)])
    args = [x, mod, *proj, w["gla_norm"], ws, bs, w["ab_w_out"]]
    scratch = [pltpu.VMEM((tm, HP), F32)]
    if carry:
        s_shape = (DEPTH - DEPTH // 2, g_n, GLA_H, GLA_DK, GLA_DV)
        s_spec = pl.BlockSpec((None, None, GLA_H, GLA_DK, GLA_DV), lambda g, i: (e, g, 0, 0, 0))
        scratch.append(pltpu.VMEM((GLA_H, LANES, LANES), F32))
    else:
        s_shape = s0.shape
        s_spec = pl.BlockSpec((None, None) + s0.shape[2:], lambda g, i: (e, g, 0, 0, 0, 0))
        in_specs.append(s_spec)
        args.append(s0)
        scratch += [pltpu.VMEM((tm, HP), F32)] * 3
    aliases = {}
    if s_stack is not None:
        aliases = {0: 1}
        in_specs.insert(0, pl.BlockSpec(memory_space=pl.ANY))
        args.insert(0, s_stack)
    return pl.pallas_call(
        functools.partial(_ab_mix_kernel, chunk=chunk, carry=carry, aliased=s_stack is not None),
        grid=(g_n, t_n // tm),
        in_specs=in_specs,
        out_specs=[_tok_spec(tm, D_MODEL), s_spec],
        out_shape=[jax.ShapeDtypeStruct(x.shape, F32), jax.ShapeDtypeStruct(s_shape, F32)],
        scratch_shapes=scratch,
        input_output_aliases=aliases,
        compiler_params=_params("parallel", "arbitrary"),
        name="ab_mix",
    )(*args)


def _ml_in_kernel(x_ref, sh_ref, sc_ref, nw_ref, w_ref, xm_ref, z_ref):
    tm = x_ref.shape[0]
    h = (_rms(x_ref[...], nw_ref[...]) * (1.0 + _mod_rows(sc_ref, tm)) + _mod_rows(sh_ref, tm)).astype(BF16)
    xm_ref[...] = jnp.dot(h, w_ref[:, :ML_INNER], preferred_element_type=F32)
    z_ref[...] = jnp.dot(h, w_ref[:, ML_INNER:], preferred_element_type=F32)


def _ml_in(x, mod, l, o, w, tm):
    g_n, t_n, _ = x.shape
    out = jax.ShapeDtypeStruct((g_n, t_n, ML_INNER), F32)
    return pl.pallas_call(
        _ml_in_kernel,
        grid=(g_n, t_n // tm),
        in_specs=[_tok_spec(tm, D_MODEL), _mod_spec(mod, l, 3, tm, t_n), _mod_spec(mod, l, 4, tm, t_n),
                  _layer_spec(w["mix_norm"], l), _layer_spec(w["ml_w_in"], o)],
        out_specs=[_tok_spec(tm, ML_INNER)] * 2,
        out_shape=[out] * 2,
        compiler_params=_params("parallel", "parallel"),
        name="ml_in",
    )(x, mod, mod, w["mix_norm"], w["ml_w_in"])


HALO = SUBLANES
CONV_FIRST = HALO - (ML_CONV - 1)


def _ml_qkv_gates(xm_ref, ext_scr, nb, cw_ref, cb_ref, wq_ref, wk_ref, wv_ref, wg_ref, bg_ref,
                  xc_ref, q_ref, k_ref, v_ref):
    tm = xm_ref.shape[0]
    ls = tm // nb
    gates = bg_ref[...]
    for blk in range(ML_INNER // MXU_DIM):
        cs = slice(blk * MXU_DIM, (blk + 1) * MXU_DIM)
        pre = cb_ref[:, cs]
        for j in range(ML_CONV):
            pre = pre + ext_scr[:, CONV_FIRST + j:CONV_FIRST + j + ls, cs].reshape(tm, MXU_DIM) * cw_ref[j:j + 1, cs]
        xc = _silu(pre)
        xc_ref[:, cs] = xc
        xcb = xc.astype(BF16)
        q = jnp.dot(xcb, wq_ref[blk], preferred_element_type=F32).astype(BF16)
        k = jnp.dot(xcb, wk_ref[blk], preferred_element_type=F32).astype(BF16)
        v = jnp.dot(xm_ref[:, cs].astype(BF16), wv_ref[blk], preferred_element_type=F32).astype(BF16)
        gates = (gates + jnp.dot(q, wg_ref[cs, :], preferred_element_type=F32)
                 + jnp.dot(k, wg_ref[ML_INNER + blk * MXU_DIM:ML_INNER + (blk + 1) * MXU_DIM, :],
                           preferred_element_type=F32)
                 + jnp.dot(v, wg_ref[2 * ML_INNER + blk * MXU_DIM:2 * ML_INNER + (blk + 1) * MXU_DIM, :],
                           preferred_element_type=F32))
        q_ref[:, cs] = q.astype(q_ref.dtype)
        k_ref[:, cs] = k.astype(k_ref.dtype)
        v_ref[:, cs] = v.astype(v_ref.dtype)
    lane = lax.broadcasted_iota(jnp.int32, gates.shape, 1)
    return jnp.where(lane < ML_H, gates, _log_sigmoid(gates))


def _mlstm_head(q, k, v, ig_col, f_col, c_prev, n_prev, m_prev, causal, eye):
    q, k, v = q.astype(BF16), k.astype(BF16), v.astype(BF16)
    n_rows = q.shape[0]
    scale = ML_DH ** -0.5
    ig_row = _col_to_row(ig_col, eye)
    f_row = _col_to_row(f_col, eye)
    log_d = jnp.where(causal, f_col - f_row + ig_row, -jnp.inf)
    log_inter = f_col + m_prev
    m_t = jnp.maximum(log_inter, jnp.max(log_d, axis=-1, keepdims=True))
    d = jnp.exp(log_d - m_t) * scale
    w_inter = jnp.exp(log_inter - m_t)
    s = _dot_nt(q, k) * d
    qn = _dot_nt(q, jnp.broadcast_to(n_prev, (SUBLANES, ML_DH)))[:, 0:1]
    num = _dot(s, v) + w_inter * _dot_nt(q, c_prev)
    den = jnp.sum(s, axis=-1, keepdims=True) + w_inter * qn
    hh = num / jnp.maximum(jnp.abs(den), jnp.exp(-m_t))
    m_new = m_t[n_rows - 1:n_rows, :]
    f_last = f_col[n_rows - 1:n_rows, :]
    w_rows = jnp.exp(f_last - f_col + ig_col - m_new) * scale
    w_wide = jnp.broadcast_to(w_rows, (n_rows, ML_DH)).astype(BF16)
    decay = jnp.exp(f_last + m_prev - m_new)
    c_new = decay * c_prev + _dot_tn(v * w_wide, k)
    n_new = decay * n_prev + _dot_tn(w_wide[:, 0:LANES], k)[0:1, :]
    return hh, c_new, n_new, m_new


def _ml_out_gate(hh, xc, z, gn, sk):
    mu = jnp.mean(hh, axis=-1, keepdims=True)
    hc = hh - mu
    var = jnp.mean(hc * hc, axis=-1, keepdims=True)
    hn = hc * lax.rsqrt(var + EPS) * gn
    return ((hn + sk * xc) * _silu(z)).astype(BF16)


_ML_WEIGHTS = ["ml_conv_w", "ml_conv_b", "ml_wq", "ml_wk", "ml_wv", "ml_w_gates", "ml_b_gates"]
_ML_OUT_WEIGHTS = ["ml_norm", "ml_skip", "ml_w_out"]


def _ml_core_kernel(*refs, aliased):
    if aliased:
        refs = refs[1:]
    (x_ref, sh_ref, sc_ref, gt_ref, nw_ref, win_ref, cw_ref, cb_ref, wq_ref, wk_ref, wv_ref, wg_ref, bg_ref,
     gn_ref, sk_ref, wo_ref, xo_ref, co_ref, no_ref, mo_ref, cvo_ref,
     ext_scr, q_scr, k_scr, v_scr, xc_scr, xm_ref, z_ref) = refs
    tm = x_ref.shape[0]
    causal, eye = _causal(tm)
    tril = jnp.where(causal, 1.0, 0.0).astype(BF16)

    h = (_rms(x_ref[...], nw_ref[...]) * (1.0 + _mod_rows(sc_ref, tm)) + _mod_rows(sh_ref, tm)).astype(BF16)
    xm_ref[...] = jnp.dot(h, win_ref[:, :ML_INNER], preferred_element_type=F32)
    z_ref[...] = jnp.dot(h, win_ref[:, ML_INNER:], preferred_element_type=F32)

    @pl.when(pl.program_id(1) == 0)
    def _():
        ext_scr[:, 0:HALO, :] = jnp.zeros((1, HALO, ML_INNER), F32)
        co_ref[...] = jnp.zeros_like(co_ref)
        no_ref[...] = jnp.zeros_like(no_ref)
        mo_ref[...] = jnp.zeros_like(mo_ref)

    ext_scr[:, HALO:HALO + tm, :] = xm_ref[...].reshape(1, tm, ML_INNER)
    gl = _ml_qkv_gates(xm_ref, ext_scr, 1, cw_ref, cb_ref, wq_ref, wk_ref, wv_ref, wg_ref, bg_ref,
                       xc_scr, q_scr, k_scr, v_scr)
    cvo_ref[...] = ext_scr[:, CONV_FIRST + tm:HALO + tm, :].reshape(cvo_ref.shape)
    ext_scr[:, 0:HALO, :] = ext_scr[:, tm:tm + HALO, :]

    cum = _dot01(tril, gl)
    mix = None
    for h in range(ML_H):
        cs = slice(h * ML_DH, (h + 1) * ML_DH)
        hh, c_new, n_new, m_new = _mlstm_head(
            q_scr[:, cs], k_scr[:, cs], v_scr[:, cs], gl[:, h:h + 1], cum[:, ML_H + h:ML_H + h + 1],
            co_ref[h], no_ref[h:h + 1, :], mo_ref[0:1, h:h + 1], causal, eye)
        co_ref[h] = c_new
        no_ref[h:h + 1, :] = n_new
        mo_ref[0:1, h:h + 1] = m_new
        p = jnp.dot(_ml_out_gate(hh, xc_scr[:, cs], z_ref[:, cs], gn_ref[:, cs], sk_ref[:, cs]), wo_ref[cs, :],
                    preferred_element_type=F32)
        mix = p if mix is None else mix + p
    xo_ref[...] = x_ref[...] + _mod_rows(gt_ref, tm) * mix


def _ml_core(x, mod, l, o, w, tm, c_stack):
    g_n, t_n, _ = x.shape
    c_spec = pl.BlockSpec((None, None, ML_H, ML_DH, ML_DH), lambda g, i: (o, g, 0, 0, 0))
    names = ["ml_w_in"] + _ML_WEIGHTS + _ML_OUT_WEIGHTS
    in_specs = ([_tok_spec(tm, D_MODEL)] + [_mod_spec(mod, l, j, tm, t_n) for j in (3, 4, 5)]
                + [_layer_spec(w["mix_norm"], l)] + [_layer_spec(w[n], o) for n in names])
    args = [x, mod, mod, mod, w["mix_norm"]] + [w[n] for n in names]
    aliases = {}
    if c_stack is not None:
        aliases = {0: 1}
        in_specs.insert(0, pl.BlockSpec(memory_space=pl.ANY))
        args.insert(0, c_stack)
    return pl.pallas_call(
        functools.partial(_ml_core_kernel, aliased=c_stack is not None),
        grid=(g_n, t_n // tm),
        in_specs=in_specs,
        out_specs=[_tok_spec(tm, D_MODEL), c_spec,
                   pl.BlockSpec((None, ML_H, ML_DH), lambda g, i: (g, 0, 0)),
                   pl.BlockSpec((None, 1, ML_H), lambda g, i: (g, 0, 0)),
                   pl.BlockSpec((None, ML_CONV - 1, ML_INNER), lambda g, i: (g, 0, 0))],
        out_shape=[jax.ShapeDtypeStruct(x.shape, F32),
                   jax.ShapeDtypeStruct((DEPTH // 2, g_n, ML_H, ML_DH, ML_DH), F32),
                   jax.ShapeDtypeStruct((g_n, ML_H, ML_DH), F32),
                   jax.ShapeDtypeStruct((g_n, 1, ML_H), F32),
                   jax.ShapeDtypeStruct((g_n, ML_CONV - 1, ML_INNER), F32)],
        scratch_shapes=[pltpu.VMEM((1, HALO + tm, ML_INNER), F32)] + [pltpu.VMEM((tm, ML_INNER), BF16)] * 3
        + [pltpu.VMEM((tm, ML_INNER), F32)] * 3,
        input_output_aliases=aliases,
        compiler_params=_params("parallel", "arbitrary"),
        name="ml_core",
    )(*args)


def _ml_pre_kernel(xm_ref, c0_ref, cw_ref, cb_ref, wq_ref, wk_ref, wv_ref, wg_ref, bg_ref,
                   xc_ref, q_ref, k_ref, v_ref, gl_ref, co_ref, ext_scr, *, nb):
    tm = xm_ref.shape[0]
    ls = tm // nb
    ext_scr[:, CONV_FIRST:HALO, :] = c0_ref[...]
    ext_scr[:, HALO:HALO + ls, :] = xm_ref[...].reshape(nb, ls, ML_INNER)
    gl_ref[...] = _ml_qkv_gates(xm_ref, ext_scr, nb, cw_ref, cb_ref, wq_ref, wk_ref, wv_ref, wg_ref, bg_ref,
                                xc_ref, q_ref, k_ref, v_ref)
    co_ref[...] = ext_scr[:, CONV_FIRST + ls:HALO + ls, :]


def _ml_pre(xm, c0, o, w, nb):
    g_n, tm, _ = xm.shape
    c_spec = pl.BlockSpec((None,) + c0.shape[1:], lambda g, i: (g, 0, 0, 0))
    big = jax.ShapeDtypeStruct((g_n, tm, ML_INNER), F32)
    return pl.pallas_call(
        functools.partial(_ml_pre_kernel, nb=nb),
        grid=(g_n, 1),
        in_specs=[_tok_spec(tm, ML_INNER), c_spec] + [_layer_spec(w[n], o) for n in _ML_WEIGHTS],
        out_specs=[_tok_spec(tm, ML_INNER)] * 4 + [_tok_spec(tm, LANES), c_spec],
        out_shape=[big] * 4 + [jax.ShapeDtypeStruct((g_n, tm, LANES), F32), jax.ShapeDtypeStruct(c0.shape, F32)],
        scratch_shapes=[pltpu.VMEM((nb, HALO + tm // nb, ML_INNER), F32)],
        compiler_params=_params("parallel", "arbitrary"),
        name="ml_pre",
    )(xm, c0, *[w[n] for n in _ML_WEIGHTS])


def _ml_scan_kernel(*refs, aliased):
    if aliased:
        refs = refs[1:]
    (q_ref, k_ref, v_ref, gl_ref, c0_ref, n0_ref, m0_ref, hh_ref, co_ref, no_ref, mo_ref) = refs
    n_seq = c0_ref.shape[0]
    n_rows = q_ref.shape[0] // n_seq
    causal, eye = _causal(n_rows)
    tril = jnp.where(causal, 1.0, 0.0).astype(BF16)
    for s in range(n_seq):
        rows = slice(s * n_rows, (s + 1) * n_rows)
        gl = gl_ref[rows, :]
        cum = _dot01(tril, gl)
        for h in range(ML_H):
            cs = slice(h * ML_DH, (h + 1) * ML_DH)
            hh, c_new, n_new, m_new = _mlstm_head(
                q_ref[rows, cs], k_ref[rows, cs], v_ref[rows, cs], gl[:, h:h + 1], cum[:, ML_H + h:ML_H + h + 1],
                c0_ref[s, h], n0_ref[s, h:h + 1, :], m0_ref[s, 0:1, h:h + 1], causal, eye)
            hh_ref[rows, cs] = hh
            co_ref[s, h] = c_new
            no_ref[s, h:h + 1, :] = n_new
            mo_ref[s, 0:1, h:h + 1] = m_new


def _ml_scan(q, k, v, gl, state, o, c_stack, n_seq):
    g_n, tm, _ = q.shape
    b_n = g_n * n_seq
    c_spec = pl.BlockSpec((None, n_seq, ML_H, ML_DH, ML_DH), lambda g, i: (o, g, 0, 0, 0))
    n_spec = pl.BlockSpec((None, n_seq, ML_H, ML_DH), lambda g, i: (o, g, 0, 0))
    m_spec = pl.BlockSpec((None, n_seq, 1, ML_H), lambda g, i: (o, g, 0, 0))
    in_specs = [_tok_spec(tm, ML_INNER)] * 3 + [_tok_spec(tm, LANES), c_spec, n_spec, m_spec]
    args = [q, k, v, gl, *state]
    aliases = {}
    if c_stack is not None:
        aliases = {0: 1}
        in_specs.insert(0, pl.BlockSpec(memory_space=pl.ANY))
        args.insert(0, c_stack)
    return pl.pallas_call(
        functools.partial(_ml_scan_kernel, aliased=c_stack is not None),
        grid=(g_n, 1),
        in_specs=in_specs,
        out_specs=[_tok_spec(tm, ML_INNER), c_spec,
                   pl.BlockSpec((n_seq, ML_H, ML_DH), lambda g, i: (g, 0, 0)),
                   pl.BlockSpec((n_seq, 1, ML_H), lambda g, i: (g, 0, 0))],
        out_shape=[jax.ShapeDtypeStruct((g_n, tm, ML_INNER), F32),
                   jax.ShapeDtypeStruct((DEPTH // 2, b_n, ML_H, ML_DH, ML_DH), F32),
                   jax.ShapeDtypeStruct((b_n, ML_H, ML_DH), F32),
                   jax.ShapeDtypeStruct((b_n, 1, ML_H), F32)],
        input_output_aliases=aliases,
        compiler_params=_params("parallel", "arbitrary"),
        name="ml_scan",
    )(*args)


def _ml_post_kernel(x_ref, gt_ref, hh_ref, xc_ref, z_ref, gn_ref, sk_ref, wo_ref, xo_ref):
    tm = x_ref.shape[0]
    outs = []
    for h in range(ML_H):
        cs = slice(h * ML_DH, (h + 1) * ML_DH)
        outs.append(_ml_out_gate(hh_ref[:, cs], xc_ref[:, cs], z_ref[:, cs], gn_ref[:, cs], sk_ref[:, cs]))
    mix = jnp.dot(jnp.concatenate(outs, axis=1), wo_ref[...], preferred_element_type=F32)
    xo_ref[...] = x_ref[...] + _mod_rows(gt_ref, tm) * mix


def _ml_post(x, mod, l, o, w, hh, xc, z, tm):
    g_n, t_n, _ = x.shape
    return pl.pallas_call(
        _ml_post_kernel,
        grid=(g_n, t_n // tm),
        in_specs=[_tok_spec(tm, D_MODEL), _mod_spec(mod, l, 5, tm, t_n)] + [_tok_spec(tm, ML_INNER)] * 3
        + [_layer_spec(w[n], o) for n in _ML_OUT_WEIGHTS],
        out_specs=_tok_spec(tm, D_MODEL),
        out_shape=jax.ShapeDtypeStruct(x.shape, F32),
        compiler_params=_params("parallel", "parallel"),
        name="ml_post",
    )(x, mod, hh, xc, z, *[w[n] for n in _ML_OUT_WEIGHTS])


def _pad_heads(a):
    lead = a.shape[:-1]
    a = a.reshape(lead + (GLA_H, GLA_DK))
    a = jnp.pad(a, [(0, 0)] * len(lead) + [(0, 0), (0, LANES - GLA_DK)])
    return a.reshape(lead + (HP,))


def _block_diag_dense(wb):
    rows = wb.reshape(wb.shape[0], -1, MXU_DIM, ML_BLOCK)
    dense = jnp.tile(rows, (1, 1, 1, MXU_DIM // ML_BLOCK))
    r_blk = lax.broadcasted_iota(jnp.int32, (MXU_DIM, MXU_DIM), 0) // ML_BLOCK
    c_blk = lax.broadcasted_iota(jnp.int32, (MXU_DIM, MXU_DIM), 1) // ML_BLOCK
    return jnp.where(r_blk == c_blk, dense, 0.0).astype(BF16)


def _prepare_weights(p):
    w = {}
    w["ffn_norm"] = p["ffn_norm"].reshape(DEPTH, 2, 1, D_MODEL)
    for name in ("ffn_w_gate", "ffn_w_up", "ffn_w_down"):
        w[name] = p[name]
    w["mix_norm"] = p["mix_norm"].reshape(DEPTH, 1, D_MODEL)
    w_in = p["ab_w_in"]
    qk = GLA_H * GLA_DK
    vw = GLA_H * GLA_DV
    o_k, o_v, o_g, o_a = qk, 2 * qk, 2 * qk + vw, 2 * qk + 2 * vw
    o_u = o_a + GLA_LOWRANK
    o_vb = o_u + GMLP_W
    w_a = jnp.pad(w_in[:, :, o_a:o_u], ((0, 0), (0, 0), (0, LANES - GLA_LOWRANK)))
    w["ab_w_in"] = jnp.concatenate(
        [_pad_heads(w_in[:, :, :o_k]), _pad_heads(w_in[:, :, o_k:o_v]), w_in[:, :, o_v:o_g], w_in[:, :, o_g:o_a],
         w_a, w_in[:, :, o_u:o_vb], w_in[:, :, o_vb:]], axis=-1).astype(BF16)
    w["gla_w_a2"] = jnp.pad(_pad_heads(p["gla_w_a2"]), ((0, 0), (0, LANES - GLA_LOWRANK), (0, 0))).astype(BF16)
    w["gla_b_a"] = _pad_heads(p["gla_b_a"])[:, None, :]
    w["gla_norm"] = p["gla_norm"][:, None, :]
    w["gmlp_norm"] = p["gmlp_norm"][:, None, :]
    w["ab_w_out"] = p["ab_w_out"].astype(BF16)
    w["ml_w_in"] = p["ml_w_in"].astype(BF16)
    w["ml_conv_w"] = p["ml_conv_w"]
    w["ml_conv_b"] = p["ml_conv_b"][:, None, :]
    w["ml_wq"] = _block_diag_dense(p["ml_wq"])
    w["ml_wk"] = _block_diag_dense(p["ml_wk"])
    w["ml_wv"] = _block_diag_dense(p["ml_wv"])
    w["ml_w_gates"] = jnp.pad(p["ml_w_gates"], ((0, 0), (0, 0), (0, LANES - 2 * ML_H))).astype(BF16)
    w["ml_b_gates"] = jnp.pad(p["ml_b_gates"], ((0, 0), (0, LANES - 2 * ML_H)))[:, None, :]
    w["ml_norm"] = p["ml_norm"][:, None, :]
    w["ml_skip"] = p["ml_skip"][:, None, :]
    w["ml_w_out"] = p["ml_w_out"].astype(BF16)
    w["final_norm"] = p["final_norm"][None, :]
    return w


def _spatial_weights(ws, bs, seq_len):
    if seq_len % GMLP_CHUNK == 0:
        length = GMLP_CHUNK
    else:
        length = seq_len
    reps = GMLP_CHUNK // length
    wt = jnp.tril(ws[:, :, :length, :length])
    eye = jnp.eye(reps, dtype=ws.dtype)
    wt = jnp.einsum("egts,ab->egatbs", wt, eye).reshape(ws.shape[0], GMLP_G, GMLP_CHUNK, GMLP_CHUNK)
    bt = jnp.tile(bs[:, :, :length], (1, 1, reps))
    bt = jnp.broadcast_to(bt[:, :, :, None], (ws.shape[0], GMLP_G, GMLP_CHUNK, GMLP_DG))
    return wt.astype(BF16), bt


def _ffn_step(x, mod, l, s, w, ffn_bf16, fmod):
    g_n, t_n, _ = x.shape
    tm = min(FFN_ROW_TILE, t_n)
    names = ("ffn_w_gate", "ffn_w_up", "ffn_w_down")
    if (l, s) not in ffn_bf16:
        ffn_bf16[(l, s)] = tuple(w[n][l, s].astype(BF16) for n in names)
    nxt = (l, 1) if s == 0 else (l + 1, 0)
    steps = g_n * (t_n // tm)
    slab_ok = all(w[n].shape[2] % (steps * 2 * SUBLANES) == 0 for n in names)
    cast = nxt if (nxt[0] < DEPTH and nxt not in ffn_bf16 and slab_ok) else None
    y, converted = _ffn(x, mod, l, s, w, ffn_bf16[(l, s)], tm, fmod, cast)
    if cast is not None:
        ffn_bf16[nxt] = converted
    return y


def _trunk(x, mod, fmod, w, ws, bs, states, ffn_bf16, *, tm, mix_tm, gla_chunk, scan_tm, seq_rows):
    fresh = states is None
    g_n, t_n, _ = x.shape
    tm = min(tm, t_n)
    v_out, n_out, m_out, conv_out = [], [], [], []
    s_stack = c_stack = None
    for l in range(DEPTH):
        x = _ffn_step(x, mod, l, 0, w, ffn_bf16, None)
        if l % 2 == 0:
            e = l // 2
            proj = _ab_in(x, mod, l, e, w, tm)
            if fresh:
                x, s_stack = _ab_mix(x, mod, l, e, w, proj, None, mix_tm, gla_chunk, ws, bs, s_stack)
            else:
                v_out.append(proj[6])
                nseq = mix_tm // seq_rows
                view = lambda a: a.reshape(-1, mix_tm, a.shape[-1])
                s_all = states["gla_S"]
                s0 = s_all.reshape(s_all.shape[0], -1, nseq, GLA_H, GLA_DK, GLA_DV)
                mod_v = mod.reshape(mod.shape[0], mod.shape[1], -1, nseq, D_MODEL)
                xv, s_stack = _ab_mix(view(x), mod_v, l, e, w, [view(a) for a in proj], s0, mix_tm, gla_chunk, ws, bs,
                                      s_stack)
                x = xv.reshape(g_n, t_n, D_MODEL)
        else:
            o = l // 2
            if fresh:
                x, c_stack, n_new, m_new, conv_new = _ml_core(x, mod, l, o, w, scan_tm, c_stack)
                m_new = m_new.reshape(g_n, ML_H)
            else:
                xm, z = _ml_in(x, mod, l, o, w, tm)
                nseq = LANES // seq_rows
                rows = nseq * seq_rows
                view = lambda a, r: a.reshape(-1, r, a.shape[-1])
                c0 = states["ml_conv"][o].reshape(-1, nseq, ML_CONV - 1, ML_INNER)
                xc, q, k, v, gl, conv_new = _ml_pre(view(xm, rows), c0, o, w, nseq)
                conv_new = conv_new.reshape(-1, ML_CONV - 1, ML_INNER)
                st = (states["ml_C"], states["ml_n"], states["ml_m"][:, :, None, :])
                srows = SCAN_SEQS * seq_rows
                hh, c_stack, n_new, m_new = _ml_scan(view(q, srows), view(k, srows), view(v, srows),
                                                     view(gl, srows), st, o, c_stack, SCAN_SEQS)
                m_new = m_new.reshape(-1, ML_H)
                xc = xc.reshape(g_n, t_n, ML_INNER)
                hh = hh.reshape(g_n, t_n, ML_INNER)
                x = _ml_post(x, mod, l, o, w, hh, xc, z, min(tm, 2 * ML_CHUNK))
            n_out.append(n_new)
            m_out.append(m_new)
            conv_out.append(conv_new)
        x = _ffn_step(x, mod, l, 1, w, ffn_bf16, fmod if l == DEPTH - 1 else None)
    y = x
    s_stack = s_stack.reshape(s_stack.shape[0], -1, GLA_H, GLA_DK, GLA_DV)
    v_stack = jnp.stack(v_out) if v_out else None
    return (y, s_stack, v_stack, c_stack, jnp.stack(n_out), jnp.stack(m_out),
            jnp.stack(conv_out))


def kernel(x_prompt, x_sample, c_prompt, c_sample, state_gla_S, state_mlstm_C, state_mlstm_n, state_mlstm_m, state_mlstm_conv, ada_w, ada_b, ffn_norm, ffn_w_gate, ffn_w_up, ffn_w_down, mix_norm, ab_w_in, gla_w_a2, gla_b_a, gla_norm, gmlp_norm, gmlp_ws, gmlp_bs, ab_w_out, ml_w_in, ml_conv_w, ml_conv_b, ml_wq, ml_wk, ml_wv, ml_w_gates, ml_b_gates, ml_norm, ml_skip, ml_w_out, final_norm, final_ada_w, final_ada_b):
    p = dict(ffn_norm=ffn_norm, ffn_w_gate=ffn_w_gate, ffn_w_up=ffn_w_up, ffn_w_down=ffn_w_down, mix_norm=mix_norm,
             ab_w_in=ab_w_in, gla_w_a2=gla_w_a2, gla_b_a=gla_b_a, gla_norm=gla_norm, gmlp_norm=gmlp_norm,
             ab_w_out=ab_w_out, ml_w_in=ml_w_in, ml_conv_w=ml_conv_w, ml_conv_b=ml_conv_b, ml_wq=ml_wq, ml_wk=ml_wk,
             ml_wv=ml_wv, ml_w_gates=ml_w_gates, ml_b_gates=ml_b_gates, ml_norm=ml_norm, ml_skip=ml_skip,
             ml_w_out=ml_w_out, final_norm=final_norm)
    w = _prepare_weights(p)
    n_p, t_p, _ = x_prompt.shape
    n_s, t_s, _ = x_sample.shape

    c_all = jnp.concatenate([c_prompt, c_sample], axis=0)
    mod = _ada(c_all, ada_w, ada_b)
    fmod = _ada(c_all, final_ada_w[None], final_ada_b[None])

    def split_mod(m):
        return m[:, :, :n_p, None, :], m[:, :, None, n_p:, :]

    mod_p, mod_s = split_mod(mod)
    fmod_p, fmod_s = split_mod(fmod)

    ws_p, bs_p = _spatial_weights(gmlp_ws, gmlp_bs, t_p)
    ws_s, bs_s = _spatial_weights(gmlp_ws, gmlp_bs, t_s)

    ffn_bf16 = {}
    y_p, s_p, _, c_p, n_p_, m_p, cv_p = _trunk(
        x_prompt, mod_p, fmod_p, w, ws_p, bs_p, None, ffn_bf16, tm=ROW_TILE, mix_tm=ROW_TILE,
        gla_chunk=_chunk_len(t_p, GLA_TILE_CHUNK, GLA_CHUNK), scan_tm=_chunk_len(t_p, ML_TILE_CHUNK, ML_CHUNK),
        seq_rows=t_p)

    states = dict(gla_S=state_gla_S, ml_C=state_mlstm_C, ml_n=state_mlstm_n, ml_m=state_mlstm_m,
                  ml_conv=state_mlstm_conv)
    xs = x_sample.reshape(1, n_s * t_s, D_MODEL)
    y_s, s_s, v_s, c_s, n_s_, m_s, cv_s = _trunk(
        xs, mod_s, fmod_s, w, ws_s, bs_s, states, ffn_bf16, tm=ROW_TILE, mix_tm=GMLP_CHUNK, gla_chunk=t_s, scan_tm=t_s,
        seq_rows=t_s)
    y_s = y_s.reshape(n_s, t_s, D_MODEL)
    v_s = v_s.reshape(-1, n_s, t_s, GMLP_W)
    return (y_p, y_s, s_p, s_s, v_s, c_p, c_s, n_p_, n_s_, m_p, m_s, cv_p, cv_s)
```

```python
import functools

import jax
import jax.numpy as jnp
from jax import lax
from jax.experimental import pallas as pl
from jax.experimental.pallas import tpu as pltpu

F32 = jnp.float32
BF16 = jnp.bfloat16
EPS = 1e-6

D_MODEL = 1024
DEPTH = 4
D_FF = 2816
GLA_H = 4
GLA_DK = 64
GLA_DV = 128
GLA_LOWRANK = 16
GLA_TAU = 16.0
GLA_CHUNK = 64
GMLP_G = 4
GMLP_DG = 128
GMLP_W = GMLP_G * GMLP_DG
GMLP_CHUNK = 128
ML_INNER = 2 * D_MODEL
ML_H = 4
ML_DH = ML_INNER // ML_H
ML_CONV = 4
ML_BLOCK = 4
ML_CHUNK = 128

LANES = 128
SUBLANES = 8
MXU_DIM = 256
VMEM_LIMIT_BYTES = 56 * 1024 * 1024

ROW_TILE = 512
GLA_TILE_CHUNK = 128
ML_TILE_CHUNK = 256
FFN_ROW_TILE = 1024
SCAN_SEQS = 2


def _chunk_len(t, preferred, nominal):
    for c in (preferred, nominal):
        if t % c == 0:
            return c
    return t


HP = GLA_H * LANES
AB_Q, AB_K, AB_V, AB_G = 0, HP, 2 * HP, 3 * HP
AB_A = 4 * HP
AB_U = AB_A + LANES
AB_VB = AB_U + GMLP_W
AB_COLS = AB_VB + GMLP_W


def _dot(a, b):
    return jnp.dot(a.astype(BF16), b.astype(BF16), preferred_element_type=F32)


def _dot_nt(a, b):
    return lax.dot_general(a.astype(BF16), b.astype(BF16), (((1,), (1,)), ((), ())), preferred_element_type=F32)


def _dot_tn(a, b):
    return lax.dot_general(a.astype(BF16), b.astype(BF16), (((0,), (0,)), ((), ())), preferred_element_type=F32)


def _dot01(sel, x):
    hi = x.astype(BF16)
    r1 = x - hi.astype(F32)
    mid = r1.astype(BF16)
    lo = (r1 - mid.astype(F32)).astype(BF16)
    return (jnp.dot(sel, hi, preferred_element_type=F32) + jnp.dot(sel, mid, preferred_element_type=F32)
            + jnp.dot(sel, lo, preferred_element_type=F32))


def _rms(x, g):
    return x * lax.rsqrt(jnp.mean(x * x, axis=-1, keepdims=True) + EPS) * g


def _silu(x):
    return x * jax.nn.sigmoid(x)


def _log_sigmoid(x):
    return jnp.minimum(x, 0.0) - jnp.log1p(jnp.exp(-jnp.abs(x)))


def _causal(n):
    row = lax.broadcasted_iota(jnp.int32, (n, n), 0)
    col = lax.broadcasted_iota(jnp.int32, (n, n), 1)
    return row >= col, row == col


def _mod_rows(ref, tm):
    m = ref[...]
    r = m.shape[0]
    if r == 1 or r == tm:
        return m
    rep = tm // r
    if rep == SUBLANES:
        return jnp.broadcast_to(m[:, None, :], (r, rep, m.shape[1])).reshape(tm, m.shape[1])
    row = lax.broadcasted_iota(jnp.int32, (tm, r), 0)
    lo = lax.broadcasted_iota(jnp.int32, (tm, r), 1) * rep
    sel = jnp.where(row >= lo, jnp.where(row < lo + rep, 1.0, 0.0), 0.0).astype(BF16)
    return _dot01(sel, m)


def _col_to_row(col, eye):
    return jnp.sum(jnp.where(eye, col, 0.0), axis=0, keepdims=True)


def _tok_spec(tm, w):
    return pl.BlockSpec((None, tm, w), lambda g, i: (g, i, 0))


def _mod_spec(mod, l, j, tm, t_n):
    t_mod = mod.shape[3]
    if t_mod == 1:
        return pl.BlockSpec((None, None, None, 1, D_MODEL), lambda g, i: (l, j, g, 0, 0))
    return pl.BlockSpec((None, None, None, t_mod * tm // t_n, D_MODEL), lambda g, i: (l, j, g, i, 0))


def _layer_spec(a, *lead):
    n = len(lead)
    shape = (None,) * n + tuple(a.shape[n:])
    zeros = (0,) * (a.ndim - n)
    return pl.BlockSpec(shape, lambda g, i: tuple(lead) + zeros, pipeline_mode=pl.Buffered(1))


def _params(*sem):
    return pltpu.CompilerParams(dimension_semantics=sem, vmem_limit_bytes=VMEM_LIMIT_BYTES)


def _ada_kernel(c_ref, w_ref, b_ref, o_ref):
    cs = _silu(c_ref[...])
    o_ref[...] = _dot(cs, w_ref[...]) + b_ref[...]


def _ada(c, w, b):
    n_l, _, width = w.shape
    n_j = width // D_MODEL
    r = c.shape[0]
    return pl.pallas_call(
        _ada_kernel,
        grid=(n_l, n_j),
        in_specs=[pl.BlockSpec((r, D_MODEL), lambda l, j: (0, 0)),
                  pl.BlockSpec((None, D_MODEL, D_MODEL), lambda l, j: (l, 0, j)),
                  pl.BlockSpec((None, None, 1, D_MODEL), lambda l, j: (l, j, 0, 0))],
        out_specs=pl.BlockSpec((None, None, r, D_MODEL), lambda l, j: (l, j, 0, 0)),
        out_shape=jax.ShapeDtypeStruct((n_l, n_j, r, D_MODEL), F32),
        compiler_params=_params("arbitrary", "arbitrary"),
        name="ada",
    )(c, w, b.reshape(n_l, n_j, 1, D_MODEL))


FFN_CHUNK = 1 * MXU_DIM
FFN_DOWN_GROUP = 11


def _ffn_kernel(*refs, final, cast):
    x_ref, sh_ref, sc_ref, gt_ref, nw_ref, wg_ref, wu_ref, wd_ref = refs[:8]
    n_in = 8 + (3 if final else 0) + (3 if cast else 0)
    o_ref = refs[n_in]
    x = x_ref[...]
    tm = x.shape[0]
    h = (_rms(x, nw_ref[...]) * (1.0 + _mod_rows(sc_ref, tm)) + _mod_rows(sh_ref, tm)).astype(BF16)
    acc = None
    pieces = []
    starts = list(range(0, D_FF, FFN_CHUNK))
    for n, f0 in enumerate(starts):
        f1 = min(f0 + FFN_CHUNK, D_FF)
        g = jnp.dot(h, wg_ref[:, f0:f1], preferred_element_type=F32)
        u = jnp.dot(h, wu_ref[:, f0:f1], preferred_element_type=F32)
        pieces.append((_silu(g) * u).astype(BF16))
        if len(pieces) == FFN_DOWN_GROUP or n == len(starts) - 1:
            lo = f1 - sum(p.shape[1] for p in pieces)
            a = pieces[0] if len(pieces) == 1 else jnp.concatenate(pieces, axis=1)
            p = jnp.dot(a, wd_ref[lo:f1, :], preferred_element_type=F32)
            acc = p if acc is None else acc + p
            pieces = []
    y = x + (0.5 * _mod_rows(gt_ref, tm)) * acc
    if final:
        fsh_ref, fsc_ref, fnw_ref = refs[8:11]
        y = _rms(y, fnw_ref[...]) * (1.0 + _mod_rows(fsc_ref, tm)) + _mod_rows(fsh_ref, tm)
    o_ref[...] = y
    if cast:
        for src, dst in zip(refs[n_in - 3:n_in], refs[n_in + 1:n_in + 4]):
            dst[...] = src[...].astype(BF16)


def _ffn(x, mod, l, s, w, ffn_w, tm, fmod=None, cast=None):
    g_n, t_n, _ = x.shape
    n_i = t_n // tm
    j0 = 6 * s
    whole = lambda a: pl.BlockSpec(a.shape, lambda g, i: (0, 0), pipeline_mode=pl.Buffered(1))
    in_specs = [_tok_spec(tm, D_MODEL), _mod_spec(mod, l, j0, tm, t_n), _mod_spec(mod, l, j0 + 1, tm, t_n),
                _mod_spec(mod, l, j0 + 2, tm, t_n), _layer_spec(w["ffn_norm"], l, s)] + [whole(a) for a in ffn_w]
    args = [x, mod, mod, mod, w["ffn_norm"], *ffn_w]
    out_specs = [_tok_spec(tm, D_MODEL)]
    out_shape = [jax.ShapeDtypeStruct(x.shape, F32)]
    if fmod is not None:
        in_specs += [_mod_spec(fmod, 0, 0, tm, t_n), _mod_spec(fmod, 0, 1, tm, t_n),
                     pl.BlockSpec((1, D_MODEL), lambda g, i: (0, 0))]
        args += [fmod, fmod, w["final_norm"]]
    if cast is not None:
        l2, s2 = cast
        steps = g_n * n_i
        for name in ("ffn_w_gate", "ffn_w_up", "ffn_w_down"):
            src = w[name]
            rows, cols = src.shape[2:]
            in_specs.append(pl.BlockSpec((None, None, rows // steps, cols), lambda g, i: (l2, s2, g * n_i + i, 0)))
            args.append(src)
            out_specs.append(pl.BlockSpec((rows // steps, cols), lambda g, i: (g * n_i + i, 0)))
            out_shape.append(jax.ShapeDtypeStruct((rows, cols), BF16))
    out = pl.pallas_call(
        functools.partial(_ffn_kernel, final=fmod is not None, cast=cast is not None),
        grid=(g_n, n_i),
        in_specs=in_specs,
        out_specs=out_specs,
        out_shape=out_shape,
        compiler_params=_params("parallel", "parallel"),
        name="ffn",
    )(*args)
    return out[0], tuple(out[1:])


def _ab_project(h, w_ref, wa2_ref, ba_ref, vn_ref, q_ref, k_ref, v_ref, g_ref, la_ref, u_ref, vb_ref):
    def seg(c0, width):
        return jnp.dot(h, w_ref[:, c0:c0 + width], preferred_element_type=F32)

    xa = _dot(seg(AB_A, LANES), wa2_ref[...]) + ba_ref[...]
    la_ref[...] = _log_sigmoid(xa) * (1.0 / GLA_TAU)
    vb = seg(AB_VB, GMLP_W)
    for gi in range(GMLP_G):
        cs = slice(gi * GMLP_DG, (gi + 1) * GMLP_DG)
        vb_ref[:, cs] = _rms(vb[:, cs], vn_ref[...])
    q_ref[...] = seg(AB_Q, HP) * GLA_DK ** -0.5
    k_ref[...] = seg(AB_K, HP)
    v_ref[...] = seg(AB_V, HP)
    g_ref[...] = seg(AB_G, HP)
    u_ref[...] = seg(AB_U, GMLP_W)


def _ab_in_kernel(x_ref, sh_ref, sc_ref, nw_ref, w_ref, wa2_ref, ba_ref, vn_ref, *out_refs):
    tm = x_ref.shape[0]
    h = (_rms(x_ref[...], nw_ref[...]) * (1.0 + _mod_rows(sc_ref, tm)) + _mod_rows(sh_ref, tm)).astype(BF16)
    _ab_project(h, w_ref, wa2_ref, ba_ref, vn_ref, *out_refs)


_AB_IN_WEIGHTS = ["ab_w_in", "gla_w_a2", "gla_b_a", "gmlp_norm"]


def _ab_in(x, mod, l, e, w, tm):
    g_n, t_n, _ = x.shape
    out = jax.ShapeDtypeStruct((g_n, t_n, HP), F32)
    return pl.pallas_call(
        _ab_in_kernel,
        grid=(g_n, t_n // tm),
        in_specs=[_tok_spec(tm, D_MODEL), _mod_spec(mod, l, 3, tm, t_n), _mod_spec(mod, l, 4, tm, t_n),
                  _layer_spec(w["mix_norm"], l), _layer_spec(w["ab_w_in"], e), _layer_spec(w["gla_w_a2"], e),
                  _layer_spec(w["gla_b_a"], e), _layer_spec(w["gmlp_norm"], e)],
        out_specs=[_tok_spec(tm, HP)] * 7,
        out_shape=[out] * 7,
        compiler_params=_params("parallel", "parallel"),
        name="ab_in",
    )(x, mod, mod, w["mix_norm"], w["ab_w_in"], w["gla_w_a2"], w["gla_b_a"], w["gmlp_norm"])


def _gla_chunk_intra(q, k, v, la, tril, causal, seq):
    n, width = q.shape
    b = _dot01(tril, la)
    if seq == n:
        bm = b[n // 2 - 1:n // 2, :]
        bl = b[n - 1:n, :]
        qd = (q * jnp.exp(b)).astype(BF16)
        qe = (q * jnp.exp(b - bm)).astype(BF16)
        ke = (k * jnp.exp(bm - b)).astype(BF16)
    else:
        assert seq == SUBLANES
        last = b.reshape(n // seq, seq, width)[:, seq - 1:seq, :]
        bl = jnp.broadcast_to(last, (n // seq, seq, width)).reshape(n, width)
        qd = q * jnp.exp(b)
        qe = qd.astype(BF16)
        ke = (k * jnp.exp(-b)).astype(BF16)
    kd = k * jnp.exp(bl - b)
    vb = v.astype(BF16)
    o_intra = []
    for h in range(GLA_H):
        cs = slice(h * LANES, (h + 1) * LANES)
        a = jnp.where(causal, _dot_nt(qe[:, cs], ke[:, cs]), 0.0)
        o_intra.append(_dot(a, vb[:, cs]))
    if seq == n:
        kd = kd.astype(BF16)
    return o_intra, qd, kd, vb, jnp.exp(bl)


def _ab_mix_kernel(*refs, chunk, carry, aliased):
    if aliased:
        refs = refs[1:]
    if carry:
        (x_ref, gt_ref, q_ref, k_ref, v_ref, g_ref, la_ref, u_ref, vb_ref, gn_ref, ws_ref, bs_ref, wo_ref,
         xo_ref, so_ref, o_scr, st_scr) = refs
    else:
        (x_ref, gt_ref, q_ref, k_ref, v_ref, g_ref, la_ref, u_ref, vb_ref, gn_ref, ws_ref, bs_ref, wo_ref, s0_ref,
         xo_ref, so_ref, o_scr, qd_scr, kd_scr, dec_scr) = refs
    tm = q_ref.shape[0]

    if carry:
        causal, _ = _causal(chunk)
        tril = jnp.where(causal, 1.0, 0.0).astype(BF16)

        @pl.when(pl.program_id(1) == 0)
        def _():
            st_scr[...] = jnp.zeros_like(st_scr)

        n_chunks = tm // chunk
        intra = [_gla_chunk_intra(q_ref[c * chunk:(c + 1) * chunk, :], k_ref[c * chunk:(c + 1) * chunk, :],
                                  v_ref[c * chunk:(c + 1) * chunk, :], la_ref[c * chunk:(c + 1) * chunk, :],
                                  tril, causal, chunk) for c in range(n_chunks)]
        for h in range(GLA_H):
            cs = slice(h * LANES, (h + 1) * LANES)
            st = st_scr[h]
            for c in range(n_chunks):
                o_intra, qd, kd, vb, dec = intra[c]
                o_scr[c * chunk:(c + 1) * chunk, cs] = o_intra[h] + _dot_nt(qd[:, cs], st)
                st = st * dec[:, cs] + _dot_tn(vb[:, cs], kd[:, cs])
            st_scr[h] = st

        @pl.when(pl.program_id(1) == pl.num_programs(1) - 1)
        def _():
            for h in range(GLA_H):
                so_ref[h] = st_scr[h].T[:GLA_DK, :]
    else:
        row = lax.broadcasted_iota(jnp.int32, (tm, tm), 0)
        col = lax.broadcasted_iota(jnp.int32, (tm, tm), 1)
        causal = jnp.logical_and(row // chunk == col // chunk, row >= col)
        tril = jnp.where(causal, 1.0, 0.0).astype(BF16)
        o_intra, qd, kd, _, dec = _gla_chunk_intra(q_ref[...], k_ref[...], v_ref[...], la_ref[...], tril, causal, chunk)
        for h in range(GLA_H):
            o_scr[:, h * LANES:(h + 1) * LANES] = o_intra[h]
        qd_scr[...] = qd
        kd_scr[...] = kd
        dec_scr[...] = dec
        key_pad = jnp.zeros((LANES - GLA_DK, GLA_DV), F32)

        def seq_body(c, _):
            r0 = pl.multiple_of(c * chunk, chunk)
            rows = pl.ds(r0, chunk)
            for h in range(GLA_H):
                cs = slice(h * LANES, (h + 1) * LANES)
                st = jnp.concatenate([s0_ref[c, h], key_pad], axis=0).T
                o_scr[rows, cs] = o_scr[rows, cs] + _dot_nt(qd_scr[rows, cs], st)
                st_new = st * dec_scr[pl.ds(r0, 1), cs] + _dot_tn(v_ref[rows, cs], kd_scr[rows, cs])
                so_ref[c, h] = st_new.T[:GLA_DK, :]
            return 0

        lax.fori_loop(0, tm // chunk, seq_body, 0, unroll=4)

    pieces = []
    for h in range(GLA_H):
        cs = slice(h * GLA_DV, (h + 1) * GLA_DV)
        pieces.append((_rms(o_scr[:, cs], gn_ref[...]) * _silu(g_ref[:, cs])).astype(BF16))
    for gi in range(GMLP_G):
        cs = slice(gi * GMLP_DG, (gi + 1) * GMLP_DG)
        zs = []
        for r0 in range(0, tm, GMLP_CHUNK):
            zs.append(_dot(ws_ref[gi], vb_ref[r0:r0 + GMLP_CHUNK, cs]) + bs_ref[gi])
        z = zs[0] if len(zs) == 1 else jnp.concatenate(zs, axis=0)
        pieces.append((u_ref[:, cs] * z).astype(BF16))
    mix = jnp.dot(jnp.concatenate(pieces, axis=1), wo_ref[...], preferred_element_type=F32)
    xo_ref[...] = x_ref[...] + _mod_rows(gt_ref, tm) * mix


def _ab_mix(x, mod, l, e, w, proj, s0, tm, chunk, ws, bs, s_stack):
    g_n, t_n, _ = x.shape
    carry = s0 is None
    in_specs = ([_tok_spec(tm, D_MODEL), _mod_spec(mod, l, 5, tm, t_n)] + [_tok_spec(tm, HP)] * 7
                + [_layer_spec(w["gla_norm"], e), _layer_spec(ws, e), _layer_spec(bs, e), _layer_spec(w["ab_w_out"], e)])
    args = [x, mod, *proj, w["gla_norm"], ws, bs, w["ab_w_out"]]
    scratch = [pltpu.VMEM((tm, HP), F32)]
    if carry:
        s_shape = (DEPTH - DEPTH // 2, g_n, GLA_H, GLA_DK, GLA_DV)
        s_spec = pl.BlockSpec((None, None, GLA_H, GLA_DK, GLA_DV), lambda g, i: (e, g, 0, 0, 0))
        scratch.append(pltpu.VMEM((GLA_H, LANES, LANES), F32))
    else:
        s_shape = s0.shape
        s_spec = pl.BlockSpec((None, None) + s0.shape[2:], lambda g, i: (e, g, 0, 0, 0, 0))
        in_specs.append(s_spec)
        args.append(s0)
        scratch += [pltpu.VMEM((tm, HP), F32)] * 3
    aliases = {}
    if s_stack is not None:
        aliases = {0: 1}
        in_specs.insert(0, pl.BlockSpec(memory_space=pl.ANY))
        args.insert(0, s_stack)
    return pl.pallas_call(
        functools.partial(_ab_mix_kernel, chunk=chunk, carry=carry, aliased=s_stack is not None),
        grid=(g_n, t_n // tm),
        in_specs=in_specs,
        out_specs=[_tok_spec(tm, D_MODEL), s_spec],
        out_shape=[jax.ShapeDtypeStruct(x.shape, F32), jax.ShapeDtypeStruct(s_shape, F32)],
        scratch_shapes=scratch,
        input_output_aliases=aliases,
        compiler_params=_params("parallel", "arbitrary"),
        name="ab_mix",
    )(*args)


def _ml_in_kernel(x_ref, sh_ref, sc_ref, nw_ref, w_ref, xm_ref, z_ref):
    tm = x_ref.shape[0]
    h = (_rms(x_ref[...], nw_ref[...]) * (1.0 + _mod_rows(sc_ref, tm)) + _mod_rows(sh_ref, tm)).astype(BF16)
    xm_ref[...] = jnp.dot(h, w_ref[:, :ML_INNER], preferred_element_type=F32)
    z_ref[...] = jnp.dot(h, w_ref[:, ML_INNER:], preferred_element_type=F32)


def _ml_in(x, mod, l, o, w, tm):
    g_n, t_n, _ = x.shape
    out = jax.ShapeDtypeStruct((g_n, t_n, ML_INNER), F32)
    return pl.pallas_call(
        _ml_in_kernel,
        grid=(g_n, t_n // tm),
        in_specs=[_tok_spec(tm, D_MODEL), _mod_spec(mod, l, 3, tm, t_n), _mod_spec(mod, l, 4, tm, t_n),
                  _layer_spec(w["mix_norm"], l), _layer_spec(w["ml_w_in"], o)],
        out_specs=[_tok_spec(tm, ML_INNER)] * 2,
        out_shape=[out] * 2,
        compiler_params=_params("parallel", "parallel"),
        name="ml_in",
    )(x, mod, mod, w["mix_norm"], w["ml_w_in"])


HALO = SUBLANES
CONV_FIRST = HALO - (ML_CONV - 1)


def _ml_qkv_gates(xm_ref, ext_scr, nb, cw_ref, cb_ref, wq_ref, wk_ref, wv_ref, wg_ref, bg_ref,
                  xc_ref, q_ref, k_ref, v_ref):
    tm = xm_ref.shape[0]
    ls = tm // nb
    gates = bg_ref[...]
    for blk in range(ML_INNER // MXU_DIM):
        cs = slice(blk * MXU_DIM, (blk + 1) * MXU_DIM)
        pre = cb_ref[:, cs]
        for j in range(ML_CONV):
            pre = pre + ext_scr[:, CONV_FIRST + j:CONV_FIRST + j + ls, cs].reshape(tm, MXU_DIM) * cw_ref[j:j + 1, cs]
        xc = _silu(pre)
        xc_ref[:, cs] = xc
        xcb = xc.astype(BF16)
        q = jnp.dot(xcb, wq_ref[blk], preferred_element_type=F32).astype(BF16)
        k = jnp.dot(xcb, wk_ref[blk], preferred_element_type=F32).astype(BF16)
        v = jnp.dot(xm_ref[:, cs].astype(BF16), wv_ref[blk], preferred_element_type=F32).astype(BF16)
        gates = (gates + jnp.dot(q, wg_ref[cs, :], preferred_element_type=F32)
                 + jnp.dot(k, wg_ref[ML_INNER + blk * MXU_DIM:ML_INNER + (blk + 1) * MXU_DIM, :],
                           preferred_element_type=F32)
                 + jnp.dot(v, wg_ref[2 * ML_INNER + blk * MXU_DIM:2 * ML_INNER + (blk + 1) * MXU_DIM, :],
                           preferred_element_type=F32))
        q_ref[:, cs] = q.astype(q_ref.dtype)
        k_ref[:, cs] = k.astype(k_ref.dtype)
        v_ref[:, cs] = v.astype(v_ref.dtype)
    lane = lax.broadcasted_iota(jnp.int32, gates.shape, 1)
    return jnp.where(lane < ML_H, gates, _log_sigmoid(gates))


def _mlstm_head(q, k, v, ig_col, f_col, c_prev, n_prev, m_prev, causal, eye):
    q, k, v = q.astype(BF16), k.astype(BF16), v.astype(BF16)
    n_rows = q.shape[0]
    scale = ML_DH ** -0.5
    ig_row = _col_to_row(ig_col, eye)
    f_row = _col_to_row(f_col, eye)
    log_d = jnp.where(causal, f_col - f_row + ig_row, -jnp.inf)
    log_inter = f_col + m_prev
    m_t = jnp.maximum(log_inter, jnp.max(log_d, axis=-1, keepdims=True))
    d = jnp.exp(log_d - m_t) * scale
    w_inter = jnp.exp(log_inter - m_t)
    s = _dot_nt(q, k) * d
    qn = _dot_nt(q, jnp.broadcast_to(n_prev, (SUBLANES, ML_DH)))[:, 0:1]
    num = _dot(s, v) + w_inter * _dot_nt(q, c_prev)
    den = jnp.sum(s, axis=-1, keepdims=True) + w_inter * qn
    hh = num / jnp.maximum(jnp.abs(den), jnp.exp(-m_t))
    m_new = m_t[n_rows - 1:n_rows, :]
    f_last = f_col[n_rows - 1:n_rows, :]
    w_rows = jnp.exp(f_last - f_col + ig_col - m_new) * scale
    w_wide = jnp.broadcast_to(w_rows, (n_rows, ML_DH)).astype(BF16)
    decay = jnp.exp(f_last + m_prev - m_new)
    c_new = decay * c_prev + _dot_tn(v * w_wide, k)
    n_new = decay * n_prev + _dot_tn(w_wide[:, 0:LANES], k)[0:1, :]
    return hh, c_new, n_new, m_new


def _ml_out_gate(hh, xc, z, gn, sk):
    mu = jnp.mean(hh, axis=-1, keepdims=True)
    hc = hh - mu
    var = jnp.mean(hc * hc, axis=-1, keepdims=True)
    hn = hc * lax.rsqrt(var + EPS) * gn
    return ((hn + sk * xc) * _silu(z)).astype(BF16)


_ML_WEIGHTS = ["ml_conv_w", "ml_conv_b", "ml_wq", "ml_wk", "ml_wv", "ml_w_gates", "ml_b_gates"]
_ML_OUT_WEIGHTS = ["ml_norm", "ml_skip", "ml_w_out"]


def _ml_core_kernel(*refs, aliased):
    if aliased:
        refs = refs[1:]
    (x_ref, sh_ref, sc_ref, gt_ref, nw_ref, win_ref, cw_ref, cb_ref, wq_ref, wk_ref, wv_ref, wg_ref, bg_ref,
     gn_ref, sk_ref, wo_ref, xo_ref, co_ref, no_ref, mo_ref, cvo_ref,
     ext_scr, q_scr, k_scr, v_scr, xc_scr, xm_ref, z_ref) = refs
    tm = x_ref.shape[0]
    causal, eye = _causal(tm)
    tril = jnp.where(causal, 1.0, 0.0).astype(BF16)

    h = (_rms(x_ref[...], nw_ref[...]) * (1.0 + _mod_rows(sc_ref, tm)) + _mod_rows(sh_ref, tm)).astype(BF16)
    xm_ref[...] = jnp.dot(h, win_ref[:, :ML_INNER], preferred_element_type=F32)
    z_ref[...] = jnp.dot(h, win_ref[:, ML_INNER:], preferred_element_type=F32)

    @pl.when(pl.program_id(1) == 0)
    def _():
        ext_scr[:, 0:HALO, :] = jnp.zeros((1, HALO, ML_INNER), F32)
        co_ref[...] = jnp.zeros_like(co_ref)
        no_ref[...] = jnp.zeros_like(no_ref)
        mo_ref[...] = jnp.zeros_like(mo_ref)

    ext_scr[:, HALO:HALO + tm, :] = xm_ref[...].reshape(1, tm, ML_INNER)
    gl = _ml_qkv_gates(xm_ref, ext_scr, 1, cw_ref, cb_ref, wq_ref, wk_ref, wv_ref, wg_ref, bg_ref,
                       xc_scr, q_scr, k_scr, v_scr)
    cvo_ref[...] = ext_scr[:, CONV_FIRST + tm:HALO + tm, :].reshape(cvo_ref.shape)
    ext_scr[:, 0:HALO, :] = ext_scr[:, tm:tm + HALO, :]

    cum = _dot01(tril, gl)
    mix = None
    for h in range(ML_H):
        cs = slice(h * ML_DH, (h + 1) * ML_DH)
        hh, c_new, n_new, m_new = _mlstm_head(
            q_scr[:, cs], k_scr[:, cs], v_scr[:, cs], gl[:, h:h + 1], cum[:, ML_H + h:ML_H + h + 1],
            co_ref[h], no_ref[h:h + 1, :], mo_ref[0:1, h:h + 1], causal, eye)
        co_ref[h] = c_new
        no_ref[h:h + 1, :] = n_new
        mo_ref[0:1, h:h + 1] = m_new
        p = jnp.dot(_ml_out_gate(hh, xc_scr[:, cs], z_ref[:, cs], gn_ref[:, cs], sk_ref[:, cs]), wo_ref[cs, :],
                    preferred_element_type=F32)
        mix = p if mix is None else mix + p
    xo_ref[...] = x_ref[...] + _mod_rows(gt_ref, tm) * mix


def _ml_core(x, mod, l, o, w, tm, c_stack):
    g_n, t_n, _ = x.shape
    c_spec = pl.BlockSpec((None, None, ML_H, ML_DH, ML_DH), lambda g, i: (o, g, 0, 0, 0))
    names = ["ml_w_in"] + _ML_WEIGHTS + _ML_OUT_WEIGHTS
    in_specs = ([_tok_spec(tm, D_MODEL)] + [_mod_spec(mod, l, j, tm, t_n) for j in (3, 4, 5)]
                + [_layer_spec(w["mix_norm"], l)] + [_layer_spec(w[n], o) for n in names])
    args = [x, mod, mod, mod, w["mix_norm"]] + [w[n] for n in names]
    aliases = {}
    if c_stack is not None:
        aliases = {0: 1}
        in_specs.insert(0, pl.BlockSpec(memory_space=pl.ANY))
        args.insert(0, c_stack)
    return pl.pallas_call(
        functools.partial(_ml_core_kernel, aliased=c_stack is not None),
        grid=(g_n, t_n // tm),
        in_specs=in_specs,
        out_specs=[_tok_spec(tm, D_MODEL), c_spec,
                   pl.BlockSpec((None, ML_H, ML_DH), lambda g, i: (g, 0, 0)),
                   pl.BlockSpec((None, 1, ML_H), lambda g, i: (g, 0, 0)),
                   pl.BlockSpec((None, ML_CONV - 1, ML_INNER), lambda g, i: (g, 0, 0))],
        out_shape=[jax.ShapeDtypeStruct(x.shape, F32),
                   jax.ShapeDtypeStruct((DEPTH // 2, g_n, ML_H, ML_DH, ML_DH), F32),
                   jax.ShapeDtypeStruct((g_n, ML_H, ML_DH), F32),
                   jax.ShapeDtypeStruct((g_n, 1, ML_H), F32),
                   jax.ShapeDtypeStruct((g_n, ML_CONV - 1, ML_INNER), F32)],
        scratch_shapes=[pltpu.VMEM((1, HALO + tm, ML_INNER), F32)] + [pltpu.VMEM((tm, ML_INNER), BF16)] * 3
        + [pltpu.VMEM((tm, ML_INNER), F32)] * 3,
        input_output_aliases=aliases,
        compiler_params=_params("parallel", "arbitrary"),
        name="ml_core",
    )(*args)


def _ml_pre_kernel(xm_ref, c0_ref, cw_ref, cb_ref, wq_ref, wk_ref, wv_ref, wg_ref, bg_ref,
                   xc_ref, q_ref, k_ref, v_ref, gl_ref, co_ref, ext_scr, *, nb):
    tm = xm_ref.shape[0]
    ls = tm // nb
    ext_scr[:, CONV_FIRST:HALO, :] = c0_ref[...]
    ext_scr[:, HALO:HALO + ls, :] = xm_ref[...].reshape(nb, ls, ML_INNER)
    gl_ref[...] = _ml_qkv_gates(xm_ref, ext_scr, nb, cw_ref, cb_ref, wq_ref, wk_ref, wv_ref, wg_ref, bg_ref,
                                xc_ref, q_ref, k_ref, v_ref)
    co_ref[...] = ext_scr[:, CONV_FIRST + ls:HALO + ls, :]


def _ml_pre(xm, c0, o, w, nb):
    g_n, tm, _ = xm.shape
    c_spec = pl.BlockSpec((None,) + c0.shape[1:], lambda g, i: (g, 0, 0, 0))
    big = jax.ShapeDtypeStruct((g_n, tm, ML_INNER), F32)
    return pl.pallas_call(
        functools.partial(_ml_pre_kernel, nb=nb),
        grid=(g_n, 1),
        in_specs=[_tok_spec(tm, ML_INNER), c_spec] + [_layer_spec(w[n], o) for n in _ML_WEIGHTS],
        out_specs=[_tok_spec(tm, ML_INNER)] * 4 + [_tok_spec(tm, LANES), c_spec],
        out_shape=[big] * 4 + [jax.ShapeDtypeStruct((g_n, tm, LANES), F32), jax.ShapeDtypeStruct(c0.shape, F32)],
        scratch_shapes=[pltpu.VMEM((nb, HALO + tm // nb, ML_INNER), F32)],
        compiler_params=_params("parallel", "arbitrary"),
        name="ml_pre",
    )(xm, c0, *[w[n] for n in _ML_WEIGHTS])


def _ml_scan_kernel(*refs, aliased):
    if aliased:
        refs = refs[1:]
    (q_ref, k_ref, v_ref, gl_ref, c0_ref, n0_ref, m0_ref, hh_ref, co_ref, no_ref, mo_ref) = refs
    n_seq = c0_ref.shape[0]
    n_rows = q_ref.shape[0] // n_seq
    causal, eye = _causal(n_rows)
    tril = jnp.where(causal, 1.0, 0.0).astype(BF16)
    for s in range(n_seq):
        rows = slice(s * n_rows, (s + 1) * n_rows)
        gl = gl_ref[rows, :]
        cum = _dot01(tril, gl)
        for h in range(ML_H):
            cs = slice(h * ML_DH, (h + 1) * ML_DH)
            hh, c_new, n_new, m_new = _mlstm_head(
                q_ref[rows, cs], k_ref[rows, cs], v_ref[rows, cs], gl[:, h:h + 1], cum[:, ML_H + h:ML_H + h + 1],
                c0_ref[s, h], n0_ref[s, h:h + 1, :], m0_ref[s, 0:1, h:h + 1], causal, eye)
            hh_ref[rows, cs] = hh
            co_ref[s, h] = c_new
            no_ref[s, h:h + 1, :] = n_new
            mo_ref[s, 0:1, h:h + 1] = m_new


def _ml_scan(q, k, v, gl, state, o, c_stack, n_seq):
    g_n, tm, _ = q.shape
    b_n = g_n * n_seq
    c_spec = pl.BlockSpec((None, n_seq, ML_H, ML_DH, ML_DH), lambda g, i: (o, g, 0, 0, 0))
    n_spec = pl.BlockSpec((None, n_seq, ML_H, ML_DH), lambda g, i: (o, g, 0, 0))
    m_spec = pl.BlockSpec((None, n_seq, 1, ML_H), lambda g, i: (o, g, 0, 0))
    in_specs = [_tok_spec(tm, ML_INNER)] * 3 + [_tok_spec(tm, LANES), c_spec, n_spec, m_spec]
    args = [q, k, v, gl, *state]
    aliases = {}
    if c_stack is not None:
        aliases = {0: 1}
        in_specs.insert(0, pl.BlockSpec(memory_space=pl.ANY))
        args.insert(0, c_stack)
    return pl.pallas_call(
        functools.partial(_ml_scan_kernel, aliased=c_stack is not None),
        grid=(g_n, 1),
        in_specs=in_specs,
        out_specs=[_tok_spec(tm, ML_INNER), c_spec,
                   pl.BlockSpec((n_seq, ML_H, ML_DH), lambda g, i: (g, 0, 0)),
                   pl.BlockSpec((n_seq, 1, ML_H), lambda g, i: (g, 0, 0))],
        out_shape=[jax.ShapeDtypeStruct((g_n, tm, ML_INNER), F32),
                   jax.ShapeDtypeStruct((DEPTH // 2, b_n, ML_H, ML_DH, ML_DH), F32),
                   jax.ShapeDtypeStruct((b_n, ML_H, ML_DH), F32),
                   jax.ShapeDtypeStruct((b_n, 1, ML_H), F32)],
        input_output_aliases=aliases,
        compiler_params=_params("parallel", "arbitrary"),
        name="ml_scan",
    )(*args)


def _ml_post_kernel(x_ref, gt_ref, hh_ref, xc_ref, z_ref, gn_ref, sk_ref, wo_ref, xo_ref):
    tm = x_ref.shape[0]
    outs = []
    for h in range(ML_H):
        cs = slice(h * ML_DH, (h + 1) * ML_DH)
        outs.append(_ml_out_gate(hh_ref[:, cs], xc_ref[:, cs], z_ref[:, cs], gn_ref[:, cs], sk_ref[:, cs]))
    mix = jnp.dot(jnp.concatenate(outs, axis=1), wo_ref[...], preferred_element_type=F32)
    xo_ref[...] = x_ref[...] + _mod_rows(gt_ref, tm) * mix


def _ml_post(x, mod, l, o, w, hh, xc, z, tm):
    g_n, t_n, _ = x.shape
    return pl.pallas_call(
        _ml_post_kernel,
        grid=(g_n, t_n // tm),
        in_specs=[_tok_spec(tm, D_MODEL), _mod_spec(mod, l, 5, tm, t_n)] + [_tok_spec(tm, ML_INNER)] * 3
        + [_layer_spec(w[n], o) for n in _ML_OUT_WEIGHTS],
        out_specs=_tok_spec(tm, D_MODEL),
        out_shape=jax.ShapeDtypeStruct(x.shape, F32),
        compiler_params=_params("parallel", "parallel"),
        name="ml_post",
    )(x, mod, hh, xc, z, *[w[n] for n in _ML_OUT_WEIGHTS])


def _pad_heads(a):
    lead = a.shape[:-1]
    a = a.reshape(lead + (GLA_H, GLA_DK))
    a = jnp.pad(a, [(0, 0)] * len(lead) + [(0, 0), (0, LANES - GLA_DK)])
    return a.reshape(lead + (HP,))


def _block_diag_dense(wb):
    rows = wb.reshape(wb.shape[0], -1, MXU_DIM, ML_BLOCK)
    dense = jnp.tile(rows, (1, 1, 1, MXU_DIM // ML_BLOCK))
    r_blk = lax.broadcasted_iota(jnp.int32, (MXU_DIM, MXU_DIM), 0) // ML_BLOCK
    c_blk = lax.broadcasted_iota(jnp.int32, (MXU_DIM, MXU_DIM), 1) // ML_BLOCK
    return jnp.where(r_blk == c_blk, dense, 0.0).astype(BF16)


def _prepare_weights(p):
    w = {}
    w["ffn_norm"] = p["ffn_norm"].reshape(DEPTH, 2, 1, D_MODEL)
    for name in ("ffn_w_gate", "ffn_w_up", "ffn_w_down"):
        w[name] = p[name]
    w["mix_norm"] = p["mix_norm"].reshape(DEPTH, 1, D_MODEL)
    w_in = p["ab_w_in"]
    qk = GLA_H * GLA_DK
    vw = GLA_H * GLA_DV
    o_k, o_v, o_g, o_a = qk, 2 * qk, 2 * qk + vw, 2 * qk + 2 * vw
    o_u = o_a + GLA_LOWRANK
    o_vb = o_u + GMLP_W
    w_a = jnp.pad(w_in[:, :, o_a:o_u], ((0, 0), (0, 0), (0, LANES - GLA_LOWRANK)))
    w["ab_w_in"] = jnp.concatenate(
        [_pad_heads(w_in[:, :, :o_k]), _pad_heads(w_in[:, :, o_k:o_v]), w_in[:, :, o_v:o_g], w_in[:, :, o_g:o_a],
         w_a, w_in[:, :, o_u:o_vb], w_in[:, :, o_vb:]], axis=-1).astype(BF16)
    w["gla_w_a2"] = jnp.pad(_pad_heads(p["gla_w_a2"]), ((0, 0), (0, LANES - GLA_LOWRANK), (0, 0))).astype(BF16)
    w["gla_b_a"] = _pad_heads(p["gla_b_a"])[:, None, :]
    w["gla_norm"] = p["gla_norm"][:, None, :]
    w["gmlp_norm"] = p["gmlp_norm"][:, None, :]
    w["ab_w_out"] = p["ab_w_out"].astype(BF16)
    w["ml_w_in"] = p["ml_w_in"].astype(BF16)
    w["ml_conv_w"] = p["ml_conv_w"]
    w["ml_conv_b"] = p["ml_conv_b"][:, None, :]
    w["ml_wq"] = _block_diag_dense(p["ml_wq"])
    w["ml_wk"] = _block_diag_dense(p["ml_wk"])
    w["ml_wv"] = _block_diag_dense(p["ml_wv"])
    w["ml_w_gates"] = jnp.pad(p["ml_w_gates"], ((0, 0), (0, 0), (0, LANES - 2 * ML_H))).astype(BF16)
    w["ml_b_gates"] = jnp.pad(p["ml_b_gates"], ((0, 0), (0, LANES - 2 * ML_H)))[:, None, :]
    w["ml_norm"] = p["ml_norm"][:, None, :]
    w["ml_skip"] = p["ml_skip"][:, None, :]
    w["ml_w_out"] = p["ml_w_out"].astype(BF16)
    w["final_norm"] = p["final_norm"][None, :]
    return w


def _spatial_weights(ws, bs, seq_len):
    if seq_len % GMLP_CHUNK == 0:
        length = GMLP_CHUNK
    else:
        length = seq_len
    reps = GMLP_CHUNK // length
    wt = jnp.tril(ws[:, :, :length, :length])
    eye = jnp.eye(reps, dtype=ws.dtype)
    wt = jnp.einsum("egts,ab->egatbs", wt, eye).reshape(ws.shape[0], GMLP_G, GMLP_CHUNK, GMLP_CHUNK)
    bt = jnp.tile(bs[:, :, :length], (1, 1, reps))
    bt = jnp.broadcast_to(bt[:, :, :, None], (ws.shape[0], GMLP_G, GMLP_CHUNK, GMLP_DG))
    return wt.astype(BF16), bt


def _ffn_step(x, mod, l, s, w, ffn_bf16, fmod):
    g_n, t_n, _ = x.shape
    tm = min(FFN_ROW_TILE, t_n)
    names = ("ffn_w_gate", "ffn_w_up", "ffn_w_down")
    if (l, s) not in ffn_bf16:
        ffn_bf16[(l, s)] = tuple(w[n][l, s].astype(BF16) for n in names)
    nxt = (l, 1) if s == 0 else (l + 1, 0)
    steps = g_n * (t_n // tm)
    slab_ok = all(w[n].shape[2] % (steps * 2 * SUBLANES) == 0 for n in names)
    cast = nxt if (nxt[0] < DEPTH and nxt not in ffn_bf16 and slab_ok) else None
    y, converted = _ffn(x, mod, l, s, w, ffn_bf16[(l, s)], tm, fmod, cast)
    if cast is not None:
        ffn_bf16[nxt] = converted
    return y


def _trunk(x, mod, fmod, w, ws, bs, states, ffn_bf16, *, tm, mix_tm, gla_chunk, scan_tm, seq_rows):
    fresh = states is None
    g_n, t_n, _ = x.shape
    tm = min(tm, t_n)
    v_out, n_out, m_out, conv_out = [], [], [], []
    s_stack = c_stack = None
    for l in range(DEPTH):
        x = _ffn_step(x, mod, l, 0, w, ffn_bf16, None)
        if l % 2 == 0:
            e = l // 2
            proj = _ab_in(x, mod, l, e, w, min(FFN_ROW_TILE, t_n))
            if fresh:
                x, s_stack = _ab_mix(x, mod, l, e, w, proj, None, mix_tm, gla_chunk, ws, bs, s_stack)
            else:
                v_out.append(proj[6])
                nseq = mix_tm // seq_rows
                view = lambda a: a.reshape(-1, mix_tm, a.shape[-1])
                s_all = states["gla_S"]
                s0 = s_all.reshape(s_all.shape[0], -1, nseq, GLA_H, GLA_DK, GLA_DV)
                mod_v = mod.reshape(mod.shape[0], mod.shape[1], -1, nseq, D_MODEL)
                xv, s_stack = _ab_mix(view(x), mod_v, l, e, w, [view(a) for a in proj], s0, mix_tm, gla_chunk, ws, bs,
                                      s_stack)
                x = xv.reshape(g_n, t_n, D_MODEL)
        else:
            o = l // 2
            if fresh:
                x, c_stack, n_new, m_new, conv_new = _ml_core(x, mod, l, o, w, scan_tm, c_stack)
                m_new = m_new.reshape(g_n, ML_H)
            else:
                xm, z = _ml_in(x, mod, l, o, w, tm)
                nseq = LANES // seq_rows
                rows = nseq * seq_rows
                view = lambda a, r: a.reshape(-1, r, a.shape[-1])
                c0 = states["ml_conv"][o].reshape(-1, nseq, ML_CONV - 1, ML_INNER)
                xc, q, k, v, gl, conv_new = _ml_pre(view(xm, rows), c0, o, w, nseq)
                conv_new = conv_new.reshape(-1, ML_CONV - 1, ML_INNER)
                st = (states["ml_C"], states["ml_n"], states["ml_m"][:, :, None, :])
                srows = SCAN_SEQS * seq_rows
                hh, c_stack, n_new, m_new = _ml_scan(view(q, srows), view(k, srows), view(v, srows),
                                                     view(gl, srows), st, o, c_stack, SCAN_SEQS)
                m_new = m_new.reshape(-1, ML_H)
                xc = xc.reshape(g_n, t_n, ML_INNER)
                hh = hh.reshape(g_n, t_n, ML_INNER)
                x = _ml_post(x, mod, l, o, w, hh, xc, z, min(tm, 2 * ML_CHUNK))
            n_out.append(n_new)
            m_out.append(m_new)
            conv_out.append(conv_new)
        x = _ffn_step(x, mod, l, 1, w, ffn_bf16, fmod if l == DEPTH - 1 else None)
    y = x
    s_stack = s_stack.reshape(s_stack.shape[0], -1, GLA_H, GLA_DK, GLA_DV)
    v_stack = jnp.stack(v_out) if v_out else None
    return (y, s_stack, v_stack, c_stack, jnp.stack(n_out), jnp.stack(m_out),
            jnp.stack(conv_out))


def kernel(x_prompt, x_sample, c_prompt, c_sample, state_gla_S, state_mlstm_C, state_mlstm_n, state_mlstm_m, state_mlstm_conv, ada_w, ada_b, ffn_norm, ffn_w_gate, ffn_w_up, ffn_w_down, mix_norm, ab_w_in, gla_w_a2, gla_b_a, gla_norm, gmlp_norm, gmlp_ws, gmlp_bs, ab_w_out, ml_w_in, ml_conv_w, ml_conv_b, ml_wq, ml_wk, ml_wv, ml_w_gates, ml_b_gates, ml_norm, ml_skip, ml_w_out, final_norm, final_ada_w, final_ada_b):
    p = dict(ffn_norm=ffn_norm, ffn_w_gate=ffn_w_gate, ffn_w_up=ffn_w_up, ffn_w_down=ffn_w_down, mix_norm=mix_norm,
             ab_w_in=ab_w_in, gla_w_a2=gla_w_a2, gla_b_a=gla_b_a, gla_norm=gla_norm, gmlp_norm=gmlp_norm,
             ab_w_out=ab_w_out, ml_w_in=ml_w_in, ml_conv_w=ml_conv_w, ml_conv_b=ml_conv_b, ml_wq=ml_wq, ml_wk=ml_wk,
             ml_wv=ml_wv, ml_w_gates=ml_w_gates, ml_b_gates=ml_b_gates, ml_norm=ml_norm, ml_skip=ml_skip,
             ml_w_out=ml_w_out, final_norm=final_norm)
    w = _prepare_weights(p)
    n_p, t_p, _ = x_prompt.shape
    n_s, t_s, _ = x_sample.shape

    c_all = jnp.concatenate([c_prompt, c_sample], axis=0)
    mod = _ada(c_all, ada_w, ada_b)
    fmod = _ada(c_all, final_ada_w[None], final_ada_b[None])

    def split_mod(m):
        return m[:, :, :n_p, None, :], m[:, :, None, n_p:, :]

    mod_p, mod_s = split_mod(mod)
    fmod_p, fmod_s = split_mod(fmod)

    ws_p, bs_p = _spatial_weights(gmlp_ws, gmlp_bs, t_p)
    ws_s, bs_s = _spatial_weights(gmlp_ws, gmlp_bs, t_s)

    ffn_bf16 = {}
    y_p, s_p, _, c_p, n_p_, m_p, cv_p = _trunk(
        x_prompt, mod_p, fmod_p, w, ws_p, bs_p, None, ffn_bf16, tm=ROW_TILE, mix_tm=ROW_TILE,
        gla_chunk=_chunk_len(t_p, GLA_TILE_CHUNK, GLA_CHUNK), scan_tm=_chunk_len(t_p, ML_TILE_CHUNK, ML_CHUNK),
        seq_rows=t_p)

    states = dict(gla_S=state_gla_S, ml_C=state_mlstm_C, ml_n=state_mlstm_n, ml_m=state_mlstm_m,
                  ml_conv=state_mlstm_conv)
    xs = x_sample.reshape(1, n_s * t_s, D_MODEL)
    y_s, s_s, v_s, c_s, n_s_, m_s, cv_s = _trunk(
        xs, mod_s, fmod_s, w, ws_s, bs_s, states, ffn_bf16, tm=ROW_TILE, mix_tm=GMLP_CHUNK, gla_chunk=t_s, scan_tm=t_s,
        seq_rows=t_s)
    y_s = y_s.reshape(n_s, t_s, D_MODEL)
    v_s = v_s.reshape(-1, n_s, t_s, GMLP_W)
    return (y_p, y_s, s_p, s_s, v_s, c_p, c_s, n_p_, n_s_, m_p, m_s, cv_p, cv_s)
```

```python
import functools

import jax
import jax.numpy as jnp
from jax import lax
from jax.experimental import pallas as pl
from jax.experimental.pallas import tpu as pltpu

F32 = jnp.float32
BF16 = jnp.bfloat16
EPS = 1e-6

D_MODEL = 1024
DEPTH = 4
D_FF = 2816
GLA_H = 4
GLA_DK = 64
GLA_DV = 128
GLA_LOWRANK = 16
GLA_TAU = 16.0
GLA_CHUNK = 64
GMLP_G = 4
GMLP_DG = 128
GMLP_W = GMLP_G * GMLP_DG
GMLP_CHUNK = 128
ML_INNER = 2 * D_MODEL
ML_H = 4
ML_DH = ML_INNER // ML_H
ML_CONV = 4
ML_BLOCK = 4
ML_CHUNK = 128

LANES = 128
SUBLANES = 8
MXU_DIM = 256
VMEM_LIMIT_BYTES = 56 * 1024 * 1024

ROW_TILE = 512
GLA_TILE_CHUNK = 128
ML_TILE_CHUNK = 256
FFN_ROW_TILE = 1024
SCAN_SEQS = 2


def _chunk_len(t, preferred, nominal):
    for c in (preferred, nominal):
        if t % c == 0:
            return c
    return t


HP = GLA_H * LANES
AB_Q, AB_K, AB_V, AB_G = 0, HP, 2 * HP, 3 * HP
AB_A = 4 * HP
AB_U = AB_A + LANES
AB_VB = AB_U + GMLP_W


def _dot(a, b):
    return jnp.dot(a.astype(BF16), b.astype(BF16), preferred_element_type=F32)


def _dot_nt(a, b):
    return lax.dot_general(a.astype(BF16), b.astype(BF16), (((1,), (1,)), ((), ())), preferred_element_type=F32)


def _dot_tn(a, b):
    return lax.dot_general(a.astype(BF16), b.astype(BF16), (((0,), (0,)), ((), ())), preferred_element_type=F32)


def _dot01(sel, x):
    hi = x.astype(BF16)
    r1 = x - hi.astype(F32)
    mid = r1.astype(BF16)
    lo = (r1 - mid.astype(F32)).astype(BF16)
    return (jnp.dot(sel, hi, preferred_element_type=F32) + jnp.dot(sel, mid, preferred_element_type=F32)
            + jnp.dot(sel, lo, preferred_element_type=F32))


def _rms(x, g):
    return x * lax.rsqrt(jnp.mean(x * x, axis=-1, keepdims=True) + EPS) * g


def _silu(x):
    return x * jax.nn.sigmoid(x)


def _log_sigmoid(x):
    return jnp.minimum(x, 0.0) - jnp.log1p(jnp.exp(-jnp.abs(x)))


def _causal(n):
    row = lax.broadcasted_iota(jnp.int32, (n, n), 0)
    col = lax.broadcasted_iota(jnp.int32, (n, n), 1)
    return row >= col, row == col


def _mod_rows(ref, tm):
    m = ref[...]
    r = m.shape[0]
    if r == 1 or r == tm:
        return m
    rep = tm // r
    if rep == SUBLANES:
        return jnp.broadcast_to(m[:, None, :], (r, rep, m.shape[1])).reshape(tm, m.shape[1])
    row = lax.broadcasted_iota(jnp.int32, (tm, r), 0)
    lo = lax.broadcasted_iota(jnp.int32, (tm, r), 1) * rep
    sel = jnp.where(row >= lo, jnp.where(row < lo + rep, 1.0, 0.0), 0.0).astype(BF16)
    return _dot01(sel, m)


def _col_to_row(col, eye):
    return jnp.sum(jnp.where(eye, col, 0.0), axis=0, keepdims=True)


def _tok_spec(tm, w):
    return pl.BlockSpec((None, tm, w), lambda g, i: (g, i, 0))


def _mod_spec(mod, l, j, tm, t_n):
    t_mod = mod.shape[3]
    if t_mod == 1:
        return pl.BlockSpec((None, None, None, 1, D_MODEL), lambda g, i: (l, j, g, 0, 0))
    return pl.BlockSpec((None, None, None, t_mod * tm // t_n, D_MODEL), lambda g, i: (l, j, g, i, 0))


def _layer_spec(a, *lead):
    n = len(lead)
    shape = (None,) * n + tuple(a.shape[n:])
    zeros = (0,) * (a.ndim - n)
    return pl.BlockSpec(shape, lambda g, i: tuple(lead) + zeros, pipeline_mode=pl.Buffered(1))


def _params(*sem):
    return pltpu.CompilerParams(dimension_semantics=sem, vmem_limit_bytes=VMEM_LIMIT_BYTES)


def _ada_kernel(c_ref, w_ref, b_ref, o_ref):
    cs = _silu(c_ref[...])
    o_ref[...] = _dot(cs, w_ref[...]) + b_ref[...]


def _ada(c, w, b):
    n_l, _, width = w.shape
    n_j = width // D_MODEL
    r = c.shape[0]
    return pl.pallas_call(
        _ada_kernel,
        grid=(n_l, n_j),
        in_specs=[pl.BlockSpec((r, D_MODEL), lambda l, j: (0, 0)),
                  pl.BlockSpec((None, D_MODEL, D_MODEL), lambda l, j: (l, 0, j)),
                  pl.BlockSpec((None, None, 1, D_MODEL), lambda l, j: (l, j, 0, 0))],
        out_specs=pl.BlockSpec((None, None, r, D_MODEL), lambda l, j: (l, j, 0, 0)),
        out_shape=jax.ShapeDtypeStruct((n_l, n_j, r, D_MODEL), F32),
        compiler_params=_params("arbitrary", "arbitrary"),
        name="ada",
    )(c, w, b.reshape(n_l, n_j, 1, D_MODEL))


FFN_CHUNK = 1 * MXU_DIM
FFN_DOWN_GROUP = 11


def _ffn_kernel(*refs, final, cast):
    x_ref, sh_ref, sc_ref, gt_ref, nw_ref, wg_ref, wu_ref, wd_ref = refs[:8]
    n_in = 8 + (3 if final else 0) + (3 if cast else 0)
    o_ref = refs[n_in]
    x = x_ref[...]
    tm = x.shape[0]
    h = (_rms(x, nw_ref[...]) * (1.0 + _mod_rows(sc_ref, tm)) + _mod_rows(sh_ref, tm)).astype(BF16)
    acc = None
    pieces = []
    starts = list(range(0, D_FF, FFN_CHUNK))
    for n, f0 in enumerate(starts):
        f1 = min(f0 + FFN_CHUNK, D_FF)
        g = jnp.dot(h, wg_ref[:, f0:f1], preferred_element_type=F32)
        u = jnp.dot(h, wu_ref[:, f0:f1], preferred_element_type=F32)
        pieces.append((_silu(g) * u).astype(BF16))
        if len(pieces) == FFN_DOWN_GROUP or n == len(starts) - 1:
            lo = f1 - sum(p.shape[1] for p in pieces)
            a = pieces[0] if len(pieces) == 1 else jnp.concatenate(pieces, axis=1)
            p = jnp.dot(a, wd_ref[lo:f1, :], preferred_element_type=F32)
            acc = p if acc is None else acc + p
            pieces = []
    y = x + (0.5 * _mod_rows(gt_ref, tm)) * acc
    if final:
        fsh_ref, fsc_ref, fnw_ref = refs[8:11]
        y = _rms(y, fnw_ref[...]) * (1.0 + _mod_rows(fsc_ref, tm)) + _mod_rows(fsh_ref, tm)
    o_ref[...] = y
    if cast:
        for src, dst in zip(refs[n_in - 3:n_in], refs[n_in + 1:n_in + 4]):
            dst[...] = src[...].astype(BF16)


def _ffn(x, mod, l, s, w, ffn_w, tm, fmod=None, cast=None):
    g_n, t_n, _ = x.shape
    n_i = t_n // tm
    j0 = 6 * s
    whole = lambda a: pl.BlockSpec(a.shape, lambda g, i: (0, 0), pipeline_mode=pl.Buffered(1))
    in_specs = [_tok_spec(tm, D_MODEL), _mod_spec(mod, l, j0, tm, t_n), _mod_spec(mod, l, j0 + 1, tm, t_n),
                _mod_spec(mod, l, j0 + 2, tm, t_n), _layer_spec(w["ffn_norm"], l, s)] + [whole(a) for a in ffn_w]
    args = [x, mod, mod, mod, w["ffn_norm"], *ffn_w]
    out_specs = [_tok_spec(tm, D_MODEL)]
    out_shape = [jax.ShapeDtypeStruct(x.shape, F32)]
    if fmod is not None:
        in_specs += [_mod_spec(fmod, 0, 0, tm, t_n), _mod_spec(fmod, 0, 1, tm, t_n),
                     pl.BlockSpec((1, D_MODEL), lambda g, i: (0, 0))]
        args += [fmod, fmod, w["final_norm"]]
    if cast is not None:
        l2, s2 = cast
        steps = g_n * n_i
        for name in ("ffn_w_gate", "ffn_w_up", "ffn_w_down"):
            src = w[name]
            rows, cols = src.shape[2:]
            in_specs.append(pl.BlockSpec((None, None, rows // steps, cols), lambda g, i: (l2, s2, g * n_i + i, 0)))
            args.append(src)
            out_specs.append(pl.BlockSpec((rows // steps, cols), lambda g, i: (g * n_i + i, 0)))
            out_shape.append(jax.ShapeDtypeStruct((rows, cols), BF16))
    out = pl.pallas_call(
        functools.partial(_ffn_kernel, final=fmod is not None, cast=cast is not None),
        grid=(g_n, n_i),
        in_specs=in_specs,
        out_specs=out_specs,
        out_shape=out_shape,
        compiler_params=_params("parallel", "parallel"),
        name="ffn",
    )(*args)
    return out[0], tuple(out[1:])


def _ab_project(h, w_ref, wa2_ref, ba_ref, vn_ref, q_ref, k_ref, v_ref, g_ref, la_ref, u_ref, vb_ref):
    def seg(c0, width):
        return jnp.dot(h, w_ref[:, c0:c0 + width], preferred_element_type=F32)

    xa = _dot(seg(AB_A, LANES), wa2_ref[...]) + ba_ref[...]
    la_ref[...] = _log_sigmoid(xa) * (1.0 / GLA_TAU)
    vb = seg(AB_VB, GMLP_W)
    for gi in range(GMLP_G):
        cs = slice(gi * GMLP_DG, (gi + 1) * GMLP_DG)
        vb_ref[:, cs] = _rms(vb[:, cs], vn_ref[...])
    q_ref[...] = seg(AB_Q, HP) * GLA_DK ** -0.5
    k_ref[...] = seg(AB_K, HP)
    v_ref[...] = seg(AB_V, HP)
    g_ref[...] = seg(AB_G, HP)
    u_ref[...] = seg(AB_U, GMLP_W)


def _ab_in_kernel(x_ref, sh_ref, sc_ref, nw_ref, w_ref, wa2_ref, ba_ref, vn_ref, *out_refs):
    tm = x_ref.shape[0]
    h = (_rms(x_ref[...], nw_ref[...]) * (1.0 + _mod_rows(sc_ref, tm)) + _mod_rows(sh_ref, tm)).astype(BF16)
    _ab_project(h, w_ref, wa2_ref, ba_ref, vn_ref, *out_refs)


def _ab_in(x, mod, l, e, w, tm):
    g_n, t_n, _ = x.shape
    out = jax.ShapeDtypeStruct((g_n, t_n, HP), F32)
    return pl.pallas_call(
        _ab_in_kernel,
        grid=(g_n, t_n // tm),
        in_specs=[_tok_spec(tm, D_MODEL), _mod_spec(mod, l, 3, tm, t_n), _mod_spec(mod, l, 4, tm, t_n),
                  _layer_spec(w["mix_norm"], l), _layer_spec(w["ab_w_in"], e), _layer_spec(w["gla_w_a2"], e),
                  _layer_spec(w["gla_b_a"], e), _layer_spec(w["gmlp_norm"], e)],
        out_specs=[_tok_spec(tm, HP)] * 7,
        out_shape=[out] * 7,
        compiler_params=_params("parallel", "parallel"),
        name="ab_in",
    )(x, mod, mod, w["mix_norm"], w["ab_w_in"], w["gla_w_a2"], w["gla_b_a"], w["gmlp_norm"])


def _gla_chunk_intra(q, k, v, la, tril, causal, seq):
    n, width = q.shape
    b = _dot01(tril, la)
    if seq == n:
        bm = b[n // 2 - 1:n // 2, :]
        bl = b[n - 1:n, :]
        qd = (q * jnp.exp(b)).astype(BF16)
        qe = (q * jnp.exp(b - bm)).astype(BF16)
        ke = (k * jnp.exp(bm - b)).astype(BF16)
    else:
        assert seq == SUBLANES
        last = b.reshape(n // seq, seq, width)[:, seq - 1:seq, :]
        bl = jnp.broadcast_to(last, (n // seq, seq, width)).reshape(n, width)
        qd = q * jnp.exp(b)
        qe = qd.astype(BF16)
        ke = (k * jnp.exp(-b)).astype(BF16)
    kd = k * jnp.exp(bl - b)
    vb = v.astype(BF16)
    o_intra = []
    for h in range(GLA_H):
        cs = slice(h * LANES, (h + 1) * LANES)
        a = jnp.where(causal, _dot_nt(qe[:, cs], ke[:, cs]), 0.0)
        o_intra.append(_dot(a, vb[:, cs]))
    if seq == n:
        kd = kd.astype(BF16)
    return o_intra, qd, kd, vb, jnp.exp(bl)


def _ab_mix_kernel(*refs, chunk, carry, aliased):
    if aliased:
        refs = refs[1:]
    if carry:
        (x_ref, gt_ref, q_ref, k_ref, v_ref, g_ref, la_ref, u_ref, vb_ref, gn_ref, ws_ref, bs_ref, wo_ref,
         xo_ref, so_ref, o_scr, st_scr) = refs
    else:
        (x_ref, gt_ref, q_ref, k_ref, v_ref, g_ref, la_ref, u_ref, vb_ref, gn_ref, ws_ref, bs_ref, wo_ref, s0_ref,
         xo_ref, so_ref, o_scr, qd_scr, kd_scr, dec_scr) = refs
    tm = q_ref.shape[0]

    if carry:
        causal, _ = _causal(chunk)
        tril = jnp.where(causal, 1.0, 0.0).astype(BF16)

        @pl.when(pl.program_id(1) == 0)
        def _():
            st_scr[...] = jnp.zeros_like(st_scr)

        n_chunks = tm // chunk
        intra = [_gla_chunk_intra(q_ref[c * chunk:(c + 1) * chunk, :], k_ref[c * chunk:(c + 1) * chunk, :],
                                  v_ref[c * chunk:(c + 1) * chunk, :], la_ref[c * chunk:(c + 1) * chunk, :],
                                  tril, causal, chunk) for c in range(n_chunks)]
        for h in range(GLA_H):
            cs = slice(h * LANES, (h + 1) * LANES)
            st = st_scr[h]
            for c in range(n_chunks):
                o_intra, qd, kd, vb, dec = intra[c]
                o_scr[c * chunk:(c + 1) * chunk, cs] = o_intra[h] + _dot_nt(qd[:, cs], st)
                st = st * dec[:, cs] + _dot_tn(vb[:, cs], kd[:, cs])
            st_scr[h] = st

        @pl.when(pl.program_id(1) == pl.num_programs(1) - 1)
        def _():
            for h in range(GLA_H):
                so_ref[h] = st_scr[h].T[:GLA_DK, :]
    else:
        row = lax.broadcasted_iota(jnp.int32, (tm, tm), 0)
        col = lax.broadcasted_iota(jnp.int32, (tm, tm), 1)
        causal = jnp.logical_and(row // chunk == col // chunk, row >= col)
        tril = jnp.where(causal, 1.0, 0.0).astype(BF16)
        o_intra, qd, kd, _, dec = _gla_chunk_intra(q_ref[...], k_ref[...], v_ref[...], la_ref[...], tril, causal, chunk)
        for h in range(GLA_H):
            o_scr[:, h * LANES:(h + 1) * LANES] = o_intra[h]
        qd_scr[...] = qd
        kd_scr[...] = kd
        dec_scr[...] = dec
        key_pad = jnp.zeros((LANES - GLA_DK, GLA_DV), F32)

        def seq_body(c, _):
            r0 = pl.multiple_of(c * chunk, chunk)
            rows = pl.ds(r0, chunk)
            for h in range(GLA_H):
                cs = slice(h * LANES, (h + 1) * LANES)
                st = jnp.concatenate([s0_ref[c, h], key_pad], axis=0).T
                o_scr[rows, cs] = o_scr[rows, cs] + _dot_nt(qd_scr[rows, cs], st)
                st_new = st * dec_scr[pl.ds(r0, 1), cs] + _dot_tn(v_ref[rows, cs], kd_scr[rows, cs])
                so_ref[c, h] = st_new.T[:GLA_DK, :]
            return 0

        lax.fori_loop(0, tm // chunk, seq_body, 0, unroll=4)

    pieces = []
    for h in range(GLA_H):
        cs = slice(h * GLA_DV, (h + 1) * GLA_DV)
        pieces.append((_rms(o_scr[:, cs], gn_ref[...]) * _silu(g_ref[:, cs])).astype(BF16))
    for gi in range(GMLP_G):
        cs = slice(gi * GMLP_DG, (gi + 1) * GMLP_DG)
        zs = []
        for r0 in range(0, tm, GMLP_CHUNK):
            zs.append(_dot(ws_ref[gi], vb_ref[r0:r0 + GMLP_CHUNK, cs]) + bs_ref[gi])
        z = zs[0] if len(zs) == 1 else jnp.concatenate(zs, axis=0)
        pieces.append((u_ref[:, cs] * z).astype(BF16))
    mix = jnp.dot(jnp.concatenate(pieces, axis=1), wo_ref[...], preferred_element_type=F32)
    xo_ref[...] = x_ref[...] + _mod_rows(gt_ref, tm) * mix


def _ab_mix(x, mod, l, e, w, proj, s0, tm, chunk, ws, bs, s_stack):
    g_n, t_n, _ = x.shape
    carry = s0 is None
    in_specs = ([_tok_spec(tm, D_MODEL), _mod_spec(mod, l, 5, tm, t_n)] + [_tok_spec(tm, HP)] * 7
                + [_layer_spec(w["gla_norm"], e), _layer_spec(ws, e), _layer_spec(bs, e), _layer_spec(w["ab_w_out"], e)])
    args = [x, mod, *proj, w["gla_norm"], ws, bs, w["ab_w_out"]]
    scratch = [pltpu.VMEM((tm, HP), F32)]
    if carry:
        s_shape = (DEPTH - DEPTH // 2, g_n, GLA_H, GLA_DK, GLA_DV)
        s_spec = pl.BlockSpec((None, None, GLA_H, GLA_DK, GLA_DV), lambda g, i: (e, g, 0, 0, 0))
        scratch.append(pltpu.VMEM((GLA_H, LANES, LANES), F32))
    else:
        s_shape = s0.shape
        s_spec = pl.BlockSpec((None, None) + s0.shape[2:], lambda g, i: (e, g, 0, 0, 0, 0))
        in_specs.append(s_spec)
        args.append(s0)
        scratch += [pltpu.VMEM((tm, HP), F32)] * 3
    aliases = {}
    if s_stack is not None:
        aliases = {0: 1}
        in_specs.insert(0, pl.BlockSpec(memory_space=pl.ANY))
        args.insert(0, s_stack)
    return pl.pallas_call(
        functools.partial(_ab_mix_kernel, chunk=chunk, carry=carry, aliased=s_stack is not None),
        grid=(g_n, t_n // tm),
        in_specs=in_specs,
        out_specs=[_tok_spec(tm, D_MODEL), s_spec],
        out_shape=[jax.ShapeDtypeStruct(x.shape, F32), jax.ShapeDtypeStruct(s_shape, F32)],
        scratch_shapes=scratch,
        input_output_aliases=aliases,
        compiler_params=_params("parallel", "arbitrary"),
        name="ab_mix",
    )(*args)


def _ml_in_kernel(x_ref, sh_ref, sc_ref, nw_ref, w_ref, xm_ref, z_ref):
    tm = x_ref.shape[0]
    h = (_rms(x_ref[...], nw_ref[...]) * (1.0 + _mod_rows(sc_ref, tm)) + _mod_rows(sh_ref, tm)).astype(BF16)
    xm_ref[...] = jnp.dot(h, w_ref[:, :ML_INNER], preferred_element_type=F32)
    z_ref[...] = jnp.dot(h, w_ref[:, ML_INNER:], preferred_element_type=F32)


def _ml_in(x, mod, l, o, w, tm):
    g_n, t_n, _ = x.shape
    out = jax.ShapeDtypeStruct((g_n, t_n, ML_INNER), F32)
    return pl.pallas_call(
        _ml_in_kernel,
        grid=(g_n, t_n // tm),
        in_specs=[_tok_spec(tm, D_MODEL), _mod_spec(mod, l, 3, tm, t_n), _mod_spec(mod, l, 4, tm, t_n),
                  _layer_spec(w["mix_norm"], l), _layer_spec(w["ml_w_in"], o)],
        out_specs=[_tok_spec(tm, ML_INNER)] * 2,
        out_shape=[out] * 2,
        compiler_params=_params("parallel", "parallel"),
        name="ml_in",
    )(x, mod, mod, w["mix_norm"], w["ml_w_in"])


HALO = SUBLANES
CONV_FIRST = HALO - (ML_CONV - 1)


def _ml_qkv_gates(xm_ref, ext_scr, nb, cw_ref, cb_ref, wq_ref, wk_ref, wv_ref, wg_ref, bg_ref,
                  xc_ref, q_ref, k_ref, v_ref):
    tm = xm_ref.shape[0]
    ls = tm // nb
    gates = bg_ref[...]
    for blk in range(ML_INNER // MXU_DIM):
        cs = slice(blk * MXU_DIM, (blk + 1) * MXU_DIM)
        pre = cb_ref[:, cs]
        for j in range(ML_CONV):
            pre = pre + ext_scr[:, CONV_FIRST + j:CONV_FIRST + j + ls, cs].reshape(tm, MXU_DIM) * cw_ref[j:j + 1, cs]
        xc = _silu(pre)
        xc_ref[:, cs] = xc
        xcb = xc.astype(BF16)
        q = jnp.dot(xcb, wq_ref[blk], preferred_element_type=F32).astype(BF16)
        k = jnp.dot(xcb, wk_ref[blk], preferred_element_type=F32).astype(BF16)
        v = jnp.dot(xm_ref[:, cs].astype(BF16), wv_ref[blk], preferred_element_type=F32).astype(BF16)
        gates = (gates + jnp.dot(q, wg_ref[cs, :], preferred_element_type=F32)
                 + jnp.dot(k, wg_ref[ML_INNER + blk * MXU_DIM:ML_INNER + (blk + 1) * MXU_DIM, :],
                           preferred_element_type=F32)
                 + jnp.dot(v, wg_ref[2 * ML_INNER + blk * MXU_DIM:2 * ML_INNER + (blk + 1) * MXU_DIM, :],
                           preferred_element_type=F32))
        q_ref[:, cs] = q.astype(q_ref.dtype)
        k_ref[:, cs] = k.astype(k_ref.dtype)
        v_ref[:, cs] = v.astype(v_ref.dtype)
    lane = lax.broadcasted_iota(jnp.int32, gates.shape, 1)
    return jnp.where(lane < ML_H, gates, _log_sigmoid(gates))


def _mlstm_head(q, k, v, ig_col, f_col, c_prev, n_prev, m_prev, causal, eye):
    q, k, v = q.astype(BF16), k.astype(BF16), v.astype(BF16)
    n_rows = q.shape[0]
    scale = ML_DH ** -0.5
    ig_row = _col_to_row(ig_col, eye)
    f_row = _col_to_row(f_col, eye)
    log_d = jnp.where(causal, f_col - f_row + ig_row, -jnp.inf)
    log_inter = f_col + m_prev
    m_t = jnp.maximum(log_inter, jnp.max(log_d, axis=-1, keepdims=True))
    d = jnp.exp(log_d - m_t) * scale
    w_inter = jnp.exp(log_inter - m_t)
    s = _dot_nt(q, k) * d
    qn = _dot_nt(q, jnp.broadcast_to(n_prev, (SUBLANES, ML_DH)))[:, 0:1]
    num = _dot(s, v) + w_inter * _dot_nt(q, c_prev)
    den = jnp.sum(s, axis=-1, keepdims=True) + w_inter * qn
    hh = num / jnp.maximum(jnp.abs(den), jnp.exp(-m_t))
    m_new = m_t[n_rows - 1:n_rows, :]
    f_last = f_col[n_rows - 1:n_rows, :]
    w_rows = jnp.exp(f_last - f_col + ig_col - m_new) * scale
    w_wide = jnp.broadcast_to(w_rows, (n_rows, ML_DH)).astype(BF16)
    decay = jnp.exp(f_last + m_prev - m_new)
    c_new = decay * c_prev + _dot_tn(v * w_wide, k)
    n_new = decay * n_prev + _dot_tn(w_wide[:, 0:LANES], k)[0:1, :]
    return hh, c_new, n_new, m_new


def _ml_out_gate(hh, xc, z, gn, sk):
    mu = jnp.mean(hh, axis=-1, keepdims=True)
    hc = hh - mu
    var = jnp.mean(hc * hc, axis=-1, keepdims=True)
    hn = hc * lax.rsqrt(var + EPS) * gn
    return ((hn + sk * xc) * _silu(z)).astype(BF16)


_ML_WEIGHTS = ["ml_conv_w", "ml_conv_b", "ml_wq", "ml_wk", "ml_wv", "ml_w_gates", "ml_b_gates"]
_ML_OUT_WEIGHTS = ["ml_norm", "ml_skip", "ml_w_out"]


def _ml_core_kernel(*refs, aliased):
    if aliased:
        refs = refs[1:]
    (x_ref, sh_ref, sc_ref, gt_ref, nw_ref, win_ref, cw_ref, cb_ref, wq_ref, wk_ref, wv_ref, wg_ref, bg_ref,
     gn_ref, sk_ref, wo_ref, xo_ref, co_ref, no_ref, mo_ref, cvo_ref,
     ext_scr, q_scr, k_scr, v_scr, xc_scr, xm_ref, z_ref) = refs
    tm = x_ref.shape[0]
    causal, eye = _causal(tm)
    tril = jnp.where(causal, 1.0, 0.0).astype(BF16)

    h = (_rms(x_ref[...], nw_ref[...]) * (1.0 + _mod_rows(sc_ref, tm)) + _mod_rows(sh_ref, tm)).astype(BF16)
    xm_ref[...] = jnp.dot(h, win_ref[:, :ML_INNER], preferred_element_type=F32)
    z_ref[...] = jnp.dot(h, win_ref[:, ML_INNER:], preferred_element_type=F32)

    @pl.when(pl.program_id(1) == 0)
    def _():
        ext_scr[:, 0:HALO, :] = jnp.zeros((1, HALO, ML_INNER), F32)
        co_ref[...] = jnp.zeros_like(co_ref)
        no_ref[...] = jnp.zeros_like(no_ref)
        mo_ref[...] = jnp.zeros_like(mo_ref)

    ext_scr[:, HALO:HALO + tm, :] = xm_ref[...].reshape(1, tm, ML_INNER)
    gl = _ml_qkv_gates(xm_ref, ext_scr, 1, cw_ref, cb_ref, wq_ref, wk_ref, wv_ref, wg_ref, bg_ref,
                       xc_scr, q_scr, k_scr, v_scr)
    cvo_ref[...] = ext_scr[:, CONV_FIRST + tm:HALO + tm, :].reshape(cvo_ref.shape)
    ext_scr[:, 0:HALO, :] = ext_scr[:, tm:tm + HALO, :]

    cum = _dot01(tril, gl)
    mix = None
    for h in range(ML_H):
        cs = slice(h * ML_DH, (h + 1) * ML_DH)
        hh, c_new, n_new, m_new = _mlstm_head(
            q_scr[:, cs], k_scr[:, cs], v_scr[:, cs], gl[:, h:h + 1], cum[:, ML_H + h:ML_H + h + 1],
            co_ref[h], no_ref[h:h + 1, :], mo_ref[0:1, h:h + 1], causal, eye)
        co_ref[h] = c_new
        no_ref[h:h + 1, :] = n_new
        mo_ref[0:1, h:h + 1] = m_new
        p = jnp.dot(_ml_out_gate(hh, xc_scr[:, cs], z_ref[:, cs], gn_ref[:, cs], sk_ref[:, cs]), wo_ref[cs, :],
                    preferred_element_type=F32)
        mix = p if mix is None else mix + p
    xo_ref[...] = x_ref[...] + _mod_rows(gt_ref, tm) * mix


def _ml_core(x, mod, l, o, w, tm, c_stack):
    g_n, t_n, _ = x.shape
    c_spec = pl.BlockSpec((None, None, ML_H, ML_DH, ML_DH), lambda g, i: (o, g, 0, 0, 0))
    names = ["ml_w_in"] + _ML_WEIGHTS + _ML_OUT_WEIGHTS
    in_specs = ([_tok_spec(tm, D_MODEL)] + [_mod_spec(mod, l, j, tm, t_n) for j in (3, 4, 5)]
                + [_layer_spec(w["mix_norm"], l)] + [_layer_spec(w[n], o) for n in names])
    args = [x, mod, mod, mod, w["mix_norm"]] + [w[n] for n in names]
    aliases = {}
    if c_stack is not None:
        aliases = {0: 1}
        in_specs.insert(0, pl.BlockSpec(memory_space=pl.ANY))
        args.insert(0, c_stack)
    return pl.pallas_call(
        functools.partial(_ml_core_kernel, aliased=c_stack is not None),
        grid=(g_n, t_n // tm),
        in_specs=in_specs,
        out_specs=[_tok_spec(tm, D_MODEL), c_spec,
                   pl.BlockSpec((None, ML_H, ML_DH), lambda g, i: (g, 0, 0)),
                   pl.BlockSpec((None, 1, ML_H), lambda g, i: (g, 0, 0)),
                   pl.BlockSpec((None, ML_CONV - 1, ML_INNER), lambda g, i: (g, 0, 0))],
        out_shape=[jax.ShapeDtypeStruct(x.shape, F32),
                   jax.ShapeDtypeStruct((DEPTH // 2, g_n, ML_H, ML_DH, ML_DH), F32),
                   jax.ShapeDtypeStruct((g_n, ML_H, ML_DH), F32),
                   jax.ShapeDtypeStruct((g_n, 1, ML_H), F32),
                   jax.ShapeDtypeStruct((g_n, ML_CONV - 1, ML_INNER), F32)],
        scratch_shapes=[pltpu.VMEM((1, HALO + tm, ML_INNER), F32)] + [pltpu.VMEM((tm, ML_INNER), BF16)] * 3
        + [pltpu.VMEM((tm, ML_INNER), F32)] * 3,
        input_output_aliases=aliases,
        compiler_params=_params("parallel", "arbitrary"),
        name="ml_core",
    )(*args)


def _ml_pre_kernel(xm_ref, c0_ref, cw_ref, cb_ref, wq_ref, wk_ref, wv_ref, wg_ref, bg_ref,
                   xc_ref, q_ref, k_ref, v_ref, gl_ref, co_ref, ext_scr, *, nb):
    tm = xm_ref.shape[0]
    ls = tm // nb
    ext_scr[:, CONV_FIRST:HALO, :] = c0_ref[...]
    ext_scr[:, HALO:HALO + ls, :] = xm_ref[...].reshape(nb, ls, ML_INNER)
    gl_ref[...] = _ml_qkv_gates(xm_ref, ext_scr, nb, cw_ref, cb_ref, wq_ref, wk_ref, wv_ref, wg_ref, bg_ref,
                                xc_ref, q_ref, k_ref, v_ref)
    co_ref[...] = ext_scr[:, CONV_FIRST + ls:HALO + ls, :]


def _ml_pre(xm, c0, o, w, nb):
    g_n, tm, _ = xm.shape
    c_spec = pl.BlockSpec((None,) + c0.shape[1:], lambda g, i: (g, 0, 0, 0))
    big = jax.ShapeDtypeStruct((g_n, tm, ML_INNER), F32)
    return pl.pallas_call(
        functools.partial(_ml_pre_kernel, nb=nb),
        grid=(g_n, 1),
        in_specs=[_tok_spec(tm, ML_INNER), c_spec] + [_layer_spec(w[n], o) for n in _ML_WEIGHTS],
        out_specs=[_tok_spec(tm, ML_INNER)] * 4 + [_tok_spec(tm, LANES), c_spec],
        out_shape=[big] * 4 + [jax.ShapeDtypeStruct((g_n, tm, LANES), F32), jax.ShapeDtypeStruct(c0.shape, F32)],
        scratch_shapes=[pltpu.VMEM((nb, HALO + tm // nb, ML_INNER), F32)],
        compiler_params=_params("parallel", "arbitrary"),
        name="ml_pre",
    )(xm, c0, *[w[n] for n in _ML_WEIGHTS])


def _ml_scan_kernel(*refs, aliased):
    if aliased:
        refs = refs[1:]
    (q_ref, k_ref, v_ref, gl_ref, c0_ref, n0_ref, m0_ref, hh_ref, co_ref, no_ref, mo_ref) = refs
    n_seq = c0_ref.shape[0]
    n_rows = q_ref.shape[0] // n_seq
    causal, eye = _causal(n_rows)
    tril = jnp.where(causal, 1.0, 0.0).astype(BF16)
    for s in range(n_seq):
        rows = slice(s * n_rows, (s + 1) * n_rows)
        gl = gl_ref[rows, :]
        cum = _dot01(tril, gl)
        for h in range(ML_H):
            cs = slice(h * ML_DH, (h + 1) * ML_DH)
            hh, c_new, n_new, m_new = _mlstm_head(
                q_ref[rows, cs], k_ref[rows, cs], v_ref[rows, cs], gl[:, h:h + 1], cum[:, ML_H + h:ML_H + h + 1],
                c0_ref[s, h], n0_ref[s, h:h + 1, :], m0_ref[s, 0:1, h:h + 1], causal, eye)
            hh_ref[rows, cs] = hh
            co_ref[s, h] = c_new
            no_ref[s, h:h + 1, :] = n_new
            mo_ref[s, 0:1, h:h + 1] = m_new


def _ml_scan(q, k, v, gl, state, o, c_stack, n_seq):
    g_n, tm, _ = q.shape
    b_n = g_n * n_seq
    c_spec = pl.BlockSpec((None, n_seq, ML_H, ML_DH, ML_DH), lambda g, i: (o, g, 0, 0, 0))
    n_spec = pl.BlockSpec((None, n_seq, ML_H, ML_DH), lambda g, i: (o, g, 0, 0))
    m_spec = pl.BlockSpec((None, n_seq, 1, ML_H), lambda g, i: (o, g, 0, 0))
    in_specs = [_tok_spec(tm, ML_INNER)] * 3 + [_tok_spec(tm, LANES), c_spec, n_spec, m_spec]
    args = [q, k, v, gl, *state]
    aliases = {}
    if c_stack is not None:
        aliases = {0: 1}
        in_specs.insert(0, pl.BlockSpec(memory_space=pl.ANY))
        args.insert(0, c_stack)
    return pl.pallas_call(
        functools.partial(_ml_scan_kernel, aliased=c_stack is not None),
        grid=(g_n, 1),
        in_specs=in_specs,
        out_specs=[_tok_spec(tm, ML_INNER), c_spec,
                   pl.BlockSpec((n_seq, ML_H, ML_DH), lambda g, i: (g, 0, 0)),
                   pl.BlockSpec((n_seq, 1, ML_H), lambda g, i: (g, 0, 0))],
        out_shape=[jax.ShapeDtypeStruct((g_n, tm, ML_INNER), F32),
                   jax.ShapeDtypeStruct((DEPTH // 2, b_n, ML_H, ML_DH, ML_DH), F32),
                   jax.ShapeDtypeStruct((b_n, ML_H, ML_DH), F32),
                   jax.ShapeDtypeStruct((b_n, 1, ML_H), F32)],
        input_output_aliases=aliases,
        compiler_params=_params("parallel", "arbitrary"),
        name="ml_scan",
    )(*args)


def _ml_post_kernel(x_ref, gt_ref, hh_ref, xc_ref, z_ref, gn_ref, sk_ref, wo_ref, xo_ref):
    tm = x_ref.shape[0]
    outs = []
    for h in range(ML_H):
        cs = slice(h * ML_DH, (h + 1) * ML_DH)
        outs.append(_ml_out_gate(hh_ref[:, cs], xc_ref[:, cs], z_ref[:, cs], gn_ref[:, cs], sk_ref[:, cs]))
    mix = jnp.dot(jnp.concatenate(outs, axis=1), wo_ref[...], preferred_element_type=F32)
    xo_ref[...] = x_ref[...] + _mod_rows(gt_ref, tm) * mix


def _ml_post(x, mod, l, o, w, hh, xc, z, tm):
    g_n, t_n, _ = x.shape
    return pl.pallas_call(
        _ml_post_kernel,
        grid=(g_n, t_n // tm),
        in_specs=[_tok_spec(tm, D_MODEL), _mod_spec(mod, l, 5, tm, t_n)] + [_tok_spec(tm, ML_INNER)] * 3
        + [_layer_spec(w[n], o) for n in _ML_OUT_WEIGHTS],
        out_specs=_tok_spec(tm, D_MODEL),
        out_shape=jax.ShapeDtypeStruct(x.shape, F32),
        compiler_params=_params("parallel", "parallel"),
        name="ml_post",
    )(x, mod, hh, xc, z, *[w[n] for n in _ML_OUT_WEIGHTS])


def _pad_heads(a):
    lead = a.shape[:-1]
    a = a.reshape(lead + (GLA_H, GLA_DK))
    a = jnp.pad(a, [(0, 0)] * len(lead) + [(0, 0), (0, LANES - GLA_DK)])
    return a.reshape(lead + (HP,))


def _block_diag_dense(wb):
    rows = wb.reshape(wb.shape[0], -1, MXU_DIM, ML_BLOCK)
    dense = jnp.tile(rows, (1, 1, 1, MXU_DIM // ML_BLOCK))
    r_blk = lax.broadcasted_iota(jnp.int32, (MXU_DIM, MXU_DIM), 0) // ML_BLOCK
    c_blk = lax.broadcasted_iota(jnp.int32, (MXU_DIM, MXU_DIM), 1) // ML_BLOCK
    return jnp.where(r_blk == c_blk, dense, 0.0).astype(BF16)


def _prepare_weights(p):
    w = {}
    w["ffn_norm"] = p["ffn_norm"].reshape(DEPTH, 2, 1, D_MODEL)
    for name in ("ffn_w_gate", "ffn_w_up", "ffn_w_down"):
        w[name] = p[name]
    w["mix_norm"] = p["mix_norm"].reshape(DEPTH, 1, D_MODEL)
    w_in = p["ab_w_in"]
    qk = GLA_H * GLA_DK
    vw = GLA_H * GLA_DV
    o_k, o_v, o_g, o_a = qk, 2 * qk, 2 * qk + vw, 2 * qk + 2 * vw
    o_u = o_a + GLA_LOWRANK
    o_vb = o_u + GMLP_W
    w_a = jnp.pad(w_in[:, :, o_a:o_u], ((0, 0), (0, 0), (0, LANES - GLA_LOWRANK)))
    w["ab_w_in"] = jnp.concatenate(
        [_pad_heads(w_in[:, :, :o_k]), _pad_heads(w_in[:, :, o_k:o_v]), w_in[:, :, o_v:o_g], w_in[:, :, o_g:o_a],
         w_a, w_in[:, :, o_u:o_vb], w_in[:, :, o_vb:]], axis=-1).astype(BF16)
    w["gla_w_a2"] = jnp.pad(_pad_heads(p["gla_w_a2"]), ((0, 0), (0, LANES - GLA_LOWRANK), (0, 0))).astype(BF16)
    w["gla_b_a"] = _pad_heads(p["gla_b_a"])[:, None, :]
    w["gla_norm"] = p["gla_norm"][:, None, :]
    w["gmlp_norm"] = p["gmlp_norm"][:, None, :]
    w["ab_w_out"] = p["ab_w_out"].astype(BF16)
    w["ml_w_in"] = p["ml_w_in"].astype(BF16)
    w["ml_conv_w"] = p["ml_conv_w"]
    w["ml_conv_b"] = p["ml_conv_b"][:, None, :]
    w["ml_wq"] = _block_diag_dense(p["ml_wq"])
    w["ml_wk"] = _block_diag_dense(p["ml_wk"])
    w["ml_wv"] = _block_diag_dense(p["ml_wv"])
    w["ml_w_gates"] = jnp.pad(p["ml_w_gates"], ((0, 0), (0, 0), (0, LANES - 2 * ML_H))).astype(BF16)
    w["ml_b_gates"] = jnp.pad(p["ml_b_gates"], ((0, 0), (0, LANES - 2 * ML_H)))[:, None, :]
    w["ml_norm"] = p["ml_norm"][:, None, :]
    w["ml_skip"] = p["ml_skip"][:, None, :]
    w["ml_w_out"] = p["ml_w_out"].astype(BF16)
    w["final_norm"] = p["final_norm"][None, :]
    return w


def _spatial_weights(ws, bs, seq_len):
    if seq_len % GMLP_CHUNK == 0:
        length = GMLP_CHUNK
    else:
        length = seq_len
    reps = GMLP_CHUNK // length
    wt = jnp.tril(ws[:, :, :length, :length])
    eye = jnp.eye(reps, dtype=ws.dtype)
    wt = jnp.einsum("egts,ab->egatbs", wt, eye).reshape(ws.shape[0], GMLP_G, GMLP_CHUNK, GMLP_CHUNK)
    bt = jnp.tile(bs[:, :, :length], (1, 1, reps))
    bt = jnp.broadcast_to(bt[:, :, :, None], (ws.shape[0], GMLP_G, GMLP_CHUNK, GMLP_DG))
    return wt.astype(BF16), bt


def _ffn_step(x, mod, l, s, w, ffn_bf16, fmod):
    g_n, t_n, _ = x.shape
    tm = min(FFN_ROW_TILE, t_n)
    names = ("ffn_w_gate", "ffn_w_up", "ffn_w_down")
    if (l, s) not in ffn_bf16:
        ffn_bf16[(l, s)] = tuple(w[n][l, s].astype(BF16) for n in names)
    nxt = (l, 1) if s == 0 else (l + 1, 0)
    steps = g_n * (t_n // tm)
    slab_ok = all(w[n].shape[2] % (steps * 2 * SUBLANES) == 0 for n in names)
    cast = nxt if (nxt[0] < DEPTH and nxt not in ffn_bf16 and slab_ok) else None
    y, converted = _ffn(x, mod, l, s, w, ffn_bf16[(l, s)], tm, fmod, cast)
    if cast is not None:
        ffn_bf16[nxt] = converted
    return y


def _trunk(x, mod, fmod, w, ws, bs, states, ffn_bf16, *, tm, mix_tm, gla_chunk, scan_tm, seq_rows):
    fresh = states is None
    g_n, t_n, _ = x.shape
    tm = min(tm, t_n)
    v_out, n_out, m_out, conv_out = [], [], [], []
    s_stack = c_stack = None
    for l in range(DEPTH):
        x = _ffn_step(x, mod, l, 0, w, ffn_bf16, None)
        if l % 2 == 0:
            e = l // 2
            proj = _ab_in(x, mod, l, e, w, min(FFN_ROW_TILE, t_n))
            if fresh:
                x, s_stack = _ab_mix(x, mod, l, e, w, proj, None, mix_tm, gla_chunk, ws, bs, s_stack)
            else:
                v_out.append(proj[6])
                nseq = mix_tm // seq_rows
                view = lambda a: a.reshape(-1, mix_tm, a.shape[-1])
                s_all = states["gla_S"]
                s0 = s_all.reshape(s_all.shape[0], -1, nseq, GLA_H, GLA_DK, GLA_DV)
                mod_v = mod.reshape(mod.shape[0], mod.shape[1], -1, nseq, D_MODEL)
                xv, s_stack = _ab_mix(view(x), mod_v, l, e, w, [view(a) for a in proj], s0, mix_tm, gla_chunk, ws, bs,
                                      s_stack)
                x = xv.reshape(g_n, t_n, D_MODEL)
        else:
            o = l // 2
            if fresh:
                x, c_stack, n_new, m_new, conv_new = _ml_core(x, mod, l, o, w, scan_tm, c_stack)
                m_new = m_new.reshape(g_n, ML_H)
            else:
                xm, z = _ml_in(x, mod, l, o, w, tm)
                nseq = LANES // seq_rows
                rows = nseq * seq_rows
                view = lambda a, r: a.reshape(-1, r, a.shape[-1])
                c0 = states["ml_conv"][o].reshape(-1, nseq, ML_CONV - 1, ML_INNER)
                xc, q, k, v, gl, conv_new = _ml_pre(view(xm, rows), c0, o, w, nseq)
                conv_new = conv_new.reshape(-1, ML_CONV - 1, ML_INNER)
                st = (states["ml_C"], states["ml_n"], states["ml_m"][:, :, None, :])
                srows = SCAN_SEQS * seq_rows
                hh, c_stack, n_new, m_new = _ml_scan(view(q, srows), view(k, srows), view(v, srows),
                                                     view(gl, srows), st, o, c_stack, SCAN_SEQS)
                m_new = m_new.reshape(-1, ML_H)
                xc = xc.reshape(g_n, t_n, ML_INNER)
                hh = hh.reshape(g_n, t_n, ML_INNER)
                x = _ml_post(x, mod, l, o, w, hh, xc, z, tm)
            n_out.append(n_new)
            m_out.append(m_new)
            conv_out.append(conv_new)
        x = _ffn_step(x, mod, l, 1, w, ffn_bf16, fmod if l == DEPTH - 1 else None)
    y = x
    s_stack = s_stack.reshape(s_stack.shape[0], -1, GLA_H, GLA_DK, GLA_DV)
    v_stack = jnp.stack(v_out) if v_out else None
    return (y, s_stack, v_stack, c_stack, jnp.stack(n_out), jnp.stack(m_out),
            jnp.stack(conv_out))


def kernel(x_prompt, x_sample, c_prompt, c_sample, state_gla_S, state_mlstm_C, state_mlstm_n, state_mlstm_m, state_mlstm_conv, ada_w, ada_b, ffn_norm, ffn_w_gate, ffn_w_up, ffn_w_down, mix_norm, ab_w_in, gla_w_a2, gla_b_a, gla_norm, gmlp_norm, gmlp_ws, gmlp_bs, ab_w_out, ml_w_in, ml_conv_w, ml_conv_b, ml_wq, ml_wk, ml_wv, ml_w_gates, ml_b_gates, ml_norm, ml_skip, ml_w_out, final_norm, final_ada_w, final_ada_b):
    p = dict(ffn_norm=ffn_norm, ffn_w_gate=ffn_w_gate, ffn_w_up=ffn_w_up, ffn_w_down=ffn_w_down, mix_norm=mix_norm,
             ab_w_in=ab_w_in, gla_w_a2=gla_w_a2, gla_b_a=gla_b_a, gla_norm=gla_norm, gmlp_norm=gmlp_norm,
             ab_w_out=ab_w_out, ml_w_in=ml_w_in, ml_conv_w=ml_conv_w, ml_conv_b=ml_conv_b, ml_wq=ml_wq, ml_wk=ml_wk,
             ml_wv=ml_wv, ml_w_gates=ml_w_gates, ml_b_gates=ml_b_gates, ml_norm=ml_norm, ml_skip=ml_skip,
             ml_w_out=ml_w_out, final_norm=final_norm)
    w = _prepare_weights(p)
    n_p, t_p, _ = x_prompt.shape
    n_s, t_s, _ = x_sample.shape

    c_all = jnp.concatenate([c_prompt, c_sample], axis=0)
    mod = _ada(c_all, ada_w, ada_b)
    fmod = _ada(c_all, final_ada_w[None], final_ada_b[None])

    def split_mod(m):
        return m[:, :, :n_p, None, :], m[:, :, None, n_p:, :]

    mod_p, mod_s = split_mod(mod)
    fmod_p, fmod_s = split_mod(fmod)

    ws_p, bs_p = _spatial_weights(gmlp_ws, gmlp_bs, t_p)
    ws_s, bs_s = _spatial_weights(gmlp_ws, gmlp_bs, t_s)

    ffn_bf16 = {}
    y_p, s_p, _, c_p, n_p_, m_p, cv_p = _trunk(
        x_prompt, mod_p, fmod_p, w, ws_p, bs_p, None, ffn_bf16, tm=ROW_TILE, mix_tm=ROW_TILE,
        gla_chunk=_chunk_len(t_p, GLA_TILE_CHUNK, GLA_CHUNK), scan_tm=_chunk_len(t_p, ML_TILE_CHUNK, ML_CHUNK),
        seq_rows=t_p)

    states = dict(gla_S=state_gla_S, ml_C=state_mlstm_C, ml_n=state_mlstm_n, ml_m=state_mlstm_m,
                  ml_conv=state_mlstm_conv)
    xs = x_sample.reshape(1, n_s * t_s, D_MODEL)
    y_s, s_s, v_s, c_s, n_s_, m_s, cv_s = _trunk(
        xs, mod_s, fmod_s, w, ws_s, bs_s, states, ffn_bf16, tm=ROW_TILE, mix_tm=GMLP_CHUNK, gla_chunk=t_s, scan_tm=t_s,
        seq_rows=t_s)
    y_s = y_s.reshape(n_s, t_s, D_MODEL)
    v_s = v_s.reshape(-1, n_s, t_s, GMLP_W)
    return (y_p, y_s, s_p, s_s, v_s, c_p, c_s, n_p_, n_s_, m_p, m_s, cv_p, cv_s)
```
